```python
import math
import jax, jax.numpy as jnp
from jax import lax
import numpy as np

D_MODEL = 1024
BATCH = 8
SEQ = 2048
DEPTH = 1
DEC_BATCH = 128
DEC_SEQ = 8
PAST_LEN = 16384
PAGE_SIZE = 128

M_HEADS = 4
M_V = (2 * D_MODEL) // M_HEADS
M_QK = M_V // 2
M_QK_W = M_HEADS * M_QK
M_V_W = M_HEADS * M_V
S_INNER = 2 * D_MODEL
S_HEADDIM = 64
S_HEADS = S_INNER // S_HEADDIM
S_GROUPS = 4
S_HPG = S_HEADS // S_GROUPS
S_STATE = 128
S_CONV_CH = S_INNER + 2 * S_GROUPS * S_STATE
CONV_W = 4
D_FF = 2816
CHUNK = 128
EPS = 1e-6
N_MOD = 9
SPLITS = (M_QK_W, M_QK_W, M_V_W, M_V_W, M_HEADS, M_HEADS, S_INNER, S_CONV_CH, S_HEADS, D_MODEL, D_MODEL)
IN_W = 2 * M_QK_W + 2 * M_V_W + 2 * M_HEADS + S_INNER + S_CONV_CH + S_HEADS + 2 * D_MODEL

kernel_name = 'hybrid_mlstm_ssd_macaron_adaln_step'


def _rmsnorm(x, g):
    x32 = x.astype(jnp.float32)
    y = x32 * lax.rsqrt(jnp.mean(x32 * x32, axis=-1, keepdims=True) + EPS)
    return (y * g.astype(jnp.float32)).astype(x.dtype)


def _modulate(xn, shift, scale):
    return xn * (1.0 + scale[:, None, :]) + shift[:, None, :]


def _swiglu(u, w1, w3, w2):
    return (jax.nn.silu(u @ w1) * (u @ w3)) @ w2


def _split_cols(a):
    idx = np.cumsum(np.array(SPLITS))[:-1].tolist()
    return jnp.split(a, idx, axis=-1)


def _causal_conv(x, buf, w, b):
    T = x.shape[1]
    xp = jnp.concatenate([buf, x], axis=1)
    out = xp[:, 0:T] * w[0]
    for j in range(1, CONV_W):
        out = out + xp[:, j:j + T] * w[j]
    return out + b, xp[:, -(CONV_W - 1):]


def _to_chunks(a, L):
    B, T = a.shape[0], a.shape[1]
    return jnp.moveaxis(a.reshape((B, T // L, L) + a.shape[2:]), 1, 0)


def _from_chunks(a):
    a = jnp.moveaxis(a, 0, 1)
    return a.reshape((a.shape[0], a.shape[1] * a.shape[2]) + a.shape[3:])


def _mlstm(q, k, v, i_pre, f_pre, C0, n0, m0):
    T = q.shape[1]
    L = math.gcd(T, CHUNK)
    f32 = jnp.float32
    q = q.astype(f32)
    k = k.astype(f32) * (M_QK ** -0.5)
    v = v.astype(f32)
    ig = i_pre.astype(f32)
    lf = jax.nn.log_sigmoid(f_pre.astype(f32))
    mask = jnp.tril(jnp.ones((L, L), dtype=bool))[None, :, :, None]

    def step(carry, inp):
        C, n, m = carry
        qc, kc, vc, ic, lfc = inp
        b = jnp.cumsum(lfc, axis=1)
        dmat = jnp.where(mask, b[:, :, None, :] - b[:, None, :, :] + ic[:, None, :, :], -jnp.inf)
        inter = b + m[:, None, :]
        mt = jnp.maximum(inter, jnp.max(dmat, axis=2))
        w_tok = jnp.exp(dmat - mt[:, :, None, :])
        w_st = jnp.exp(inter - mt)
        s = jnp.einsum('bthd,bshd->btsh', qc, kc) * w_tok
        num = jnp.einsum('btsh,bshv->bthv', s, vc) + w_st[..., None] * jnp.einsum('bthd,bhdv->bthv', qc, C)
        den = jnp.sum(s, axis=2) + w_st * jnp.einsum('bthd,bhd->bth', qc, n)
        h = num / jnp.maximum(jnp.abs(den), jnp.exp(-mt))[..., None]
        m_new = mt[:, -1]
        b_last = b[:, -1]
        ws = jnp.exp(b_last[:, None, :] - b + ic - m_new[:, None, :])
        decay = jnp.exp(b_last + m - m_new)
        C_new = decay[..., None, None] * C + jnp.einsum('bsh,bshd,bshv->bhdv', ws, kc, vc)
        n_new = decay[..., None] * n + jnp.einsum('bsh,bshd->bhd', ws, kc)
        return (C_new, n_new, m_new), h

    xs = (_to_chunks(q, L), _to_chunks(k, L), _to_chunks(v, L), _to_chunks(ig, L), _to_chunks(lf, L))
    (C1, n1, m1), h = lax.scan(step, (C0.astype(f32), n0.astype(f32), m0.astype(f32)), xs)
    return _from_chunks(h), C1, n1, m1


def _ssd(x, dt, A, Bm, Cm, h0):
    T = x.shape[1]
    L = math.gcd(T, CHUNK)
    mask = jnp.tril(jnp.ones((L, L), dtype=bool))[None, :, :, None, None]

    def step(h, inp):
        xc, dtc, bc, cc = inp
        cum = jnp.cumsum(dtc * A, axis=1)
        lmat = jnp.exp(jnp.where(mask, cum[:, :, None] - cum[:, None, :], -jnp.inf))
        xdt = xc * dtc[..., None]
        cb = jnp.einsum('btgn,bsgn->btsg', cc, bc)
        y = (jnp.einsum('btsg,btsge,bsgep->btgep', cb, lmat, xdt)
             + jnp.exp(cum)[..., None] * jnp.einsum('btgn,bgepn->btgep', cc, h))
        decay = jnp.exp(cum[:, -1:] - cum)
        h_new = (jnp.exp(cum[:, -1])[..., None, None] * h
                 + jnp.einsum('bsgn,bsge,bsgep->bgepn', bc, decay, xdt))
        return h_new, y

    xs = (_to_chunks(x, L), _to_chunks(dt, L), _to_chunks(Bm, L), _to_chunks(Cm, L))
    h1, y = lax.scan(step, h0, xs)
    return _from_chunks(y), h1


def _layer(x, c, st, p):
    C0, n0, m0, mconv0, ssm0, sconv0 = st
    Bsz, T, _ = x.shape
    f32 = jnp.float32
    mod = jax.nn.silu(c) @ p['ada_w'] + p['ada_b']
    sh1, sc1, g1, sh2, sc2, g2, sh3, sc3, g3 = jnp.split(mod, N_MOD, axis=-1)
    u = _modulate(_rmsnorm(x, p['norm_ff1']), sh1, sc1)
    x = x + 0.5 * g1[:, None, :] * _swiglu(u, p['ff1_w1'], p['ff1_w3'], p['ff1_w2'])
    u = _modulate(_rmsnorm(x, p['norm_mix']), sh2, sc2)
    mq, mk, mv, mo, mi, mf, z, xbc, dtp, ga, gb = _split_cols(u @ p['w_in'])
    qk, mconv1 = _causal_conv(jnp.concatenate([mq, mk], axis=-1), mconv0, p['m_conv_w'], p['m_conv_b'])
    qk = jax.nn.silu(qk)
    q = qk[..., :M_QK_W].reshape(Bsz, T, M_HEADS, M_QK)
    k = qk[..., M_QK_W:].reshape(Bsz, T, M_HEADS, M_QK)
    v = mv.reshape(Bsz, T, M_HEADS, M_V)
    i_pre = mi + p['m_if_b'][:M_HEADS]
    f_pre = mf + p['m_if_b'][M_HEADS:]
    hm, C1, n1, m1 = _mlstm(q, k, v, i_pre, f_pre, C0, n0, m0)
    mu = jnp.mean(hm, axis=-1, keepdims=True)
    var = jnp.mean(jnp.square(hm - mu), axis=-1, keepdims=True)
    hm = (hm - mu) * lax.rsqrt(var + EPS) * p['m_head_g'].astype(f32).reshape(M_HEADS, M_V)
    hm = hm * jax.nn.sigmoid(mo.astype(f32)).reshape(Bsz, T, M_HEADS, M_V)
    ya = hm.reshape(Bsz, T, M_V_W).astype(x.dtype) @ p['w_pa']
    xbc, sconv1 = _causal_conv(xbc, sconv0, p['s_conv_w'], p['s_conv_b'])
    xbc = jax.nn.silu(xbc).astype(f32)
    xs = xbc[..., :S_INNER].reshape(Bsz, T, S_GROUPS, S_HPG, S_HEADDIM)
    Bm = xbc[..., S_INNER:S_INNER + S_GROUPS * S_STATE].reshape(Bsz, T, S_GROUPS, S_STATE)
    Cm = xbc[..., S_INNER + S_GROUPS * S_STATE:].reshape(Bsz, T, S_GROUPS, S_STATE)
    dt = jax.nn.softplus((dtp + p['s_dt_bias']).astype(f32)).reshape(Bsz, T, S_GROUPS, S_HPG)
    A = -jnp.exp(p['s_A_log'].astype(f32)).reshape(S_GROUPS, S_HPG)
    h0 = ssm0.astype(f32).reshape(Bsz, S_GROUPS, S_HPG, S_HEADDIM, S_STATE)
    ys, h1 = _ssd(xs, dt, A, Bm, Cm, h0)
    ys = ys + p['s_D'].astype(f32).reshape(S_GROUPS, S_HPG)[..., None] * xs
    gated = ys.reshape(Bsz, T, S_INNER) * jax.nn.silu(z.astype(f32))
    gg = gated.reshape(Bsz, T, S_GROUPS, S_INNER // S_GROUPS)
    gg = gg * lax.rsqrt(jnp.mean(gg * gg, axis=-1, keepdims=True) + EPS)
    gated = gg.reshape(Bsz, T, S_INNER) * p['s_norm_g'].astype(f32)
    yb = gated.astype(x.dtype) @ p['w_pb']
    ssm1 = h1.reshape(Bsz, S_HEADS, S_HEADDIM, S_STATE)
    merged = jax.nn.sigmoid(ga) * ya + jax.nn.sigmoid(gb) * yb
    x = x + g2[:, None, :] * (merged @ p['w_out'])
    u = _modulate(_rmsnorm(x, p['norm_ff2']), sh3, sc3)
    x = x + 0.5 * g3[:, None, :] * _swiglu(u, p['ff2_w1'], p['ff2_w3'], p['ff2_w2'])
    return x, (C1, n1, m1, mconv1, ssm1, sconv1)


def setup_inputs(seed: int = 0) -> dict:
    key = jax.random.key(seed)
    ks = iter(jax.random.split(key, 48))
    f32 = jnp.float32

    def nrm(shape, scale):
        return jax.random.normal(next(ks), shape, f32) * scale

    def gain(shape):
        return 1.0 + nrm(shape, 0.05)

    D = D_MODEL
    dt0 = jnp.exp(jax.random.uniform(next(ks), (DEPTH, S_HEADS), f32, math.log(1e-3), math.log(1e-1)))
    f_bias = jnp.linspace(3.0, 6.0, M_HEADS, dtype=f32)[None, :] + nrm((DEPTH, M_HEADS), 0.1)
    return {
        'x_prompt': nrm((BATCH, SEQ, D), 1.0),
        'x_sample': nrm((DEC_BATCH, DEC_SEQ, D), 1.0),
        'c_prompt': nrm((BATCH, D), 1.0),
        'c_sample': nrm((DEC_BATCH, D), 1.0),
        'state_mlstm_C': nrm((DEPTH, DEC_BATCH, M_HEADS, M_QK, M_V), M_QK ** -0.5),
        'state_mlstm_n': nrm((DEPTH, DEC_BATCH, M_HEADS, M_QK), M_QK ** -0.5),
        'state_mlstm_m': nrm((DEPTH, DEC_BATCH, M_HEADS), 0.5),
        'state_mlstm_conv': nrm((DEPTH, DEC_BATCH, CONV_W - 1, 2 * M_QK_W), 1.0),
        'state_ssm': nrm((DEPTH, DEC_BATCH, S_HEADS, S_HEADDIM, S_STATE), 0.1),
        'state_ssm_conv': nrm((DEPTH, DEC_BATCH, CONV_W - 1, S_CONV_CH), 1.0),
        'ada_w': nrm((DEPTH, D, N_MOD * D), 0.5 * D ** -0.5),
        'ada_b': nrm((DEPTH, N_MOD * D), 0.02),
        'norm_ff1': gain((DEPTH, D)),
        'ff1_w1': nrm((DEPTH, D, D_FF), D ** -0.5),
        'ff1_w3': nrm((DEPTH, D, D_FF), D ** -0.5),
        'ff1_w2': nrm((DEPTH, D_FF, D), D_FF ** -0.5),
        'norm_mix': gain((DEPTH, D)),
        'w_in': nrm((DEPTH, D, IN_W), D ** -0.5),
        'm_conv_w': nrm((DEPTH, CONV_W, 2 * M_QK_W), CONV_W ** -0.5),
        'm_conv_b': nrm((DEPTH, 2 * M_QK_W), 0.02),
        'm_if_b': jnp.concatenate([nrm((DEPTH, M_HEADS), 0.1), f_bias], axis=-1),
        'm_head_g': gain((DEPTH, M_V_W)),
        'w_pa': nrm((DEPTH, M_V_W, D), M_V_W ** -0.5),
        's_conv_w': nrm((DEPTH, CONV_W, S_CONV_CH), CONV_W ** -0.5),
        's_conv_b': nrm((DEPTH, S_CONV_CH), 0.02),
        's_dt_bias': dt0 + jnp.log(-jnp.expm1(-dt0)),
        's_A_log': jnp.log(jax.random.uniform(next(ks), (DEPTH, S_HEADS), f32, 1.0, 16.0)),
        's_D': 1.0 + nrm((DEPTH, S_HEADS), 0.1),
        's_norm_g': gain((DEPTH, S_INNER)),
        'w_pb': nrm((DEPTH, S_INNER, D), S_INNER ** -0.5),
        'w_out': nrm((DEPTH, D, D), D ** -0.5),
        'norm_ff2': gain((DEPTH, D)),
        'ff2_w1': nrm((DEPTH, D, D_FF), D ** -0.5),
        'ff2_w3': nrm((DEPTH, D, D_FF), D ** -0.5),
        'ff2_w2': nrm((DEPTH, D_FF, D), D_FF ** -0.5),
        'final_norm': gain((D,)),
    }


def reference(x_prompt, x_sample, c_prompt, c_sample, state_mlstm_C, state_mlstm_n, state_mlstm_m,
              state_mlstm_conv, state_ssm, state_ssm_conv, ada_w, ada_b, norm_ff1, ff1_w1, ff1_w3,
              ff1_w2, norm_mix, w_in, m_conv_w, m_conv_b, m_if_b, m_head_g, w_pa, s_conv_w, s_conv_b,
              s_dt_bias, s_A_log, s_D, s_norm_g, w_pb, w_out, norm_ff2, ff2_w1, ff2_w3, ff2_w2,
              final_norm):
    f32 = jnp.float32
    Bp = x_prompt.shape[0]
    hp, hs = x_prompt, x_sample
    new_p, new_s = [], []
    for l in range(DEPTH):
        p = {'ada_w': ada_w[l], 'ada_b': ada_b[l], 'norm_ff1': norm_ff1[l], 'ff1_w1': ff1_w1[l],
             'ff1_w3': ff1_w3[l], 'ff1_w2': ff1_w2[l], 'norm_mix': norm_mix[l], 'w_in': w_in[l],
             'm_conv_w': m_conv_w[l], 'm_conv_b': m_conv_b[l], 'm_if_b': m_if_b[l],
             'm_head_g': m_head_g[l], 'w_pa': w_pa[l], 's_conv_w': s_conv_w[l],
             's_conv_b': s_conv_b[l], 's_dt_bias': s_dt_bias[l], 's_A_log': s_A_log[l],
             's_D': s_D[l], 's_norm_g': s_norm_g[l], 'w_pb': w_pb[l], 'w_out': w_out[l],
             'norm_ff2': norm_ff2[l], 'ff2_w1': ff2_w1[l], 'ff2_w3': ff2_w3[l], 'ff2_w2': ff2_w2[l]}
        st_p = (jnp.zeros((Bp, M_HEADS, M_QK, M_V), f32),
                jnp.zeros((Bp, M_HEADS, M_QK), f32),
                jnp.zeros((Bp, M_HEADS), f32),
                jnp.zeros((Bp, CONV_W - 1, 2 * M_QK_W), x_prompt.dtype),
                jnp.zeros((Bp, S_HEADS, S_HEADDIM, S_STATE), f32),
                jnp.zeros((Bp, CONV_W - 1, S_CONV_CH), x_prompt.dtype))
        st_s = (state_mlstm_C[l], state_mlstm_n[l], state_mlstm_m[l], state_mlstm_conv[l],
                state_ssm[l], state_ssm_conv[l])
        hp, sp = _layer(hp, c_prompt, st_p, p)
        hs, ss = _layer(hs, c_sample, st_s, p)
        new_p.append(sp)
        new_s.append(ss)
    y_prompt = _rmsnorm(hp, final_norm)
    y_sample = _rmsnorm(hs, final_norm)
    pC, pn, pm, pmc, pssm, psc = [jnp.stack([s[i] for s in new_p]) for i in range(6)]
    sC, sn, sm, smc, sssm, ssc = [jnp.stack([s[i] for s in new_s]) for i in range(6)]
    return (y_prompt, y_sample, pC, pn, pm, pmc, pssm, psc, sC, sn, sm, smc, sssm, ssc)
```

```python
import functools

import jax
import jax.numpy as jnp
from jax import lax
from jax.experimental import pallas as pl
from jax.experimental.pallas import tpu as pltpu

F32 = jnp.float32
BF16 = jnp.bfloat16

D_MODEL = 1024
M_HEADS = 4
M_QK = 256
M_V = 512
M_QK_W = M_HEADS * M_QK
M_V_W = M_HEADS * M_V
S_INNER = 2048
S_HEADDIM = 64
S_HEADS = 32
S_GROUPS = 4
S_HPG = S_HEADS // S_GROUPS
S_STATE = 128
S_GW = S_INNER // S_GROUPS
S_CONV_CH = S_INNER + 2 * S_GROUPS * S_STATE
CONV_W = 4
D_FF = 2816
CHUNK = 128
EPS = 1e-6
N_MOD = 9

LANES = 128
SUBLANES = 8

P_QK = 0
P_V = 2 * M_QK_W
P_O = P_V + M_V_W
P_Z = P_O + M_V_W
P_XBC = P_Z + S_INNER
P_B = P_XBC + S_INNER
P_C = P_B + S_GROUPS * S_STATE
P_GA = P_XBC + S_CONV_CH
P_GB = P_GA + D_MODEL
P_SMALL = P_GB + D_MODEL
P_WIDTH = 13824
SM_I = 0
SM_F = M_HEADS
SM_DT = 2 * M_HEADS

VMEM_LIMIT = 56 * 1024 * 1024


def _silu(x):
    return x * jax.nn.sigmoid(x)


def _rms(x, g):
    return x * lax.rsqrt(jnp.mean(x * x, axis=-1, keepdims=True) + EPS) * g


def _ada_kernel(c_ref, w_ref, b_ref, o_ref):
    s = _silu(c_ref[...]).astype(BF16)
    o_ref[...] = jnp.dot(s, w_ref[...].astype(BF16), preferred_element_type=F32) + b_ref[...]


def _ada(c_all, ada_w, ada_b):
    m = c_all.shape[0]
    n = ada_w.shape[1]
    tn = 1024
    return pl.pallas_call(
        _ada_kernel,
        grid=(n // tn,),
        in_specs=[
            pl.BlockSpec((m, D_MODEL), lambda j: (0, 0)),
            pl.BlockSpec((D_MODEL, tn), lambda j: (0, j)),
            pl.BlockSpec((1, tn), lambda j: (0, j)),
        ],
        out_specs=pl.BlockSpec((m, tn), lambda j: (0, j)),
        out_shape=jax.ShapeDtypeStruct((m, n), F32),
        compiler_params=pltpu.CompilerParams(
            dimension_semantics=("arbitrary",), vmem_limit_bytes=VMEM_LIMIT),
        name="ada_mod",
    )(c_all, ada_w, ada_b.reshape(1, n))


def _ffn_kernel(x_ref, sh_ref, sc_ref, g_ref, nw_ref, w1_ref, w3_ref, w2_ref, *rest,
                nf, final):
    if final:
        fn_ref, o_ref, u_ref = rest
    else:
        o_ref, u_ref = rest
    f = pl.program_id(1)

    @pl.when(f == 0)
    def _():
        xn = _rms(x_ref[...], nw_ref[...])
        u_ref[...] = (xn * (1.0 + sc_ref[0]) + sh_ref[0]).astype(BF16)
        o_ref[...] = jnp.zeros_like(o_ref)

    u = u_ref[...]
    h1 = jnp.dot(u, w1_ref[...], preferred_element_type=F32)
    h3 = jnp.dot(u, w3_ref[...], preferred_element_type=F32)
    a = (_silu(h1) * h3).astype(BF16)
    o_ref[...] += jnp.dot(a, w2_ref[...], preferred_element_type=F32)

    @pl.when(f == nf - 1)
    def _():
        y = x_ref[...] + (0.5 * g_ref[0]) * o_ref[...]
        if final:
            y = _rms(y, fn_ref[...])
        o_ref[...] = y


def _mod_spec(mod3, rows_per_group, tm, k):
    if mod3.shape[1] == 1:
        tiles_per_group = rows_per_group // tm
        return pl.BlockSpec((1, 1, D_MODEL), lambda i, *_: (i // tiles_per_group, 0, k))
    return pl.BlockSpec((1, tm, D_MODEL), lambda i, *_: (0, i, k))


def _ffn(x, mod3, rows_per_group, kmod, norm_w, w1, w3, w2, final_norm=None):
    n = x.shape[0]
    tm = min(1024, rows_per_group)
    tf = 256
    nf = D_FF // tf
    final = final_norm is not None
    in_specs = [
        pl.BlockSpec((tm, D_MODEL), lambda i, f: (i, 0)),
        _mod_spec(mod3, rows_per_group, tm, kmod),
        _mod_spec(mod3, rows_per_group, tm, kmod + 1),
        _mod_spec(mod3, rows_per_group, tm, kmod + 2),
        pl.BlockSpec((1, D_MODEL), lambda i, f: (0, 0)),
        pl.BlockSpec((D_MODEL, tf), lambda i, f: (0, f)),
        pl.BlockSpec((D_MODEL, tf), lambda i, f: (0, f)),
        pl.BlockSpec((tf, D_MODEL), lambda i, f: (f, 0)),
    ]
    args = [x, mod3, mod3, mod3, norm_w.reshape(1, D_MODEL), w1, w3, w2]
    if final:
        in_specs.append(pl.BlockSpec((1, D_MODEL), lambda i, f: (0, 0)))
        args.append(final_norm.reshape(1, D_MODEL))
    return pl.pallas_call(
        functools.partial(_ffn_kernel, nf=nf, final=final),
        grid=(n // tm, nf),
        in_specs=in_specs,
        out_specs=pl.BlockSpec((tm, D_MODEL), lambda i, f: (i, 0)),
        out_shape=jax.ShapeDtypeStruct((n, D_MODEL), F32),
        scratch_shapes=[pltpu.VMEM((tm, D_MODEL), BF16)],
        compiler_params=pltpu.CompilerParams(
            dimension_semantics=("parallel", "arbitrary"), vmem_limit_bytes=VMEM_LIMIT),
        name="ffn_final" if final else "ffn",
    )(*args)


def _proj_kernel(x_ref, sh_ref, sc_ref, nw_ref, w_ref, o_ref, u_ref):
    @pl.when(pl.program_id(1) == 0)
    def _():
        xn = _rms(x_ref[...], nw_ref[...])
        u_ref[...] = (xn * (1.0 + sc_ref[0]) + sh_ref[0]).astype(BF16)

    o_ref[...] = jnp.dot(u_ref[...], w_ref[...], preferred_element_type=F32)


def _proj(x, mod3, rows_per_group, kmod, norm_w, w_packed):
    n = x.shape[0]
    tm = min(1024, rows_per_group)
    tn = 1536
    return pl.pallas_call(
        _proj_kernel,
        grid=(n // tm, P_WIDTH // tn),
        in_specs=[
            pl.BlockSpec((tm, D_MODEL), lambda i, j: (i, 0)),
            _mod_spec(mod3, rows_per_group, tm, kmod),
            _mod_spec(mod3, rows_per_group, tm, kmod + 1),
            pl.BlockSpec((1, D_MODEL), lambda i, j: (0, 0)),
            pl.BlockSpec((D_MODEL, tn), lambda i, j: (0, j)),
        ],
        out_specs=pl.BlockSpec((tm, tn), lambda i, j: (i, j)),
        out_shape=jax.ShapeDtypeStruct((n, P_WIDTH), F32),
        scratch_shapes=[pltpu.VMEM((tm, D_MODEL), BF16)],
        compiler_params=pltpu.CompilerParams(
            dimension_semantics=("parallel", "arbitrary"), vmem_limit_bytes=VMEM_LIMIT),
        name="mix_proj",
    )(x, mod3, mod3, norm_w.reshape(1, D_MODEL), w_packed)


def _conv_silu(xp_ref, x_ref, w_ref, b_ref, L):
    xp_ref[pl.ds(SUBLANES, L), :] = x_ref[...]
    w = w_ref[...]
    acc = b_ref[...] + xp_ref[pl.ds(SUBLANES - (CONV_W - 1), L), :] * w[0:1, :]
    for j in range(1, CONV_W):
        acc = acc + xp_ref[pl.ds(SUBLANES - (CONV_W - 1) + j, L), :] * w[j:j + 1, :]
    xp_ref[pl.ds(0, SUBLANES), :] = xp_ref[pl.ds(L, SUBLANES), :]
    return _silu(acc)


def _init_conv(xp_ref, init_ref):
    xp_ref[pl.ds(0, SUBLANES), :] = jnp.zeros((SUBLANES, xp_ref.shape[1]), F32)
    if init_ref is not None:
        xp_ref[pl.ds(SUBLANES - (CONV_W - 1), CONV_W - 1), :] = init_ref[0]


def _lane_pick(a, lane, idx):
    return jnp.sum(jnp.where(lane == idx, a, 0.0), axis=1, keepdims=True)


def _to_row(col, eye):
    return jnp.sum(jnp.where(eye, col, 0.0), axis=0, keepdims=True)


def _cumsum_col(row, tril):
    return jnp.sum(jnp.where(tril, row, 0.0), axis=1, keepdims=True)


def _log_sigmoid(x):
    return jnp.minimum(x, 0.0) - jnp.log1p(jnp.exp(-jnp.abs(x)))


def _softplus(x):
    return jnp.maximum(x, 0.0) + jnp.log1p(jnp.exp(-jnp.abs(x)))


def _mlstm_kernel(*refs, L, nc, has_state):
    (qp_ref, kp_ref, v_ref, og_ref, sm_ref, wq_ref, wk_ref, bq_ref, bk_ref,
     sb_ref, hg_ref) = refs[:11]
    refs = refs[11:]
    if has_state:
        qc0_ref, kc0_ref, c0_ref, n0_ref, m0_ref = refs[:5]
        refs = refs[5:]
    hm_ref, c_ref, n_ref, m_ref, xq_ref, xk_ref = refs
    h = pl.program_id(1)
    c = pl.program_id(2)

    @pl.when(c == 0)
    def _():
        if has_state:
            c_ref[0, 0] = c0_ref[0, 0]
            n_ref[0, 0] = n0_ref[0, 0]
            m_ref[0, 0] = m0_ref[0, 0]
            _init_conv(xq_ref, qc0_ref)
            _init_conv(xk_ref, kc0_ref)
        else:
            c_ref[...] = jnp.zeros_like(c_ref)
            n_ref[...] = jnp.zeros_like(n_ref)
            m_ref[...] = jnp.zeros_like(m_ref)
            _init_conv(xq_ref, None)
            _init_conv(xk_ref, None)

    q = _conv_silu(xq_ref, qp_ref, wq_ref, bq_ref, L)
    k = _conv_silu(xk_ref, kp_ref, wk_ref, bk_ref, L) * (M_QK ** -0.5)
    v = v_ref[...]

    sm = sm_ref[...] + sb_ref[...]
    lane = lax.broadcasted_iota(jnp.int32, (L, LANES), 1)
    i_col = _lane_pick(sm, lane, h + SM_I)
    lf_col = _log_sigmoid(_lane_pick(sm, lane, h + SM_F))

    rr = lax.broadcasted_iota(jnp.int32, (L, L), 0)
    cc = lax.broadcasted_iota(jnp.int32, (L, L), 1)
    eye = rr == cc
    tril = cc <= rr
    b_col = _cumsum_col(_to_row(lf_col, eye), tril)
    b_row = _to_row(b_col, eye)
    i_row = _to_row(i_col, eye)

    c_prev = c_ref[0, 0]
    n_prev = n_ref[0, 0]
    m_prev = m_ref[0, 0]

    dmat = jnp.where(tril, b_col - b_row + i_row, -jnp.inf)
    inter = b_col + m_prev
    mt = jnp.maximum(inter, jnp.max(dmat, axis=1, keepdims=True))
    w_tok = jnp.exp(dmat - mt)
    w_st = jnp.exp(inter - mt)

    qb = q.astype(BF16)
    kb = k.astype(BF16)
    vb = v.astype(BF16)
    s = lax.dot_general(qb, kb, (((1,), (1,)), ((), ())), preferred_element_type=F32) * w_tok
    num = (jnp.dot(s.astype(BF16), vb, preferred_element_type=F32)
           + w_st * jnp.dot(qb, c_prev.astype(BF16), preferred_element_type=F32))
    den = (jnp.sum(s, axis=1, keepdims=True)
           + w_st * jnp.sum(q * n_prev, axis=1, keepdims=True))
    hh = num / jnp.maximum(jnp.abs(den), jnp.exp(-mt))

    m_new = mt[L - 1:L, :]
    b_last = b_col[L - 1:L, :]
    ws = jnp.exp(b_last - b_col + i_col - m_new)
    decay = jnp.exp(b_last + m_prev - m_new)
    kw = k * ws
    c_ref[0, 0] = decay * c_prev + lax.dot_general(
        kw.astype(BF16), vb, (((0,), (0,)), ((), ())), preferred_element_type=F32)
    n_ref[0, 0] = decay * n_prev + jnp.sum(kw, axis=0, keepdims=True)
    m_ref[0, 0] = m_new

    mu = jnp.mean(hh, axis=-1, keepdims=True)
    hc = hh - mu
    var = jnp.mean(hc * hc, axis=-1, keepdims=True)
    hm_ref[...] = hc * lax.rsqrt(var + EPS) * hg_ref[...] * jax.nn.sigmoid(og_ref[...])


def _mlstm(proj, bsz, T, m_conv_w, m_conv_b, small_bias, m_head_g, state=None):
    L = min(T, CHUNK)
    nc = T // L
    has_state = state is not None
    kq = M_QK_W // M_QK

    def row(b, h, c):
        return b * nc + c

    in_specs = [
        pl.BlockSpec((L, M_QK), lambda b, h, c: (row(b, h, c), h)),
        pl.BlockSpec((L, M_QK), lambda b, h, c: (row(b, h, c), kq + h)),
        pl.BlockSpec((L, M_V), lambda b, h, c: (row(b, h, c), P_V // M_V + h)),
        pl.BlockSpec((L, M_V), lambda b, h, c: (row(b, h, c), P_O // M_V + h)),
        pl.BlockSpec((L, LANES), lambda b, h, c: (row(b, h, c), P_SMALL // LANES)),
        pl.BlockSpec((CONV_W, M_QK), lambda b, h, c: (0, h)),
        pl.BlockSpec((CONV_W, M_QK), lambda b, h, c: (0, kq + h)),
        pl.BlockSpec((1, M_QK), lambda b, h, c: (0, h)),
        pl.BlockSpec((1, M_QK), lambda b, h, c: (0, kq + h)),
        pl.BlockSpec((1, LANES), lambda b, h, c: (0, 0)),
        pl.BlockSpec((1, M_V), lambda b, h, c: (0, h)),
    ]
    args = [proj, proj, proj, proj, proj, m_conv_w, m_conv_w,
            m_conv_b.reshape(1, -1), m_conv_b.reshape(1, -1), small_bias,
            m_head_g.reshape(1, -1)]
    if has_state:
        conv0, c0, n0, m0 = state
        in_specs += [
            pl.BlockSpec((1, CONV_W - 1, M_QK), lambda b, h, c: (b, 0, h)),
            pl.BlockSpec((1, CONV_W - 1, M_QK), lambda b, h, c: (b, 0, kq + h)),
            pl.BlockSpec((1, 1, M_QK, M_V), lambda b, h, c: (b, h, 0, 0)),
            pl.BlockSpec((1, 1, 1, M_QK), lambda b, h, c: (b, h, 0, 0)),
            pl.BlockSpec((1, 1, 1, 1), lambda b, h, c: (b, h, 0, 0)),
        ]
        args += [conv0, conv0, c0, n0.reshape(bsz, M_HEADS, 1, M_QK),
                 m0.reshape(bsz, M_HEADS, 1, 1)]
    out_shape = (
        jax.ShapeDtypeStruct((bsz * T, M_V_W), F32),
        jax.ShapeDtypeStruct((bsz, M_HEADS, M_QK, M_V), F32),
        jax.ShapeDtypeStruct((bsz, M_HEADS, 1, M_QK), F32),
        jax.ShapeDtypeStruct((bsz, M_HEADS, 1, 1), F32),
    )
    out_specs = (
        pl.BlockSpec((L, M_V), lambda b, h, c: (row(b, h, c), h)),
        pl.BlockSpec((1, 1, M_QK, M_V), lambda b, h, c: (b, h, 0, 0)),
        pl.BlockSpec((1, 1, 1, M_QK), lambda b, h, c: (b, h, 0, 0)),
        pl.BlockSpec((1, 1, 1, 1), lambda b, h, c: (b, h, 0, 0)),
    )
    hm, c1, n1, m1 = pl.pallas_call(
        functools.partial(_mlstm_kernel, L=L, nc=nc, has_state=has_state),
        grid=(bsz, M_HEADS, nc),
        in_specs=in_specs,
        out_specs=out_specs,
        out_shape=out_shape,
        scratch_shapes=[pltpu.VMEM((L + SUBLANES, M_QK), F32),
                        pltpu.VMEM((L + SUBLANES, M_QK), F32)],
        compiler_params=pltpu.CompilerParams(
            dimension_semantics=("parallel", "parallel", "arbitrary"),
            vmem_limit_bytes=VMEM_LIMIT),
        name="mlstm_state" if has_state else "mlstm",
    )(*args)
    return hm, c1, n1.reshape(bsz, M_HEADS, M_QK), m1.reshape(bsz, M_HEADS)


def _ssd_kernel(*refs, L, nc, has_state):
    (z_ref, xs_ref, bp_ref, cp_ref, sm_ref, wx_ref, wb_ref, wc_ref, bx_ref, bb_ref, bc_ref,
     sb_ref, al_ref, d_ref, ng_ref) = refs[:15]
    refs = refs[15:]
    if has_state:
        x0_ref, b0_ref, c0_ref, h0_ref = refs[:4]
        refs = refs[4:]
    y_ref, h_ref, xx_ref, xb_ref, xc_ref = refs
    g = pl.program_id(1)
    c = pl.program_id(2)

    @pl.when(c == 0)
    def _():
        if has_state:
            h_ref[0, 0] = h0_ref[0, 0]
            _init_conv(xx_ref, x0_ref)
            _init_conv(xb_ref, b0_ref)
            _init_conv(xc_ref, c0_ref)
        else:
            h_ref[...] = jnp.zeros_like(h_ref)
            _init_conv(xx_ref, None)
            _init_conv(xb_ref, None)
            _init_conv(xc_ref, None)

    xs = _conv_silu(xx_ref, xs_ref, wx_ref, bx_ref, L)
    bm = _conv_silu(xb_ref, bp_ref, wb_ref, bb_ref, L)
    cm = _conv_silu(xc_ref, cp_ref, wc_ref, bc_ref, L)

    dt_all = _softplus(sm_ref[...] + sb_ref[...])
    da_all = dt_all * (-jnp.exp(al_ref[...]))
    lane = lax.broadcasted_iota(jnp.int32, (L, LANES), 1)
    rr = lax.broadcasted_iota(jnp.int32, (L, L), 0)
    cc = lax.broadcasted_iota(jnp.int32, (L, L), 1)
    eye = rr == cc
    tril = cc <= rr
    lo = lane < S_HEADDIM

    bmb = bm.astype(BF16)
    cmb = cm.astype(BF16)
    cb = lax.dot_general(cmb, bmb, (((1,), (1,)), ((), ())), preferred_element_type=F32)

    lmats, dt_cols, dec_cols, ecum_cols, elast = [], [], [], [], []
    for e in range(S_HPG):
        idx = SM_DT + g * S_HPG + e
        dt_col = _lane_pick(dt_all, lane, idx)
        a_col = _lane_pick(da_all, lane, idx)
        cum_col = _cumsum_col(_to_row(a_col, eye), tril)
        cum_row = _to_row(cum_col, eye)
        cum_last = cum_col[L - 1:L, :]
        lmats.append(jnp.exp(jnp.where(tril, cum_col - cum_row, -jnp.inf)))
        dt_cols.append(dt_col)
        dec_cols.append(dt_col * jnp.exp(cum_last - cum_col))
        ecum_cols.append(jnp.exp(cum_col))
        elast.append(jnp.exp(cum_last))

    def expand(cols):
        return jnp.concatenate(
            [jnp.where(lo, cols[2 * p], cols[2 * p + 1]) for p in range(S_HPG // 2)], axis=1)

    xdt = xs * expand(dt_cols)
    xdd = xs * expand(dec_cols)

    h_prev = h_ref[0, 0]
    y_parts = []
    for p in range(S_HPG // 2):
        xpair = xdt[:, p * LANES:(p + 1) * LANES]
        x_lo = jnp.where(lo, xpair, 0.0).astype(BF16)
        x_hi = jnp.where(lo, 0.0, xpair).astype(BF16)
        y_parts.append(
            jnp.dot((cb * lmats[2 * p]).astype(BF16), x_lo, preferred_element_type=F32)
            + jnp.dot((cb * lmats[2 * p + 1]).astype(BF16), x_hi, preferred_element_type=F32))
    y = jnp.concatenate(y_parts, axis=1)
    y = y + expand(ecum_cols) * lax.dot_general(
        cmb, h_prev.astype(BF16), (((1,), (1,)), ((), ())), preferred_element_type=F32)

    hrow = lax.broadcasted_iota(jnp.int32, (S_GW, 1), 0) // S_HEADDIM
    hdec = jnp.zeros((S_GW, 1), F32)
    for e in range(S_HPG):
        hdec = jnp.where(hrow == e, elast[e], hdec)
    h_ref[0, 0] = hdec * h_prev + lax.dot_general(
        xdd.astype(BF16), bmb, (((0,), (0,)), ((), ())), preferred_element_type=F32)

    y = y + d_ref[...] * xs
    gated = y * _silu(z_ref[...])
    y_ref[...] = _rms(gated, ng_ref[...])


def _ssd(proj, bsz, T, s_conv_w, s_conv_b, small_bias, alog_row, d_row, s_norm_g, state=None):
    L = min(T, CHUNK)
    nc = T // L
    has_state = state is not None
    cb_b = S_INNER // S_STATE
    cb_c = cb_b + S_GROUPS

    def row(b, g, c):
        return b * nc + c

    in_specs = [
        pl.BlockSpec((L, S_GW), lambda b, g, c: (row(b, g, c), P_Z // S_GW + g)),
        pl.BlockSpec((L, S_GW), lambda b, g, c: (row(b, g, c), P_XBC // S_GW + g)),
        pl.BlockSpec((L, S_STATE), lambda b, g, c: (row(b, g, c), P_B // S_STATE + g)),
        pl.BlockSpec((L, S_STATE), lambda b, g, c: (row(b, g, c), P_C // S_STATE + g)),
        pl.BlockSpec((L, LANES), lambda b, g, c: (row(b, g, c), P_SMALL // LANES)),
        pl.BlockSpec((CONV_W, S_GW), lambda b, g, c: (0, g)),
        pl.BlockSpec((CONV_W, S_STATE), lambda b, g, c: (0, cb_b + g)),
        pl.BlockSpec((CONV_W, S_STATE), lambda b, g, c: (0, cb_c + g)),
        pl.BlockSpec((1, S_GW), lambda b, g, c: (0, g)),
        pl.BlockSpec((1, S_STATE), lambda b, g, c: (0, cb_b + g)),
        pl.BlockSpec((1, S_STATE), lambda b, g, c: (0, cb_c + g)),
        pl.BlockSpec((1, LANES), lambda b, g, c: (0, 0)),
        pl.BlockSpec((1, LANES), lambda b, g, c: (0, 0)),
        pl.BlockSpec((1, S_GW), lambda b, g, c: (0, g)),
        pl.BlockSpec((1, S_GW), lambda b, g, c: (0, g)),
    ]
    cbias = s_conv_b.reshape(1, -1)
    args = [proj, proj, proj, proj, proj, s_conv_w, s_conv_w, s_conv_w, cbias, cbias, cbias,
            small_bias, alog_row, d_row, s_norm_g.reshape(1, -1)]
    if has_state:
        conv0, h0 = state
        in_specs += [
            pl.BlockSpec((1, CONV_W - 1, S_GW), lambda b, g, c: (b, 0, g)),
            pl.BlockSpec((1, CONV_W - 1, S_STATE), lambda b, g, c: (b, 0, cb_b + g)),
            pl.BlockSpec((1, CONV_W - 1, S_STATE), lambda b, g, c: (b, 0, cb_c + g)),
            pl.BlockSpec((1, 1, S_GW, S_STATE), lambda b, g, c: (b, g, 0, 0)),
        ]
        args += [conv0, conv0, conv0, h0.reshape(bsz, S_GROUPS, S_GW, S_STATE)]
    y, h1 = pl.pallas_call(
        functools.partial(_ssd_kernel, L=L, nc=nc, has_state=has_state),
        grid=(bsz, S_GROUPS, nc),
        in_specs=in_specs,
        out_specs=(
            pl.BlockSpec((L, S_GW), lambda b, g, c: (row(b, g, c), g)),
            pl.BlockSpec((1, 1, S_GW, S_STATE), lambda b, g, c: (b, g, 0, 0)),
        ),
        out_shape=(
            jax.ShapeDtypeStruct((bsz * T, S_INNER), F32),
            jax.ShapeDtypeStruct((bsz, S_GROUPS, S_GW, S_STATE), F32),
        ),
        scratch_shapes=[pltpu.VMEM((L + SUBLANES, S_GW), F32),
                        pltpu.VMEM((L + SUBLANES, S_STATE), F32),
                        pltpu.VMEM((L + SUBLANES, S_STATE), F32)],
        compiler_params=pltpu.CompilerParams(
            dimension_semantics=("parallel", "parallel", "arbitrary"),
            vmem_limit_bytes=VMEM_LIMIT),
        name="ssd_state" if has_state else "ssd",
    )(*args)
    return y, h1.reshape(bsz, S_HEADS, S_HEADDIM, S_STATE)


def _merge_kernel(x_ref, hm_ref, gt_ref, ga_ref, gb_ref, g_ref, wpa_ref, wpb_ref, wo_ref, o_ref):
    ya = jnp.dot(hm_ref[...].astype(BF16), wpa_ref[...], preferred_element_type=F32)
    yb = jnp.dot(gt_ref[...].astype(BF16), wpb_ref[...], preferred_element_type=F32)
    merged = jax.nn.sigmoid(ga_ref[...]) * ya + jax.nn.sigmoid(gb_ref[...]) * yb
    o_ref[...] = x_ref[...] + g_ref[0] * jnp.dot(
        merged.astype(BF16), wo_ref[...], preferred_element_type=F32)


def _merge(x, hm, gated, proj, mod3, rows_per_group, kmod, w_pa, w_pb, w_out):
    n = x.shape[0]
    tm = 256
    const = lambda i: (0, 0)
    return pl.pallas_call(
        _merge_kernel,
        grid=(n // tm,),
        in_specs=[
            pl.BlockSpec((tm, D_MODEL), lambda i: (i, 0)),
            pl.BlockSpec((tm, M_V_W), lambda i: (i, 0)),
            pl.BlockSpec((tm, S_INNER), lambda i: (i, 0)),
            pl.BlockSpec((tm, D_MODEL), lambda i: (i, P_GA // D_MODEL)),
            pl.BlockSpec((tm, D_MODEL), lambda i: (i, P_GB // D_MODEL)),
            _mod_spec(mod3, rows_per_group, tm, kmod),
            pl.BlockSpec((M_V_W, D_MODEL), const),
            pl.BlockSpec((S_INNER, D_MODEL), const),
            pl.BlockSpec((D_MODEL, D_MODEL), const),
        ],
        out_specs=pl.BlockSpec((tm, D_MODEL), lambda i: (i, 0)),
        out_shape=jax.ShapeDtypeStruct((n, D_MODEL), F32),
        compiler_params=pltpu.CompilerParams(
            dimension_semantics=("parallel",), vmem_limit_bytes=VMEM_LIMIT),
        name="merge",
    )(x, hm, gated, proj, proj, mod3, w_pa, w_pb, w_out)


def _layer(x, mod3, rows_per_group, bsz, T, wts, final_norm, state):
    n = bsz * T
    x = x.reshape(n, D_MODEL)
    x = _ffn(x, mod3, rows_per_group, 0, wts["norm_ff1"], wts["ff1_w1"], wts["ff1_w3"],
             wts["ff1_w2"])
    proj = _proj(x, mod3, rows_per_group, 3, wts["norm_mix"], wts["w_packed"])
    if state is None:
        m_state = s_state = None
    else:
        c0, n0, m0, mconv0, ssm0, sconv0 = state
        m_state = (mconv0, c0, n0, m0)
        s_state = (sconv0, ssm0)
    hm, c1, n1, m1 = _mlstm(proj, bsz, T, wts["m_conv_w"], wts["m_conv_b"], wts["small_bias"],
                            wts["m_head_g"], m_state)
    gated, h1 = _ssd(proj, bsz, T, wts["s_conv_w"], wts["s_conv_b"], wts["small_bias"],
                     wts["alog_row"], wts["d_row"], wts["s_norm_g"], s_state)
    x = _merge(x, hm, gated, proj, mod3, rows_per_group, 5, wts["w_pa"], wts["w_pb"],
               wts["w_out"])
    y = _ffn(x, mod3, rows_per_group, 6, wts["norm_ff2"], wts["ff2_w1"], wts["ff2_w3"],
             wts["ff2_w2"], final_norm=final_norm)
    p3 = proj.reshape(bsz, T, P_WIDTH)
    mconv1 = p3[:, T - (CONV_W - 1):, P_QK:P_QK + 2 * M_QK_W]
    sconv1 = p3[:, T - (CONV_W - 1):, P_XBC:P_XBC + S_CONV_CH]
    return y.reshape(bsz, T, D_MODEL), (c1, n1, m1, mconv1, h1, sconv1)


def _pack_w_in(w):
    o_if = 2 * M_QK_W + 2 * M_V_W
    o_z = o_if + 2 * M_HEADS
    o_dt = o_z + S_INNER + S_CONV_CH
    o_ga = o_dt + S_HEADS
    pad = P_WIDTH - (P_SMALL + 2 * M_HEADS + S_HEADS)
    return jnp.concatenate(
        [w[:, :o_if], w[:, o_z:o_dt], w[:, o_ga:], w[:, o_if:o_z], w[:, o_dt:o_ga],
         jnp.zeros((w.shape[0], pad), w.dtype)], axis=1).astype(BF16)


def kernel(x_prompt, x_sample, c_prompt, c_sample, state_mlstm_C, state_mlstm_n, state_mlstm_m, state_mlstm_conv, state_ssm, state_ssm_conv, ada_w, ada_b, norm_ff1, ff1_w1, ff1_w3, ff1_w2, norm_mix, w_in, m_conv_w, m_conv_b, m_if_b, m_head_g, w_pa, s_conv_w, s_conv_b, s_dt_bias, s_A_log, s_D, s_norm_g, w_pb, w_out, norm_ff2, ff2_w1, ff2_w3, ff2_w2, final_norm):
    bp, tp, _ = x_prompt.shape
    bs, ts, _ = x_sample.shape
    depth = ada_w.shape[0]
    assert depth == 1

    hp, hs = x_prompt, x_sample
    new_p, new_s = [], []
    for l in range(depth):
        c_all = jnp.concatenate([c_prompt, c_sample], axis=0)
        pad_rows = (-c_all.shape[0]) % (2 * SUBLANES)
        c_all = jnp.pad(c_all, ((0, pad_rows), (0, 0)))
        mod = _ada(c_all, ada_w[l], ada_b[l])
        mod_p = mod[:bp].reshape(bp, 1, N_MOD * D_MODEL)
        mod_s = jnp.repeat(mod[bp:bp + bs], ts, axis=0).reshape(1, bs * ts, N_MOD * D_MODEL)

        lane_pad = LANES - (2 * M_HEADS + S_HEADS)
        small_bias = jnp.concatenate(
            [m_if_b[l], s_dt_bias[l], jnp.zeros((lane_pad,), F32)]).reshape(1, LANES)
        alog_row = jnp.concatenate(
            [jnp.zeros((2 * M_HEADS,), F32), s_A_log[l], jnp.zeros((lane_pad,), F32)]
        ).reshape(1, LANES)
        wts = {
            "norm_ff1": norm_ff1[l], "ff1_w1": ff1_w1[l].astype(BF16),
            "ff1_w3": ff1_w3[l].astype(BF16), "ff1_w2": ff1_w2[l].astype(BF16),
            "norm_mix": norm_mix[l], "w_packed": _pack_w_in(w_in[l]),
            "m_conv_w": m_conv_w[l], "m_conv_b": m_conv_b[l], "small_bias": small_bias,
            "m_head_g": m_head_g[l], "w_pa": w_pa[l].astype(BF16),
            "s_conv_w": s_conv_w[l], "s_conv_b": s_conv_b[l], "alog_row": alog_row,
            "d_row": jnp.repeat(s_D[l], S_HEADDIM).reshape(1, S_INNER),
            "s_norm_g": s_norm_g[l], "w_pb": w_pb[l].astype(BF16),
            "w_out": w_out[l].astype(BF16),
            "norm_ff2": norm_ff2[l], "ff2_w1": ff2_w1[l].astype(BF16),
            "ff2_w3": ff2_w3[l].astype(BF16), "ff2_w2": ff2_w2[l].astype(BF16),
        }
        st_s = (state_mlstm_C[l], state_mlstm_n[l], state_mlstm_m[l], state_mlstm_conv[l],
                state_ssm[l], state_ssm_conv[l])
        hp, sp = _layer(hp, mod_p, tp, bp, tp, wts, final_norm, None)
        hs, ss = _layer(hs, mod_s, bs * ts, bs, ts, wts, final_norm, st_s)
        new_p.append(sp)
        new_s.append(ss)
    outs_p = [jnp.stack([s[i] for s in new_p]) for i in range(6)]
    outs_s = [jnp.stack([s[i] for s in new_s]) for i in range(6)]
    return (hp, hs, *outs_p, *outs_s)
```

```python
import functools

import jax
import jax.numpy as jnp
from jax import lax
from jax.experimental import pallas as pl
from jax.experimental.pallas import tpu as pltpu

F32 = jnp.float32
BF16 = jnp.bfloat16

D_MODEL = 1024
M_HEADS = 4
M_QK = 256
M_V = 512
M_QK_W = M_HEADS * M_QK
M_V_W = M_HEADS * M_V
S_INNER = 2048
S_HEADDIM = 64
S_HEADS = 32
S_GROUPS = 4
S_HPG = S_HEADS // S_GROUPS
S_STATE = 128
S_GW = S_INNER // S_GROUPS
S_BCW = S_GROUPS * S_STATE
S_CONV_CH = S_INNER + 2 * S_BCW
CONV_W = 4
D_FF = 2816
CHUNK = 128
EPS = 1e-6
N_MOD = 9

LANES = 128
SUBLANES = 8

P_QK = 0
P_V = 2 * M_QK_W
P_O = P_V + M_V_W
P_Z = P_O + M_V_W
P_XBC = P_Z + S_INNER
P_B = P_XBC + S_INNER
P_C = P_B + S_BCW
P_GA = P_XBC + S_CONV_CH
P_GB = P_GA + D_MODEL
P_SMALL = P_GB + D_MODEL
P_WIDTH = 13824
SM_I = 0
SM_F = M_HEADS
SM_DT = 2 * M_HEADS

VMEM_LIMIT = 56 * 1024 * 1024


def _silu(x):
    return x * jax.nn.sigmoid(x)


def _rms(x, g):
    return x * lax.rsqrt(jnp.mean(x * x, axis=-1, keepdims=True) + EPS) * g


def _ada_kernel(c_ref, w_ref, b_ref, o_ref):
    s = _silu(c_ref[...]).astype(BF16)
    o_ref[...] = jnp.dot(s, w_ref[...].astype(BF16), preferred_element_type=F32) + b_ref[...]


def _ada(c_all, ada_w, ada_b):
    m = c_all.shape[0]
    n = ada_w.shape[1]
    tn = 1024
    return pl.pallas_call(
        _ada_kernel,
        grid=(n // tn,),
        in_specs=[
            pl.BlockSpec((m, D_MODEL), lambda j: (0, 0)),
            pl.BlockSpec((D_MODEL, tn), lambda j: (0, j)),
            pl.BlockSpec((1, tn), lambda j: (0, j)),
        ],
        out_specs=pl.BlockSpec((m, tn), lambda j: (0, j)),
        out_shape=jax.ShapeDtypeStruct((m, n), F32),
        compiler_params=pltpu.CompilerParams(
            dimension_semantics=("arbitrary",), vmem_limit_bytes=VMEM_LIMIT),
        name="ada_mod",
    )(c_all, ada_w, ada_b.reshape(1, n))


def _ffn_kernel(x_ref, sh_ref, sc_ref, g_ref, nw_ref, w1_ref, w3_ref, w2_ref, *rest,
                nf, final):
    if final:
        fn_ref, o_ref, u_ref = rest
    else:
        o_ref, u_ref = rest
    f = pl.program_id(1)

    @pl.when(f == 0)
    def _():
        xn = _rms(x_ref[...], nw_ref[...])
        u_ref[...] = (xn * (1.0 + sc_ref[0]) + sh_ref[0]).astype(BF16)
        o_ref[...] = jnp.zeros_like(o_ref)

    u = u_ref[...]
    h1 = jnp.dot(u, w1_ref[...], preferred_element_type=F32)
    h3 = jnp.dot(u, w3_ref[...], preferred_element_type=F32)
    a = (_silu(h1) * h3).astype(BF16)
    o_ref[...] += jnp.dot(a, w2_ref[...], preferred_element_type=F32)

    @pl.when(f == nf - 1)
    def _():
        y = x_ref[...] + (0.5 * g_ref[0]) * o_ref[...]
        if final:
            y = _rms(y, fn_ref[...])
        o_ref[...] = y


def _mod_spec(mod3, rows_per_group, tm, k):
    if mod3.shape[1] == 1:
        tiles_per_group = rows_per_group // tm
        return pl.BlockSpec((1, 1, D_MODEL), lambda i, *_: (i // tiles_per_group, 0, k))
    return pl.BlockSpec((1, tm, D_MODEL), lambda i, *_: (0, i, k))


def _ffn(x, mod3, rows_per_group, kmod, norm_w, w1, w3, w2, final_norm=None):
    n = x.shape[0]
    tm = min(1024, rows_per_group)
    tf = 256
    nf = D_FF // tf
    final = final_norm is not None
    in_specs = [
        pl.BlockSpec((tm, D_MODEL), lambda i, f: (i, 0)),
        _mod_spec(mod3, rows_per_group, tm, kmod),
        _mod_spec(mod3, rows_per_group, tm, kmod + 1),
        _mod_spec(mod3, rows_per_group, tm, kmod + 2),
        pl.BlockSpec((1, D_MODEL), lambda i, f: (0, 0)),
        pl.BlockSpec((D_MODEL, tf), lambda i, f: (0, f)),
        pl.BlockSpec((D_MODEL, tf), lambda i, f: (0, f)),
        pl.BlockSpec((tf, D_MODEL), lambda i, f: (f, 0)),
    ]
    args = [x, mod3, mod3, mod3, norm_w.reshape(1, D_MODEL), w1, w3, w2]
    if final:
        in_specs.append(pl.BlockSpec((1, D_MODEL), lambda i, f: (0, 0)))
        args.append(final_norm.reshape(1, D_MODEL))
    return pl.pallas_call(
        functools.partial(_ffn_kernel, nf=nf, final=final),
        grid=(n // tm, nf),
        in_specs=in_specs,
        out_specs=pl.BlockSpec((tm, D_MODEL), lambda i, f: (i, 0)),
        out_shape=jax.ShapeDtypeStruct((n, D_MODEL), F32),
        scratch_shapes=[pltpu.VMEM((tm, D_MODEL), BF16)],
        compiler_params=pltpu.CompilerParams(
            dimension_semantics=("parallel", "arbitrary"), vmem_limit_bytes=VMEM_LIMIT),
        name="ffn_final" if final else "ffn",
    )(*args)


def _proj_kernel(x_ref, sh_ref, sc_ref, nw_ref, w_ref, o_ref, u_ref):
    @pl.when(pl.program_id(1) == 0)
    def _():
        xn = _rms(x_ref[...], nw_ref[...])
        u_ref[...] = (xn * (1.0 + sc_ref[0]) + sh_ref[0]).astype(BF16)

    o_ref[...] = jnp.dot(u_ref[...], w_ref[...], preferred_element_type=F32)


def _proj(x, mod3, rows_per_group, kmod, norm_w, w_packed):
    n = x.shape[0]
    tm = min(1024, rows_per_group)
    tn = 1536
    return pl.pallas_call(
        _proj_kernel,
        grid=(n // tm, P_WIDTH // tn),
        in_specs=[
            pl.BlockSpec((tm, D_MODEL), lambda i, j: (i, 0)),
            _mod_spec(mod3, rows_per_group, tm, kmod),
            _mod_spec(mod3, rows_per_group, tm, kmod + 1),
            pl.BlockSpec((1, D_MODEL), lambda i, j: (0, 0)),
            pl.BlockSpec((D_MODEL, tn), lambda i, j: (0, j)),
        ],
        out_specs=pl.BlockSpec((tm, tn), lambda i, j: (i, j)),
        out_shape=jax.ShapeDtypeStruct((n, P_WIDTH), F32),
        scratch_shapes=[pltpu.VMEM((tm, D_MODEL), BF16)],
        compiler_params=pltpu.CompilerParams(
            dimension_semantics=("parallel", "arbitrary"), vmem_limit_bytes=VMEM_LIMIT),
        name="mix_proj",
    )(x, mod3, mod3, norm_w.reshape(1, D_MODEL), w_packed)


CONV_SLAB = 512


def _conv_silu(xp_ref, x_ref, w_ref, b_ref, L, dst_ref):
    first = SUBLANES - (CONV_W - 1)
    for s in range(0, x_ref.shape[1], CONV_SLAB):
        cols = slice(s, s + CONV_SLAB)
        xp_ref[pl.ds(SUBLANES, L), cols] = x_ref[:, cols]
        acc = b_ref[:, cols]
        for j in range(CONV_W):
            acc = acc + xp_ref[pl.ds(first + j, L), cols] * w_ref[j:j + 1, cols]
        dst_ref[:, cols] = _silu(acc)
        xp_ref[pl.ds(0, SUBLANES), cols] = xp_ref[pl.ds(L, SUBLANES), cols]


def _init_conv(xp_ref, init_ref):
    xp_ref[pl.ds(0, SUBLANES), :] = jnp.zeros((SUBLANES, xp_ref.shape[1]), F32)
    if init_ref is not None:
        xp_ref[pl.ds(SUBLANES - (CONV_W - 1), CONV_W - 1), :] = init_ref[0]


def _split3(a):
    hi = a.astype(BF16)
    r1 = a - hi.astype(F32)
    mid = r1.astype(BF16)
    lo = (r1 - mid.astype(F32)).astype(BF16)
    return hi, mid, lo


def _dot3(mask_b, a):
    return sum(jnp.dot(mask_b, p, preferred_element_type=F32) for p in _split3(a))


def _tn3(a, mask_b):
    return sum(_tn_dot(p, mask_b) for p in _split3(a))


def _log_sigmoid(x):
    return jnp.minimum(x, 0.0) - jnp.log1p(jnp.exp(-jnp.abs(x)))


def _softplus(x):
    return jnp.maximum(x, 0.0) + jnp.log1p(jnp.exp(-jnp.abs(x)))


def _nt_dot(a, b):
    return lax.dot_general(a, b, (((1,), (1,)), ((), ())), preferred_element_type=F32)


def _tn_dot(a, b):
    return lax.dot_general(a, b, (((0,), (0,)), ((), ())), preferred_element_type=F32)


def _mlstm_kernel(*refs, L, has_state):
    qk_ref, v_ref, og_ref, sm_ref, w_ref, b_ref, sb_ref, hg_ref = refs[:8]
    refs = refs[8:]
    if has_state:
        cv0_ref, c0_ref, n0_ref, m0_ref = refs[:4]
        refs = refs[4:]
    hm_ref, c_ref, n_ref, m_ref, xp_ref, qk_s = refs

    @pl.when(pl.program_id(1) == 0)
    def _():
        if has_state:
            c_ref[...] = c0_ref[...]
            n_ref[...] = n0_ref[...]
            m_ref[...] = m0_ref[...]
            _init_conv(xp_ref, cv0_ref)
        else:
            c_ref[...] = jnp.zeros_like(c_ref)
            n_ref[...] = jnp.zeros_like(n_ref)
            m_ref[...] = jnp.zeros_like(m_ref)
            _init_conv(xp_ref, None)

    _conv_silu(xp_ref, qk_ref, w_ref, b_ref, L, qk_s)
    sm = sm_ref[...] + sb_ref[...]
    lf_all = _log_sigmoid(sm)

    rr = lax.broadcasted_iota(jnp.int32, (L, L), 0)
    cc = lax.broadcasted_iota(jnp.int32, (L, L), 1)
    tril = cc <= rr
    b_all = _dot3(tril.astype(BF16), lf_all)
    b_all_t = _tn3(lf_all, (rr <= cc).astype(BF16))
    sm_t = _tn3(sm, (rr == cc).astype(BF16))

    for h in range(M_HEADS):
        q = qk_s[:, h * M_QK:(h + 1) * M_QK]
        k = qk_s[:, M_QK_W + h * M_QK:M_QK_W + (h + 1) * M_QK] * (M_QK ** -0.5)
        v = v_ref[:, h * M_V:(h + 1) * M_V]
        i_col = sm[:, SM_I + h:SM_I + h + 1]
        i_row = sm_t[SM_I + h:SM_I + h + 1, :]
        b_col = b_all[:, SM_F + h:SM_F + h + 1]
        b_row = b_all_t[SM_F + h:SM_F + h + 1, :]

        c_prev = c_ref[0, h]
        n_prev = n_ref[0, h]
        m_prev = m_ref[0, h]

        dmat = jnp.where(tril, b_col - b_row + i_row, -jnp.inf)
        inter = b_col + m_prev
        mt = jnp.maximum(inter, jnp.max(dmat, axis=1, keepdims=True))
        w_tok = jnp.exp(dmat - mt)
        w_st = jnp.exp(inter - mt)

        qb = q.astype(BF16)
        kb = k.astype(BF16)
        vb = v.astype(BF16)
        s = _nt_dot(qb, kb) * w_tok
        num = (jnp.dot(s.astype(BF16), vb, preferred_element_type=F32)
               + w_st * jnp.dot(qb, c_prev.astype(BF16), preferred_element_type=F32))
        den = (jnp.sum(s, axis=1, keepdims=True)
               + w_st * jnp.sum(q * n_prev, axis=1, keepdims=True))
        hh = num / jnp.maximum(jnp.abs(den), jnp.exp(-mt))

        m_new = mt[L - 1:L, :]
        b_last = b_col[L - 1:L, :]
        ws = jnp.exp(b_last - b_col + i_col - m_new)
        decay = jnp.exp(b_last + m_prev - m_new)
        kw = k * ws
        c_ref[0, h] = decay * c_prev + _tn_dot(kw.astype(BF16), vb)
        n_ref[0, h] = decay * n_prev + jnp.sum(kw, axis=0, keepdims=True)
        m_ref[0, h] = m_new

        mu = jnp.mean(hh, axis=-1, keepdims=True)
        hc = hh - mu
        var = jnp.mean(hc * hc, axis=-1, keepdims=True)
        hsl = slice(h * M_V, (h + 1) * M_V)
        hm_ref[:, hsl] = (hc * lax.rsqrt(var + EPS) * hg_ref[:, hsl]
                          * jax.nn.sigmoid(og_ref[:, hsl]))


def _mlstm(proj, bsz, T, m_conv_w, m_conv_b, small_bias, m_head_g, state=None):
    L = min(T, CHUNK)
    nc = T // L
    has_state = state is not None
    qkw = 2 * M_QK_W

    def row(b, c):
        return b * nc + c

    const = lambda b, c: (0, 0)
    per_b = lambda b, c: (b, 0, 0, 0)
    in_specs = [
        pl.BlockSpec((L, qkw), lambda b, c: (row(b, c), P_QK // qkw)),
        pl.BlockSpec((L, M_V_W), lambda b, c: (row(b, c), P_V // M_V_W)),
        pl.BlockSpec((L, M_V_W), lambda b, c: (row(b, c), P_O // M_V_W)),
        pl.BlockSpec((L, LANES), lambda b, c: (row(b, c), P_SMALL // LANES)),
        pl.BlockSpec((CONV_W, qkw), const),
        pl.BlockSpec((1, qkw), const),
        pl.BlockSpec((1, LANES), const),
        pl.BlockSpec((1, M_V_W), const),
    ]
    args = [proj, proj, proj, proj, m_conv_w, m_conv_b.reshape(1, -1), small_bias,
            m_head_g.reshape(1, -1)]
    if has_state:
        conv0, c0, n0, m0 = state
        in_specs += [
            pl.BlockSpec((1, CONV_W - 1, qkw), lambda b, c: (b, 0, 0)),
            pl.BlockSpec((1, M_HEADS, M_QK, M_V), per_b),
            pl.BlockSpec((1, M_HEADS, 1, M_QK), per_b),
            pl.BlockSpec((1, M_HEADS, 1, 1), per_b),
        ]
        args += [conv0, c0, n0.reshape(bsz, M_HEADS, 1, M_QK), m0.reshape(bsz, M_HEADS, 1, 1)]
    out_shape = (
        jax.ShapeDtypeStruct((bsz * T, M_V_W), F32),
        jax.ShapeDtypeStruct((bsz, M_HEADS, M_QK, M_V), F32),
        jax.ShapeDtypeStruct((bsz, M_HEADS, 1, M_QK), F32),
        jax.ShapeDtypeStruct((bsz, M_HEADS, 1, 1), F32),
    )
    out_specs = (
        pl.BlockSpec((L, M_V_W), lambda b, c: (row(b, c), 0)),
        pl.BlockSpec((1, M_HEADS, M_QK, M_V), per_b),
        pl.BlockSpec((1, M_HEADS, 1, M_QK), per_b),
        pl.BlockSpec((1, M_HEADS, 1, 1), per_b),
    )
    hm, c1, n1, m1 = pl.pallas_call(
        functools.partial(_mlstm_kernel, L=L, has_state=has_state),
        grid=(bsz, nc),
        in_specs=in_specs,
        out_specs=out_specs,
        out_shape=out_shape,
        scratch_shapes=[pltpu.VMEM((L + SUBLANES, qkw), F32), pltpu.VMEM((L, qkw), F32)],
        compiler_params=pltpu.CompilerParams(
            dimension_semantics=("parallel", "arbitrary"), vmem_limit_bytes=VMEM_LIMIT),
        name="mlstm_state" if has_state else "mlstm",
    )(*args)
    return hm, c1, n1.reshape(bsz, M_HEADS, M_QK), m1.reshape(bsz, M_HEADS)


def _ssd_kernel(*refs, L, has_state):
    (z_ref, xs_ref, bp_ref, cp_ref, sm_ref, w_ref, b_ref, sb_ref, al_ref, d_ref,
     ng_ref) = refs[:11]
    refs = refs[11:]
    if has_state:
        cv0_ref, h0_ref = refs[:2]
        refs = refs[2:]
    y_ref, h_ref, xx_ref, xb_ref, xc_ref, xs_s, bm_s, cm_s = refs
    x_cols = slice(0, S_INNER)
    b_cols = slice(S_INNER, S_INNER + S_BCW)
    c_cols = slice(S_INNER + S_BCW, S_CONV_CH)

    @pl.when(pl.program_id(1) == 0)
    def _():
        if has_state:
            h_ref[...] = h0_ref[...]
            for xp_ref, cols in ((xx_ref, x_cols), (xb_ref, b_cols), (xc_ref, c_cols)):
                _init_conv(xp_ref, cv0_ref.at[:, :, cols])
        else:
            h_ref[...] = jnp.zeros_like(h_ref)
            for xp_ref in (xx_ref, xb_ref, xc_ref):
                _init_conv(xp_ref, None)

    _conv_silu(xx_ref, xs_ref, w_ref.at[:, x_cols], b_ref.at[:, x_cols], L, xs_s)
    _conv_silu(xb_ref, bp_ref, w_ref.at[:, b_cols], b_ref.at[:, b_cols], L, bm_s)
    _conv_silu(xc_ref, cp_ref, w_ref.at[:, c_cols], b_ref.at[:, c_cols], L, cm_s)

    dt_all = _softplus(sm_ref[...] + sb_ref[...])
    da_all = dt_all * (-jnp.exp(al_ref[...]))
    lane = lax.broadcasted_iota(jnp.int32, (L, LANES), 1)
    rr = lax.broadcasted_iota(jnp.int32, (L, L), 0)
    cc = lax.broadcasted_iota(jnp.int32, (L, L), 1)
    tril = cc <= rr
    lo = lane < S_HEADDIM
    lo_row = lax.broadcasted_iota(jnp.int32, (LANES, 1), 0) < S_HEADDIM
    cum_all = _dot3(tril.astype(BF16), da_all)
    cum_all_t = _tn3(da_all, (rr <= cc).astype(BF16))

    for g in range(S_GROUPS):
        bmb = bm_s[:, g * S_STATE:(g + 1) * S_STATE].astype(BF16)
        cmb = cm_s[:, g * S_STATE:(g + 1) * S_STATE].astype(BF16)
        cb = _nt_dot(cmb, bmb)
        ssq = jnp.zeros((L, 1), F32)
        for p in range(S_HPG // 2):
            psl = slice(g * S_GW + p * LANES, g * S_GW + (p + 1) * LANES)
            hsl = slice(p * LANES, (p + 1) * LANES)
            lmat, dt_col, dec_col, ecum_col, elast = [], [], [], [], []
            for e in range(2):
                idx = SM_DT + g * S_HPG + 2 * p + e
                cum_col = cum_all[:, idx:idx + 1]
                cum_last = cum_col[L - 1:L, :]
                lmat.append(jnp.exp(
                    jnp.where(tril, cum_col - cum_all_t[idx:idx + 1, :], -jnp.inf)))
                dt_col.append(dt_all[:, idx:idx + 1])
                dec_col.append(dt_col[e] * jnp.exp(cum_last - cum_col))
                ecum_col.append(jnp.exp(cum_col))
                elast.append(jnp.exp(cum_last))
            xs = xs_s[:, psl]
            xdt = xs * jnp.where(lo, dt_col[0], dt_col[1])
            xdd = xs * jnp.where(lo, dec_col[0], dec_col[1])
            h_prev = h_ref[0, g, hsl, :]
            y = (jnp.dot((cb * lmat[0]).astype(BF16), jnp.where(lo, xdt, 0.0).astype(BF16),
                         preferred_element_type=F32)
                 + jnp.dot((cb * lmat[1]).astype(BF16), jnp.where(lo, 0.0, xdt).astype(BF16),
                           preferred_element_type=F32)
                 + jnp.where(lo, ecum_col[0], ecum_col[1]) * _nt_dot(cmb, h_prev.astype(BF16)))
            h_ref[0, g, hsl, :] = (jnp.where(lo_row, elast[0], elast[1]) * h_prev
                                   + _tn_dot(xdd.astype(BF16), bmb))
            gated = (y + d_ref[:, psl] * xs) * _silu(z_ref[:, psl])
            ssq = ssq + jnp.sum(gated * gated, axis=1, keepdims=True)
            y_ref[:, psl] = gated
        gsl = slice(g * S_GW, (g + 1) * S_GW)
        y_ref[:, gsl] = y_ref[:, gsl] * lax.rsqrt(ssq * (1.0 / S_GW) + EPS) * ng_ref[:, gsl]


def _ssd(proj, bsz, T, s_conv_w, s_conv_b, small_bias, alog_row, d_row, s_norm_g, state=None):
    L = min(T, CHUNK)
    nc = T // L
    has_state = state is not None

    def row(b, c):
        return b * nc + c

    const = lambda b, c: (0, 0)
    in_specs = [
        pl.BlockSpec((L, S_INNER), lambda b, c: (row(b, c), P_Z // S_INNER)),
        pl.BlockSpec((L, S_INNER), lambda b, c: (row(b, c), P_XBC // S_INNER)),
        pl.BlockSpec((L, S_BCW), lambda b, c: (row(b, c), P_B // S_BCW)),
        pl.BlockSpec((L, S_BCW), lambda b, c: (row(b, c), P_C // S_BCW)),
        pl.BlockSpec((L, LANES), lambda b, c: (row(b, c), P_SMALL // LANES)),
        pl.BlockSpec((CONV_W, S_CONV_CH), const),
        pl.BlockSpec((1, S_CONV_CH), const),
        pl.BlockSpec((1, LANES), const),
        pl.BlockSpec((1, LANES), const),
        pl.BlockSpec((1, S_INNER), const),
        pl.BlockSpec((1, S_INNER), const),
    ]
    args = [proj, proj, proj, proj, proj, s_conv_w, s_conv_b.reshape(1, -1),
            small_bias, alog_row, d_row, s_norm_g.reshape(1, -1)]
    if has_state:
        conv0, h0 = state
        in_specs += [
            pl.BlockSpec((1, CONV_W - 1, S_CONV_CH), lambda b, c: (b, 0, 0)),
            pl.BlockSpec((1, S_GROUPS, S_GW, S_STATE), lambda b, c: (b, 0, 0, 0)),
        ]
        args += [conv0, h0.reshape(bsz, S_GROUPS, S_GW, S_STATE)]
    y, h1 = pl.pallas_call(
        functools.partial(_ssd_kernel, L=L, has_state=has_state),
        grid=(bsz, nc),
        in_specs=in_specs,
        out_specs=(
            pl.BlockSpec((L, S_INNER), lambda b, c: (row(b, c), 0)),
            pl.BlockSpec((1, S_GROUPS, S_GW, S_STATE), lambda b, c: (b, 0, 0, 0)),
        ),
        out_shape=(
            jax.ShapeDtypeStruct((bsz * T, S_INNER), F32),
            jax.ShapeDtypeStruct((bsz, S_GROUPS, S_GW, S_STATE), F32),
        ),
        scratch_shapes=[pltpu.VMEM((L + SUBLANES, S_INNER), F32),
                        pltpu.VMEM((L + SUBLANES, S_BCW), F32),
                        pltpu.VMEM((L + SUBLANES, S_BCW), F32),
                        pltpu.VMEM((L, S_INNER), F32),
                        pltpu.VMEM((L, S_BCW), F32),
                        pltpu.VMEM((L, S_BCW), F32)],
        compiler_params=pltpu.CompilerParams(
            dimension_semantics=("parallel", "arbitrary"), vmem_limit_bytes=VMEM_LIMIT),
        name="ssd_state" if has_state else "ssd",
    )(*args)
    return y, h1.reshape(bsz, S_HEADS, S_HEADDIM, S_STATE)


def _merge_kernel(x_ref, hm_ref, gt_ref, ga_ref, gb_ref, g_ref, wpa_ref, wpb_ref, wo_ref, o_ref):
    ya = jnp.dot(hm_ref[...].astype(BF16), wpa_ref[...], preferred_element_type=F32)
    yb = jnp.dot(gt_ref[...].astype(BF16), wpb_ref[...], preferred_element_type=F32)
    merged = jax.nn.sigmoid(ga_ref[...]) * ya + jax.nn.sigmoid(gb_ref[...]) * yb
    o_ref[...] = x_ref[...] + g_ref[0] * jnp.dot(
        merged.astype(BF16), wo_ref[...], preferred_element_type=F32)


def _merge(x, hm, gated, proj, mod3, rows_per_group, kmod, w_pa, w_pb, w_out):
    n = x.shape[0]
    tm = 256
    const = lambda i: (0, 0)
    return pl.pallas_call(
        _merge_kernel,
        grid=(n // tm,),
        in_specs=[
            pl.BlockSpec((tm, D_MODEL), lambda i: (i, 0)),
            pl.BlockSpec((tm, M_V_W), lambda i: (i, 0)),
            pl.BlockSpec((tm, S_INNER), lambda i: (i, 0)),
            pl.BlockSpec((tm, D_MODEL), lambda i: (i, P_GA // D_MODEL)),
            pl.BlockSpec((tm, D_MODEL), lambda i: (i, P_GB // D_MODEL)),
            _mod_spec(mod3, rows_per_group, tm, kmod),
            pl.BlockSpec((M_V_W, D_MODEL), const),
            pl.BlockSpec((S_INNER, D_MODEL), const),
            pl.BlockSpec((D_MODEL, D_MODEL), const),
        ],
        out_specs=pl.BlockSpec((tm, D_MODEL), lambda i: (i, 0)),
        out_shape=jax.ShapeDtypeStruct((n, D_MODEL), F32),
        compiler_params=pltpu.CompilerParams(
            dimension_semantics=("parallel",), vmem_limit_bytes=VMEM_LIMIT),
        name="merge",
    )(x, hm, gated, proj, proj, mod3, w_pa, w_pb, w_out)


def _layer(x, mod3, rows_per_group, bsz, T, wts, final_norm, state):
    n = bsz * T
    x = x.reshape(n, D_MODEL)
    x = _ffn(x, mod3, rows_per_group, 0, wts["norm_ff1"], wts["ff1_w1"], wts["ff1_w3"],
             wts["ff1_w2"])
    proj = _proj(x, mod3, rows_per_group, 3, wts["norm_mix"], wts["w_packed"])
    if state is None:
        m_state = s_state = None
    else:
        c0, n0, m0, mconv0, ssm0, sconv0 = state
        m_state = (mconv0, c0, n0, m0)
        s_state = (sconv0, ssm0)
    hm, c1, n1, m1 = _mlstm(proj, bsz, T, wts["m_conv_w"], wts["m_conv_b"], wts["small_bias"],
                            wts["m_head_g"], m_state)
    gated, h1 = _ssd(proj, bsz, T, wts["s_conv_w"], wts["s_conv_b"], wts["small_bias"],
                     wts["alog_row"], wts["d_row"], wts["s_norm_g"], s_state)
    x = _merge(x, hm, gated, proj, mod3, rows_per_group, 5, wts["w_pa"], wts["w_pb"],
               wts["w_out"])
    y = _ffn(x, mod3, rows_per_group, 6, wts["norm_ff2"], wts["ff2_w1"], wts["ff2_w3"],
             wts["ff2_w2"], final_norm=final_norm)
    p3 = proj.reshape(bsz, T, P_WIDTH)
    mconv1 = p3[:, T - (CONV_W - 1):, P_QK:P_QK + 2 * M_QK_W]
    sconv1 = p3[:, T - (CONV_W - 1):, P_XBC:P_XBC + S_CONV_CH]
    return y.reshape(bsz, T, D_MODEL), (c1, n1, m1, mconv1, h1, sconv1)


def _pack_w_in(w):
    o_if = 2 * M_QK_W + 2 * M_V_W
    o_z = o_if + 2 * M_HEADS
    o_dt = o_z + S_INNER + S_CONV_CH
    o_ga = o_dt + S_HEADS
    pad = P_WIDTH - (P_SMALL + 2 * M_HEADS + S_HEADS)
    return jnp.concatenate(
        [w[:, :o_if], w[:, o_z:o_dt], w[:, o_ga:], w[:, o_if:o_z], w[:, o_dt:o_ga],
         jnp.zeros((w.shape[0], pad), w.dtype)], axis=1).astype(BF16)


def kernel(x_prompt, x_sample, c_prompt, c_sample, state_mlstm_C, state_mlstm_n, state_mlstm_m, state_mlstm_conv, state_ssm, state_ssm_conv, ada_w, ada_b, norm_ff1, ff1_w1, ff1_w3, ff1_w2, norm_mix, w_in, m_conv_w, m_conv_b, m_if_b, m_head_g, w_pa, s_conv_w, s_conv_b, s_dt_bias, s_A_log, s_D, s_norm_g, w_pb, w_out, norm_ff2, ff2_w1, ff2_w3, ff2_w2, final_norm):
    bp, tp, _ = x_prompt.shape
    bs, ts, _ = x_sample.shape
    depth = ada_w.shape[0]
    assert depth == 1

    hp, hs = x_prompt, x_sample
    new_p, new_s = [], []
    for l in range(depth):
        c_all = jnp.concatenate([c_prompt, c_sample], axis=0)
        pad_rows = (-c_all.shape[0]) % (2 * SUBLANES)
        c_all = jnp.pad(c_all, ((0, pad_rows), (0, 0)))
        mod = _ada(c_all, ada_w[l], ada_b[l])
        mod_p = mod[:bp].reshape(bp, 1, N_MOD * D_MODEL)
        mod_s = jnp.repeat(mod[bp:bp + bs], ts, axis=0).reshape(1, bs * ts, N_MOD * D_MODEL)

        lane_pad = LANES - (2 * M_HEADS + S_HEADS)
        small_bias = jnp.concatenate(
            [m_if_b[l], s_dt_bias[l], jnp.zeros((lane_pad,), F32)]).reshape(1, LANES)
        alog_row = jnp.concatenate(
            [jnp.zeros((2 * M_HEADS,), F32), s_A_log[l], jnp.zeros((lane_pad,), F32)]
        ).reshape(1, LANES)
        wts = {
            "norm_ff1": norm_ff1[l], "ff1_w1": ff1_w1[l].astype(BF16),
            "ff1_w3": ff1_w3[l].astype(BF16), "ff1_w2": ff1_w2[l].astype(BF16),
            "norm_mix": norm_mix[l], "w_packed": _pack_w_in(w_in[l]),
            "m_conv_w": m_conv_w[l], "m_conv_b": m_conv_b[l], "small_bias": small_bias,
            "m_head_g": m_head_g[l], "w_pa": w_pa[l].astype(BF16),
            "s_conv_w": s_conv_w[l], "s_conv_b": s_conv_b[l], "alog_row": alog_row,
            "d_row": jnp.repeat(s_D[l], S_HEADDIM).reshape(1, S_INNER),
            "s_norm_g": s_norm_g[l], "w_pb": w_pb[l].astype(BF16),
            "w_out": w_out[l].astype(BF16),
            "norm_ff2": norm_ff2[l], "ff2_w1": ff2_w1[l].astype(BF16),
            "ff2_w3": ff2_w3[l].astype(BF16), "ff2_w2": ff2_w2[l].astype(BF16),
        }
        st_s = (state_mlstm_C[l], state_mlstm_n[l], state_mlstm_m[l], state_mlstm_conv[l],
                state_ssm[l], state_ssm_conv[l])
        hp, sp = _layer(hp, mod_p, tp, bp, tp, wts, final_norm, None)
        hs, ss = _layer(hs, mod_s, bs * ts, bs, ts, wts, final_norm, st_s)
        new_p.append(sp)
        new_s.append(ss)
    outs_p = [jnp.stack([s[i] for s in new_p]) for i in range(6)]
    outs_s = [jnp.stack([s[i] for s in new_s]) for i in range(6)]
    return (hp, hs, *outs_p, *outs_s)
```

```python
import functools

import jax
import jax.numpy as jnp
from jax import lax
from jax.experimental import pallas as pl
from jax.experimental.pallas import tpu as pltpu

F32 = jnp.float32
BF16 = jnp.bfloat16

D_MODEL = 1024
M_HEADS = 4
M_QK = 256
M_V = 512
M_QK_W = M_HEADS * M_QK
M_V_W = M_HEADS * M_V
S_INNER = 2048
S_HEADDIM = 64
S_HEADS = 32
S_GROUPS = 4
S_HPG = S_HEADS // S_GROUPS
S_STATE = 128
S_GW = S_INNER // S_GROUPS
S_BCW = S_GROUPS * S_STATE
S_CONV_CH = S_INNER + 2 * S_BCW
CONV_W = 4
D_FF = 2816
CHUNK = 128
EPS = 1e-6
N_MOD = 9

LANES = 128
SUBLANES = 8

P_QK = 0
P_V = 2 * M_QK_W
P_O = P_V + M_V_W
P_Z = P_O + M_V_W
P_XBC = P_Z + S_INNER
P_B = P_XBC + S_INNER
P_C = P_B + S_BCW
P_GA = P_XBC + S_CONV_CH
P_GB = P_GA + D_MODEL
P_SMALL = P_GB + D_MODEL
P_WIDTH = 13824
SM_I = 0
SM_F = M_HEADS
SM_DT = 2 * M_HEADS
K_SCALE = M_QK ** -0.5

VMEM_LIMIT = 56 * 1024 * 1024


def _sigmoid(x):
    return 0.5 * jnp.tanh(0.5 * x) + 0.5


def _silu(x):
    h = 0.5 * x
    return h * jnp.tanh(h) + h


def _rms(x, g):
    return x * lax.rsqrt(jnp.mean(x * x, axis=-1, keepdims=True) + EPS) * g


def _ada_kernel(c_ref, w_ref, b_ref, o_ref):
    s = _silu(c_ref[...]).astype(BF16)
    o_ref[...] = jnp.dot(s, w_ref[...].astype(BF16), preferred_element_type=F32) + b_ref[...]


def _ada(c_all, ada_w, ada_b):
    m = c_all.shape[0]
    n = ada_w.shape[1]
    tn = 1024
    return pl.pallas_call(
        _ada_kernel,
        grid=(n // tn,),
        in_specs=[
            pl.BlockSpec((m, D_MODEL), lambda j: (0, 0)),
            pl.BlockSpec((D_MODEL, tn), lambda j: (0, j)),
            pl.BlockSpec((1, tn), lambda j: (0, j)),
        ],
        out_specs=pl.BlockSpec((m, tn), lambda j: (0, j)),
        out_shape=jax.ShapeDtypeStruct((m, n), F32),
        compiler_params=pltpu.CompilerParams(
            dimension_semantics=("arbitrary",), vmem_limit_bytes=VMEM_LIMIT),
        name="ada_mod",
    )(c_all, ada_w, ada_b.reshape(1, n))


def _ffn_kernel(x_ref, sh_ref, sc_ref, g_ref, nw_ref, w1_ref, w3_ref, w2_ref, *rest,
                nf, final):
    if final:
        fn_ref, o_ref, u_ref = rest
    else:
        o_ref, u_ref = rest
    f = pl.program_id(1)

    @pl.when(f == 0)
    def _():
        xn = _rms(x_ref[...], nw_ref[...])
        u_ref[...] = (xn * (1.0 + sc_ref[0]) + sh_ref[0]).astype(BF16)
        o_ref[...] = jnp.zeros_like(o_ref)

    u = u_ref[...]
    h1 = jnp.dot(u, w1_ref[...].astype(BF16), preferred_element_type=F32)
    h3 = jnp.dot(u, w3_ref[...].astype(BF16), preferred_element_type=F32)
    a = (_silu(h1) * h3).astype(BF16)
    o_ref[...] += jnp.dot(a, w2_ref[...].astype(BF16), preferred_element_type=F32)

    @pl.when(f == nf - 1)
    def _():
        y = x_ref[...] + (0.5 * g_ref[0]) * o_ref[...]
        if final:
            y = _rms(y, fn_ref[...])
        o_ref[...] = y


def _mod_spec(mod3, rows_per_group, tm, k):
    if mod3.shape[1] == 1:
        tiles_per_group = rows_per_group // tm
        return pl.BlockSpec((1, 1, D_MODEL), lambda i, *_: (i // tiles_per_group, 0, k))
    return pl.BlockSpec((1, tm, D_MODEL), lambda i, *_: (0, i, k))


def _ffn(x, mod3, rows_per_group, kmod, norm_w, w1, w3, w2, final_norm=None):
    n = x.shape[0]
    tm = min(1024, rows_per_group)
    tf = 256
    nf = D_FF // tf
    final = final_norm is not None
    in_specs = [
        pl.BlockSpec((tm, D_MODEL), lambda i, f: (i, 0)),
        _mod_spec(mod3, rows_per_group, tm, kmod),
        _mod_spec(mod3, rows_per_group, tm, kmod + 1),
        _mod_spec(mod3, rows_per_group, tm, kmod + 2),
        pl.BlockSpec((1, D_MODEL), lambda i, f: (0, 0)),
        pl.BlockSpec((D_MODEL, tf), lambda i, f: (0, f)),
        pl.BlockSpec((D_MODEL, tf), lambda i, f: (0, f)),
        pl.BlockSpec((tf, D_MODEL), lambda i, f: (f, 0)),
    ]
    args = [x, mod3, mod3, mod3, norm_w.reshape(1, D_MODEL), w1, w3, w2]
    if final:
        in_specs.append(pl.BlockSpec((1, D_MODEL), lambda i, f: (0, 0)))
        args.append(final_norm.reshape(1, D_MODEL))
    return pl.pallas_call(
        functools.partial(_ffn_kernel, nf=nf, final=final),
        grid=(n // tm, nf),
        in_specs=in_specs,
        out_specs=pl.BlockSpec((tm, D_MODEL), lambda i, f: (i, 0)),
        out_shape=jax.ShapeDtypeStruct((n, D_MODEL), F32),
        scratch_shapes=[pltpu.VMEM((tm, D_MODEL), BF16)],
        compiler_params=pltpu.CompilerParams(
            dimension_semantics=("parallel", "arbitrary"), vmem_limit_bytes=VMEM_LIMIT),
        name="ffn_final" if final else "ffn",
    )(*args)


def _proj_kernel(x_ref, sh_ref, sc_ref, nw_ref, w_ref, o_ref, u_ref):
    @pl.when(pl.program_id(1) == 0)
    def _():
        xn = _rms(x_ref[...], nw_ref[...])
        u_ref[...] = (xn * (1.0 + sc_ref[0]) + sh_ref[0]).astype(BF16)

    o_ref[...] = jnp.dot(u_ref[...], w_ref[...], preferred_element_type=F32)


def _proj(x, mod3, rows_per_group, kmod, norm_w, w_packed):
    n = x.shape[0]
    tm = min(1024, rows_per_group)
    tn = 1536
    return pl.pallas_call(
        _proj_kernel,
        grid=(n // tm, P_WIDTH // tn),
        in_specs=[
            pl.BlockSpec((tm, D_MODEL), lambda i, j: (i, 0)),
            _mod_spec(mod3, rows_per_group, tm, kmod),
            _mod_spec(mod3, rows_per_group, tm, kmod + 1),
            pl.BlockSpec((1, D_MODEL), lambda i, j: (0, 0)),
            pl.BlockSpec((D_MODEL, tn), lambda i, j: (0, j)),
        ],
        out_specs=pl.BlockSpec((tm, tn), lambda i, j: (i, j)),
        out_shape=jax.ShapeDtypeStruct((n, P_WIDTH), F32),
        scratch_shapes=[pltpu.VMEM((tm, D_MODEL), BF16)],
        compiler_params=pltpu.CompilerParams(
            dimension_semantics=("parallel", "arbitrary"), vmem_limit_bytes=VMEM_LIMIT),
        name="mix_proj",
    )(x, mod3, mod3, norm_w.reshape(1, D_MODEL), w_packed)


PROJ_TN = 512
CONV_ROWS = 64


def _proj_act_kernel(x_ref, sh_ref, sc_ref, nw_ref, w_ref, cw_ref, cb_ref, cs_ref,
                     o_ref, tail_ref, u_ref, raw_ref, *, tm):
    j = pl.program_id(1)

    @pl.when(j == 0)
    def _():
        xn = _rms(x_ref[...], nw_ref[...])
        u_ref[...] = (xn * (1.0 + sc_ref[0]) + sh_ref[0]).astype(BF16)
        raw_ref[pl.ds(0, SUBLANES), :] = jnp.zeros((SUBLANES, PROJ_TN), F32)

    def tile(lo, hi):
        return (j >= lo // PROJ_TN) & (j < hi // PROJ_TN)

    is_conv = tile(P_QK, P_V) | tile(P_XBC, P_GA)
    is_sig = tile(P_O, P_Z) | tile(P_GA, P_SMALL)
    is_silu = tile(P_Z, P_XBC)

    def matmul():
        return jnp.dot(u_ref[...], w_ref[...], preferred_element_type=F32)

    @pl.when(is_conv)
    def _():
        raw_ref[pl.ds(SUBLANES, tm), :] = matmul()
        first = SUBLANES - (CONV_W - 1)
        for r0 in range(0, tm, CONV_ROWS):
            acc = cb_ref[...]
            for t in range(CONV_W):
                acc = acc + raw_ref[pl.ds(first + t + r0, CONV_ROWS), :] * cw_ref[t:t + 1, :]
            o_ref[pl.ds(r0, CONV_ROWS), :] = _silu(acc) * cs_ref[...]
        tail_ref[0] = raw_ref[pl.ds(tm, SUBLANES), :]

    @pl.when(is_sig)
    def _():
        o_ref[...] = _sigmoid(matmul())
        tail_ref[0] = o_ref[pl.ds(tm - SUBLANES, SUBLANES), :]

    @pl.when(is_silu)
    def _():
        o_ref[...] = _silu(matmul())
        tail_ref[0] = o_ref[pl.ds(tm - SUBLANES, SUBLANES), :]

    @pl.when(jnp.logical_not(is_conv | is_sig | is_silu))
    def _():
        o_ref[...] = matmul()
        tail_ref[0] = o_ref[pl.ds(tm - SUBLANES, SUBLANES), :]


def _proj_act(x, mod3, rows_per_group, kmod, norm_w, w_packed, conv_w, conv_b, conv_s):
    n = x.shape[0]
    tm = rows_per_group
    tn = PROJ_TN
    col = lambda i, j: (0, j)
    return pl.pallas_call(
        functools.partial(_proj_act_kernel, tm=tm),
        grid=(n // tm, P_WIDTH // tn),
        in_specs=[
            pl.BlockSpec((tm, D_MODEL), lambda i, j: (i, 0)),
            _mod_spec(mod3, rows_per_group, tm, kmod),
            _mod_spec(mod3, rows_per_group, tm, kmod + 1),
            pl.BlockSpec((1, D_MODEL), lambda i, j: (0, 0)),
            pl.BlockSpec((D_MODEL, tn), col),
            pl.BlockSpec((CONV_W, tn), col),
            pl.BlockSpec((1, tn), col),
            pl.BlockSpec((1, tn), col),
        ],
        out_specs=(
            pl.BlockSpec((tm, tn), lambda i, j: (i, j)),
            pl.BlockSpec((1, SUBLANES, tn), lambda i, j: (i, 0, j)),
        ),
        out_shape=(
            jax.ShapeDtypeStruct((n, P_WIDTH), F32),
            jax.ShapeDtypeStruct((n // tm, SUBLANES, P_WIDTH), F32),
        ),
        scratch_shapes=[pltpu.VMEM((tm, D_MODEL), BF16),
                        pltpu.VMEM((tm + SUBLANES, tn), F32)],
        compiler_params=pltpu.CompilerParams(
            dimension_semantics=("parallel", "arbitrary"), vmem_limit_bytes=VMEM_LIMIT),
        name="mix_proj_act",
    )(x, mod3, mod3, norm_w.reshape(1, D_MODEL), w_packed, conv_w, conv_b, conv_s)


CONV_SLAB = 256


def _conv_silu(xp_ref, x_ref, w_ref, b_ref, L, dst_ref, slabs=None):
    if slabs is None:
        slabs = range(x_ref.shape[1] // CONV_SLAB)
    for s in slabs:
        cols = slice(s * CONV_SLAB, (s + 1) * CONV_SLAB)
        xp_ref[pl.ds(SUBLANES, L), cols] = x_ref[:, cols]
        xp = xp_ref[:, cols]
        acc = b_ref[:, cols] + xp[SUBLANES:] * w_ref[CONV_W - 1:CONV_W, cols]
        for back in range(1, CONV_W):
            shifted = pltpu.roll(xp, back, axis=0)[SUBLANES:]
            acc = acc + shifted * w_ref[CONV_W - 1 - back:CONV_W - back, cols]
        dst_ref[:, cols] = _silu(acc)
        xp_ref[pl.ds(0, SUBLANES), cols] = xp_ref[pl.ds(L, SUBLANES), cols]


def _init_conv(xp_ref, init_ref):
    xp_ref[pl.ds(0, SUBLANES), :] = jnp.zeros((SUBLANES, xp_ref.shape[1]), F32)
    if init_ref is not None:
        xp_ref[pl.ds(SUBLANES - (CONV_W - 1), CONV_W - 1), :] = init_ref[0]


def _run_stages(streams):
    live = list(streams)
    while live:
        for stream in list(live):
            try:
                next(stream)
            except StopIteration:
                live.remove(stream)


def _split3(a):
    hi = a.astype(BF16)
    r1 = a - hi.astype(F32)
    mid = r1.astype(BF16)
    lo = (r1 - mid.astype(F32)).astype(BF16)
    return hi, mid, lo


def _dot3(mask_b, a):
    return sum(jnp.dot(mask_b, p, preferred_element_type=F32) for p in _split3(a))


def _tn3(a, mask_b):
    return sum(_tn_dot(p, mask_b) for p in _split3(a))


def _log_sigmoid(x):
    return jnp.minimum(x, 0.0) - jnp.log1p(jnp.exp(-jnp.abs(x)))


def _softplus(x):
    return jnp.maximum(x, 0.0) + jnp.log1p(jnp.exp(-jnp.abs(x)))


def _nt_dot(a, b):
    return lax.dot_general(a, b, (((1,), (1,)), ((), ())), preferred_element_type=F32)


def _tn_dot(a, b):
    return lax.dot_general(a, b, (((0,), (0,)), ((), ())), preferred_element_type=F32)


def _mlstm_kernel(*refs, L, has_state, pre):
    qk_ref, v_ref, og_ref, sm_ref, sb_ref, hg_ref = refs[:6]
    refs = refs[6:]
    if not pre:
        w_ref, b_ref = refs[:2]
        refs = refs[2:]
    if has_state:
        cv0_ref, c0_ref, n0_ref, m0_ref = refs[:4]
        refs = refs[4:]
    if pre:
        hm_ref, c_ref, n_ref, m_ref = refs
        qk_s = qk_ref
    else:
        hm_ref, c_ref, n_ref, m_ref, xp_ref, qk_s = refs

    @pl.when(pl.program_id(1) == 0)
    def _():
        if has_state:
            c_ref[...] = c0_ref[...]
            n_ref[...] = n0_ref[...]
            m_ref[...] = m0_ref[...]
        else:
            c_ref[...] = jnp.zeros_like(c_ref)
            n_ref[...] = jnp.zeros_like(n_ref)
            m_ref[...] = jnp.zeros_like(m_ref)
        if not pre:
            _init_conv(xp_ref, cv0_ref if has_state else None)

    sm = sm_ref[...] + sb_ref[...]
    lf_all = _log_sigmoid(sm)

    rr = lax.broadcasted_iota(jnp.int32, (L, L), 0)
    cc = lax.broadcasted_iota(jnp.int32, (L, L), 1)
    tril = cc <= rr
    b_all = _dot3(tril.astype(BF16), lf_all)
    b_all_t = _tn3(lf_all, (rr <= cc).astype(BF16))
    sm_t = _tn3(sm, (rr == cc).astype(BF16))

    def head_stages(h):
        if not pre:
            _conv_silu(xp_ref, qk_ref, w_ref, b_ref, L, qk_s,
                       slabs=[h * M_QK // CONV_SLAB, (M_QK_W + h * M_QK) // CONV_SLAB])
        yield
        q = qk_s[:, h * M_QK:(h + 1) * M_QK]
        k = qk_s[:, M_QK_W + h * M_QK:M_QK_W + (h + 1) * M_QK]
        if not pre:
            k = k * K_SCALE
        v = v_ref[:, h * M_V:(h + 1) * M_V]
        i_col = sm[:, SM_I + h:SM_I + h + 1]
        i_row = sm_t[SM_I + h:SM_I + h + 1, :]
        b_col = b_all[:, SM_F + h:SM_F + h + 1]
        b_row = b_all_t[SM_F + h:SM_F + h + 1, :]

        c_prev = c_ref[0, h]
        n_prev = n_ref[0, h]
        m_prev = m_ref[0, h]

        dmat = jnp.where(tril, b_col - b_row + i_row, -jnp.inf)
        inter = b_col + m_prev
        mt = jnp.maximum(inter, jnp.max(dmat, axis=1, keepdims=True))
        w_tok = jnp.exp(dmat - mt)
        w_st = jnp.exp(inter - mt)
        yield

        qb = q.astype(BF16)
        kb = k.astype(BF16)
        vb = v.astype(BF16)
        s = _nt_dot(qb, kb) * w_tok
        yield
        num = (jnp.dot(s.astype(BF16), vb, preferred_element_type=F32)
               + w_st * jnp.dot(qb, c_prev.astype(BF16), preferred_element_type=F32))
        den = (jnp.sum(s, axis=1, keepdims=True)
               + w_st * jnp.sum(q * n_prev, axis=1, keepdims=True))
        yield
        hh = num / jnp.maximum(jnp.abs(den), jnp.exp(-mt))

        m_new = mt[L - 1:L, :]
        b_last = b_col[L - 1:L, :]
        ws = jnp.exp(b_last - b_col + i_col - m_new)
        decay = jnp.exp(b_last + m_prev - m_new)
        kw = k * ws
        c_ref[0, h] = decay * c_prev + _tn_dot(kw.astype(BF16), vb)
        n_ref[0, h] = decay * n_prev + jnp.sum(kw, axis=0, keepdims=True)
        m_ref[0, h] = m_new
        yield

        mu = jnp.mean(hh, axis=-1, keepdims=True)
        hc = hh - mu
        yield
        var = jnp.mean(hc * hc, axis=-1, keepdims=True)
        hsl = slice(h * M_V, (h + 1) * M_V)
        og = og_ref[:, hsl]
        if not pre:
            og = _sigmoid(og)
        hm_ref[:, hsl] = hc * lax.rsqrt(var + EPS) * hg_ref[:, hsl] * og

    _run_stages([head_stages(h) for h in range(M_HEADS)])


def _mlstm(proj, bsz, T, m_conv_w, m_conv_b, small_bias, m_head_g, state=None, pre=False):
    L = min(T, CHUNK)
    nc = T // L
    has_state = state is not None
    assert not (pre and has_state)
    qkw = 2 * M_QK_W

    def row(b, c):
        return b * nc + c

    const = lambda b, c: (0, 0)
    per_b = lambda b, c: (b, 0, 0, 0)
    in_specs = [
        pl.BlockSpec((L, qkw), lambda b, c: (row(b, c), P_QK // qkw)),
        pl.BlockSpec((L, M_V_W), lambda b, c: (row(b, c), P_V // M_V_W)),
        pl.BlockSpec((L, M_V_W), lambda b, c: (row(b, c), P_O // M_V_W)),
        pl.BlockSpec((L, LANES), lambda b, c: (row(b, c), P_SMALL // LANES)),
        pl.BlockSpec((1, LANES), const),
        pl.BlockSpec((1, M_V_W), const),
    ]
    args = [proj, proj, proj, proj, small_bias, m_head_g.reshape(1, -1)]
    scratch = []
    if not pre:
        in_specs += [pl.BlockSpec((CONV_W, qkw), const), pl.BlockSpec((1, qkw), const)]
        args += [m_conv_w, m_conv_b.reshape(1, -1)]
        scratch = [pltpu.VMEM((L + SUBLANES, qkw), F32), pltpu.VMEM((L, qkw), F32)]
    if has_state:
        conv0, c0, n0, m0 = state
        in_specs += [
            pl.BlockSpec((1, CONV_W - 1, qkw), lambda b, c: (b, 0, 0)),
            pl.BlockSpec((1, M_HEADS, M_QK, M_V), per_b),
            pl.BlockSpec((1, M_HEADS, 1, M_QK), per_b),
            pl.BlockSpec((1, M_HEADS, 1, 1), per_b),
        ]
        args += [conv0, c0, n0.reshape(bsz, M_HEADS, 1, M_QK), m0.reshape(bsz, M_HEADS, 1, 1)]
    out_shape = (
        jax.ShapeDtypeStruct((bsz * T, M_V_W), F32),
        jax.ShapeDtypeStruct((bsz, M_HEADS, M_QK, M_V), F32),
        jax.ShapeDtypeStruct((bsz, M_HEADS, 1, M_QK), F32),
        jax.ShapeDtypeStruct((bsz, M_HEADS, 1, 1), F32),
    )
    out_specs = (
        pl.BlockSpec((L, M_V_W), lambda b, c: (row(b, c), 0)),
        pl.BlockSpec((1, M_HEADS, M_QK, M_V), per_b),
        pl.BlockSpec((1, M_HEADS, 1, M_QK), per_b),
        pl.BlockSpec((1, M_HEADS, 1, 1), per_b),
    )
    hm, c1, n1, m1 = pl.pallas_call(
        functools.partial(_mlstm_kernel, L=L, has_state=has_state, pre=pre),
        grid=(bsz, nc),
        in_specs=in_specs,
        out_specs=out_specs,
        out_shape=out_shape,
        scratch_shapes=scratch,
        compiler_params=pltpu.CompilerParams(
            dimension_semantics=("parallel", "arbitrary"), vmem_limit_bytes=VMEM_LIMIT),
        name="mlstm_state" if has_state else "mlstm",
    )(*args)
    return hm, c1, n1.reshape(bsz, M_HEADS, M_QK), m1.reshape(bsz, M_HEADS)


def _ssd_kernel(*refs, L, has_state, pre):
    (z_ref, xs_ref, bp_ref, cp_ref, sm_ref, sb_ref, al_ref, d_ref, ng_ref, e_ref,
     et_ref) = refs[:11]
    refs = refs[11:]
    if not pre:
        w_ref, b_ref = refs[:2]
        refs = refs[2:]
    if has_state:
        cv0_ref, h0_ref = refs[:2]
        refs = refs[2:]
    if pre:
        y_ref, h_ref = refs
        xs_s, bm_s, cm_s = xs_ref, bp_ref, cp_ref
    else:
        y_ref, h_ref, xx_ref, xb_ref, xc_ref, xs_s, bm_s, cm_s = refs

    @pl.when(pl.program_id(1) == 0)
    def _():
        if has_state:
            h_ref[...] = h0_ref[...]
        else:
            h_ref[...] = jnp.zeros_like(h_ref)

    if not pre:
        x_cols = slice(0, S_INNER)
        b_cols = slice(S_INNER, S_INNER + S_BCW)
        c_cols = slice(S_INNER + S_BCW, S_CONV_CH)
        for xp_ref, cols in ((xx_ref, x_cols), (xb_ref, b_cols), (xc_ref, c_cols)):
            @pl.when(pl.program_id(1) == 0)
            def _():
                _init_conv(xp_ref, cv0_ref.at[:, :, cols] if has_state else None)

        _conv_silu(xb_ref, bp_ref, w_ref.at[:, b_cols], b_ref.at[:, b_cols], L, bm_s)
        _conv_silu(xc_ref, cp_ref, w_ref.at[:, c_cols], b_ref.at[:, c_cols], L, cm_s)

    dt_all = _softplus(sm_ref[...] + sb_ref[...])
    da_all = dt_all * (-jnp.exp(al_ref[...]))
    lane = lax.broadcasted_iota(jnp.int32, (L, LANES), 1)
    rr = lax.broadcasted_iota(jnp.int32, (L, L), 0)
    cc = lax.broadcasted_iota(jnp.int32, (L, L), 1)
    tril = cc <= rr
    lo = lane < S_HEADDIM
    cum_all = _dot3(tril.astype(BF16), da_all)
    cum_all_t = _tn3(da_all, (rr <= cc).astype(BF16))
    cum_last = cum_all[L - 1:L, :]
    dec_all = dt_all * jnp.exp(cum_last - cum_all)
    ecum_all = jnp.exp(cum_all)
    mxu_spread = L >= LANES
    if mxu_spread:
        dt_src = _split3(dt_all)[:2]
        dec_src = _split3(dec_all)[:2]
        ecum_src = _split3(ecum_all)[:2]
        clast_src = _split3(jnp.broadcast_to(cum_all_t[:, L - 1:L], (LANES, LANES)))
    else:
        dt_src, dec_src, ecum_src = dt_all, dec_all, ecum_all
        lo_row = lax.broadcasted_iota(jnp.int32, (LANES, 1), 0) < S_HEADDIM

    def spread(src, pair):
        if mxu_spread:
            return sum(jnp.dot(p, e_ref[pair], preferred_element_type=F32) for p in src)
        idx = SM_DT + 2 * pair
        return jnp.where(lo, src[:, idx:idx + 1], src[:, idx + 1:idx + 2])

    def state_decay(pair):
        if mxu_spread:
            return jnp.exp(sum(jnp.dot(et_ref[pair], p, preferred_element_type=F32)
                               for p in clast_src))
        idx = SM_DT + 2 * pair
        return jnp.exp(jnp.where(lo_row, cum_last[:, idx:idx + 1], cum_last[:, idx + 1:idx + 2]))

    def group_stages(g):
        if not pre:
            _conv_silu(xx_ref, xs_ref, w_ref.at[:, x_cols], b_ref.at[:, x_cols], L, xs_s,
                       slabs=range(g * S_GW // CONV_SLAB, (g + 1) * S_GW // CONV_SLAB))
        yield
        bmb = bm_s[:, g * S_STATE:(g + 1) * S_STATE].astype(BF16)
        cmb = cm_s[:, g * S_STATE:(g + 1) * S_STATE].astype(BF16)
        cb = _nt_dot(cmb, bmb)
        ssq = jnp.zeros((L, 1), F32)
        for p in range(S_HPG // 2):
            yield
            pair = g * (S_HPG // 2) + p
            psl = slice(g * S_GW + p * LANES, g * S_GW + (p + 1) * LANES)
            hsl = slice(p * LANES, (p + 1) * LANES)
            lmat = []
            for e in range(2):
                idx = SM_DT + 2 * pair + e
                lmat.append(jnp.exp(jnp.where(
                    tril, cum_all[:, idx:idx + 1] - cum_all_t[idx:idx + 1, :], -jnp.inf)))
            xs = xs_s[:, psl]
            xdt = xs * spread(dt_src, pair)
            xdd = xs * spread(dec_src, pair)
            h_prev = h_ref[0, g, hsl, :]
            yield
            y = (jnp.dot((cb * lmat[0]).astype(BF16), jnp.where(lo, xdt, 0.0).astype(BF16),
                         preferred_element_type=F32)
                 + jnp.dot((cb * lmat[1]).astype(BF16), jnp.where(lo, 0.0, xdt).astype(BF16),
                           preferred_element_type=F32)
                 + spread(ecum_src, pair) * _nt_dot(cmb, h_prev.astype(BF16)))
            h_ref[0, g, hsl, :] = state_decay(pair) * h_prev + _tn_dot(xdd.astype(BF16), bmb)
            yield
            z = z_ref[:, psl]
            if not pre:
                z = _silu(z)
            gated = (y + d_ref[:, psl] * xs) * z
            ssq = ssq + jnp.sum(gated * gated, axis=1, keepdims=True)
            y_ref[:, psl] = gated
        yield
        gsl = slice(g * S_GW, (g + 1) * S_GW)
        y_ref[:, gsl] = y_ref[:, gsl] * lax.rsqrt(ssq * (1.0 / S_GW) + EPS) * ng_ref[:, gsl]

    _run_stages([group_stages(g) for g in range(S_GROUPS)])


def _pair_onehots():
    n_pairs = S_HEADS // 2
    pair = jnp.arange(n_pairs)[:, None, None]
    src = jnp.arange(LANES)[None, :, None]
    dst = jnp.arange(LANES)[None, None, :]
    onehot = src == SM_DT + 2 * pair + dst // S_HEADDIM
    return onehot.astype(BF16), jnp.swapaxes(onehot, 1, 2).astype(BF16)


def _ssd(proj, bsz, T, s_conv_w, s_conv_b, small_bias, alog_row, d_row, s_norm_g, state=None,
         pre=False):
    L = min(T, CHUNK)
    nc = T // L
    has_state = state is not None
    assert not (pre and has_state)
    n_pairs = S_HEADS // 2
    onehot, onehot_t = _pair_onehots()

    def row(b, c):
        return b * nc + c

    const = lambda b, c: (0, 0)
    const3 = lambda b, c: (0, 0, 0)
    in_specs = [
        pl.BlockSpec((L, S_INNER), lambda b, c: (row(b, c), P_Z // S_INNER)),
        pl.BlockSpec((L, S_INNER), lambda b, c: (row(b, c), P_XBC // S_INNER)),
        pl.BlockSpec((L, S_BCW), lambda b, c: (row(b, c), P_B // S_BCW)),
        pl.BlockSpec((L, S_BCW), lambda b, c: (row(b, c), P_C // S_BCW)),
        pl.BlockSpec((L, LANES), lambda b, c: (row(b, c), P_SMALL // LANES)),
        pl.BlockSpec((1, LANES), const),
        pl.BlockSpec((1, LANES), const),
        pl.BlockSpec((1, S_INNER), const),
        pl.BlockSpec((1, S_INNER), const),
        pl.BlockSpec((n_pairs, LANES, LANES), const3),
        pl.BlockSpec((n_pairs, LANES, LANES), const3),
    ]
    args = [proj, proj, proj, proj, proj, small_bias, alog_row, d_row, s_norm_g.reshape(1, -1),
            onehot, onehot_t]
    scratch = []
    if not pre:
        in_specs += [pl.BlockSpec((CONV_W, S_CONV_CH), const),
                     pl.BlockSpec((1, S_CONV_CH), const)]
        args += [s_conv_w, s_conv_b.reshape(1, -1)]
        scratch = [pltpu.VMEM((L + SUBLANES, S_INNER), F32),
                   pltpu.VMEM((L + SUBLANES, S_BCW), F32),
                   pltpu.VMEM((L + SUBLANES, S_BCW), F32),
                   pltpu.VMEM((L, S_INNER), F32),
                   pltpu.VMEM((L, S_BCW), F32),
                   pltpu.VMEM((L, S_BCW), F32)]
    if has_state:
        conv0, h0 = state
        in_specs += [
            pl.BlockSpec((1, CONV_W - 1, S_CONV_CH), lambda b, c: (b, 0, 0)),
            pl.BlockSpec((1, S_GROUPS, S_GW, S_STATE), lambda b, c: (b, 0, 0, 0)),
        ]
        args += [conv0, h0.reshape(bsz, S_GROUPS, S_GW, S_STATE)]
    y, h1 = pl.pallas_call(
        functools.partial(_ssd_kernel, L=L, has_state=has_state, pre=pre),
        grid=(bsz, nc),
        in_specs=in_specs,
        out_specs=(
            pl.BlockSpec((L, S_INNER), lambda b, c: (row(b, c), 0)),
            pl.BlockSpec((1, S_GROUPS, S_GW, S_STATE), lambda b, c: (b, 0, 0, 0)),
        ),
        out_shape=(
            jax.ShapeDtypeStruct((bsz * T, S_INNER), F32),
            jax.ShapeDtypeStruct((bsz, S_GROUPS, S_GW, S_STATE), F32),
        ),
        scratch_shapes=scratch,
        compiler_params=pltpu.CompilerParams(
            dimension_semantics=("parallel", "arbitrary"), vmem_limit_bytes=VMEM_LIMIT),
        name="ssd_state" if has_state else "ssd",
    )(*args)
    return y, h1.reshape(bsz, S_HEADS, S_HEADDIM, S_STATE)


def _merge_kernel(x_ref, hm_ref, gt_ref, ga_ref, gb_ref, g_ref, wpa_ref, wpb_ref, wo_ref, o_ref,
                  wpa_s, wpb_s, wo_s, *, gates_activated):
    @pl.when(pl.program_id(0) == 0)
    def _():
        wpa_s[...] = wpa_ref[...].astype(BF16)
        wpb_s[...] = wpb_ref[...].astype(BF16)
        wo_s[...] = wo_ref[...].astype(BF16)

    ya = jnp.dot(hm_ref[...].astype(BF16), wpa_s[...], preferred_element_type=F32)
    yb = jnp.dot(gt_ref[...].astype(BF16), wpb_s[...], preferred_element_type=F32)
    ga, gb = ga_ref[...], gb_ref[...]
    if not gates_activated:
        ga, gb = _sigmoid(ga), _sigmoid(gb)
    merged = ga * ya + gb * yb
    o_ref[...] = x_ref[...] + g_ref[0] * jnp.dot(
        merged.astype(BF16), wo_s[...], preferred_element_type=F32)


def _merge(x, hm, gated, proj, mod3, rows_per_group, kmod, w_pa, w_pb, w_out, gates_activated):
    n = x.shape[0]
    tm = 256
    const = lambda i: (0, 0)
    return pl.pallas_call(
        functools.partial(_merge_kernel, gates_activated=gates_activated),
        grid=(n // tm,),
        in_specs=[
            pl.BlockSpec((tm, D_MODEL), lambda i: (i, 0)),
            pl.BlockSpec((tm, M_V_W), lambda i: (i, 0)),
            pl.BlockSpec((tm, S_INNER), lambda i: (i, 0)),
            pl.BlockSpec((tm, D_MODEL), lambda i: (i, P_GA // D_MODEL)),
            pl.BlockSpec((tm, D_MODEL), lambda i: (i, P_GB // D_MODEL)),
            _mod_spec(mod3, rows_per_group, tm, kmod),
            pl.BlockSpec((M_V_W, D_MODEL), const, pipeline_mode=pl.Buffered(1)),
            pl.BlockSpec((S_INNER, D_MODEL), const, pipeline_mode=pl.Buffered(1)),
            pl.BlockSpec((D_MODEL, D_MODEL), const, pipeline_mode=pl.Buffered(1)),
        ],
        out_specs=pl.BlockSpec((tm, D_MODEL), lambda i: (i, 0)),
        out_shape=jax.ShapeDtypeStruct((n, D_MODEL), F32),
        scratch_shapes=[pltpu.VMEM((M_V_W, D_MODEL), BF16),
                        pltpu.VMEM((S_INNER, D_MODEL), BF16),
                        pltpu.VMEM((D_MODEL, D_MODEL), BF16)],
        compiler_params=pltpu.CompilerParams(
            dimension_semantics=("arbitrary",), vmem_limit_bytes=VMEM_LIMIT),
        name="merge",
    )(x, hm, gated, proj, proj, mod3, w_pa, w_pb, w_out)


def _layer(x, mod3, rows_per_group, bsz, T, wts, final_norm, state):
    n = bsz * T
    x = x.reshape(n, D_MODEL)
    x = _ffn(x, mod3, rows_per_group, 0, wts["norm_ff1"], wts["ff1_w1"], wts["ff1_w3"],
             wts["ff1_w2"])
    proj = _proj(x, mod3, rows_per_group, 3, wts["norm_mix"], wts["w_packed"])
    tail = proj.reshape(bsz, T, P_WIDTH)[:, T - SUBLANES:]
    if state is None:
        m_state = s_state = None
    else:
        c0, n0, m0, mconv0, ssm0, sconv0 = state
        m_state = (mconv0, c0, n0, m0)
        s_state = (sconv0, ssm0)
    hm, c1, n1, m1 = _mlstm(proj, bsz, T, wts["m_conv_w"], wts["m_conv_b"], wts["small_bias"],
                            wts["m_head_g"], m_state)
    gated, h1 = _ssd(proj, bsz, T, wts["s_conv_w"], wts["s_conv_b"], wts["small_bias"],
                     wts["alog_row"], wts["d_row"], wts["s_norm_g"], s_state)
    x = _merge(x, hm, gated, proj, mod3, rows_per_group, 5, wts["w_pa"], wts["w_pb"],
               wts["w_out"], gates_activated=False)
    y = _ffn(x, mod3, rows_per_group, 6, wts["norm_ff2"], wts["ff2_w1"], wts["ff2_w3"],
             wts["ff2_w2"], final_norm=final_norm)
    last = slice(SUBLANES - (CONV_W - 1), SUBLANES)
    mconv1 = tail[:, last, P_QK:P_QK + 2 * M_QK_W]
    sconv1 = tail[:, last, P_XBC:P_XBC + S_CONV_CH]
    return y.reshape(bsz, T, D_MODEL), (c1, n1, m1, mconv1, h1, sconv1)


def _pack_w_in(w):
    o_if = 2 * M_QK_W + 2 * M_V_W
    o_z = o_if + 2 * M_HEADS
    o_dt = o_z + S_INNER + S_CONV_CH
    o_ga = o_dt + S_HEADS
    pad = P_WIDTH - (P_SMALL + 2 * M_HEADS + S_HEADS)
    return jnp.concatenate(
        [w[:, :o_if], w[:, o_z:o_dt], w[:, o_ga:], w[:, o_if:o_z], w[:, o_dt:o_ga],
         jnp.zeros((w.shape[0], pad), w.dtype)], axis=1).astype(BF16)


def _pack_cols(qk_part, xbc_part, fill):
    rows = qk_part.shape[0]
    gap = jnp.full((rows, P_XBC - P_V), fill, F32)
    rest = jnp.full((rows, P_WIDTH - P_GA), fill, F32)
    return jnp.concatenate([qk_part, gap, xbc_part, rest], axis=1)


def kernel(x_prompt, x_sample, c_prompt, c_sample, state_mlstm_C, state_mlstm_n, state_mlstm_m, state_mlstm_conv, state_ssm, state_ssm_conv, ada_w, ada_b, norm_ff1, ff1_w1, ff1_w3, ff1_w2, norm_mix, w_in, m_conv_w, m_conv_b, m_if_b, m_head_g, w_pa, s_conv_w, s_conv_b, s_dt_bias, s_A_log, s_D, s_norm_g, w_pb, w_out, norm_ff2, ff2_w1, ff2_w3, ff2_w2, final_norm):
    bp, tp, _ = x_prompt.shape
    bs, ts, _ = x_sample.shape
    depth = ada_w.shape[0]
    assert depth == 1

    hp, hs = x_prompt, x_sample
    new_p, new_s = [], []
    for l in range(depth):
        c_all = jnp.concatenate([c_prompt, c_sample], axis=0)
        pad_rows = (-c_all.shape[0]) % (2 * SUBLANES)
        c_all = jnp.pad(c_all, ((0, pad_rows), (0, 0)))
        mod = _ada(c_all, ada_w[l], ada_b[l])
        mod_p = mod[:bp].reshape(bp, 1, N_MOD * D_MODEL)
        mod_s = jnp.repeat(mod[bp:bp + bs], ts, axis=0).reshape(1, bs * ts, N_MOD * D_MODEL)

        lane_pad = LANES - (2 * M_HEADS + S_HEADS)
        small_bias = jnp.concatenate(
            [m_if_b[l], s_dt_bias[l], jnp.zeros((lane_pad,), F32)]).reshape(1, LANES)
        alog_row = jnp.concatenate(
            [jnp.zeros((2 * M_HEADS,), F32), s_A_log[l], jnp.zeros((lane_pad,), F32)]
        ).reshape(1, LANES)
        wts = {
            "norm_ff1": norm_ff1[l], "ff1_w1": ff1_w1[l], "ff1_w3": ff1_w3[l],
            "ff1_w2": ff1_w2[l],
            "norm_mix": norm_mix[l], "w_packed": _pack_w_in(w_in[l]),
            "m_conv_w": m_conv_w[l], "m_conv_b": m_conv_b[l], "small_bias": small_bias,
            "m_head_g": m_head_g[l], "w_pa": w_pa[l],
            "s_conv_w": s_conv_w[l], "s_conv_b": s_conv_b[l], "alog_row": alog_row,
            "d_row": jnp.repeat(s_D[l], S_HEADDIM).reshape(1, S_INNER),
            "s_norm_g": s_norm_g[l], "w_pb": w_pb[l], "w_out": w_out[l],
            "norm_ff2": norm_ff2[l], "ff2_w1": ff2_w1[l], "ff2_w3": ff2_w3[l],
            "ff2_w2": ff2_w2[l],
        }
        st_s = (state_mlstm_C[l], state_mlstm_n[l], state_mlstm_m[l], state_mlstm_conv[l],
                state_ssm[l], state_ssm_conv[l])
        hp, sp = _layer(hp, mod_p, tp, bp, tp, wts, final_norm, None)
        hs, ss = _layer(hs, mod_s, bs * ts, bs, ts, wts, final_norm, st_s)
        new_p.append(sp)
        new_s.append(ss)
    outs_p = [jnp.stack([s[i] for s in new_p]) for i in range(6)]
    outs_s = [jnp.stack([s[i] for s in new_s]) for i in range(6)]
    return (hp, hs, *outs_p, *outs_s)
```

```python
import functools

import jax
import jax.numpy as jnp
from jax import lax
from jax.experimental import pallas as pl
from jax.experimental.pallas import tpu as pltpu

F32 = jnp.float32
BF16 = jnp.bfloat16

D_MODEL = 1024
M_HEADS = 4
M_QK = 256
M_V = 512
M_QK_W = M_HEADS * M_QK
M_V_W = M_HEADS * M_V
S_INNER = 2048
S_HEADDIM = 64
S_HEADS = 32
S_GROUPS = 4
S_HPG = S_HEADS // S_GROUPS
S_STATE = 128
S_GW = S_INNER // S_GROUPS
S_BCW = S_GROUPS * S_STATE
S_CONV_CH = S_INNER + 2 * S_BCW
CONV_W = 4
D_FF = 2816
CHUNK = 128
EPS = 1e-6
N_MOD = 9
K_SCALE = M_QK ** -0.5

LANES = 128
SUBLANES = 8
VMEM_LIMIT = 56 * 1024 * 1024

P_QK = 0
P_V = 2 * M_QK_W
P_O = P_V + M_V_W
P_HEAD = P_O + M_V_W
P_Z = P_HEAD
P_XBC = P_Z + S_INNER
P_B = P_XBC + S_INNER
P_C = P_B + S_BCW
P_GA = P_XBC + S_CONV_CH
P_GB = P_GA + D_MODEL
P_SMALL = P_GB + D_MODEL
PROJ_TN = 1536
P_WIDTH = 13824
SM_I = 0
SM_F = M_HEADS
SM_DT = 2 * M_HEADS

FFN_TM = 2048
FFN_TF = 256
PROJ_TM = 1024
MERGE_TM = 256
SHORT_SEQS_PER_STEP = 4
CONV_SLAB = 256


def _sigmoid(x):
    return 0.5 * jnp.tanh(0.5 * x) + 0.5


def _silu(x):
    h = 0.5 * x
    return h * jnp.tanh(h) + h


def _rms(x, g):
    return x * lax.rsqrt(jnp.mean(x * x, axis=-1, keepdims=True) + EPS) * g


def _ada_kernel(c_ref, w_ref, b_ref, o_ref):
    s = _silu(c_ref[...]).astype(BF16)
    o_ref[...] = jnp.dot(s, w_ref[...].astype(BF16), preferred_element_type=F32) + b_ref[...]


def _ada(c_all, ada_w, ada_b):
    m = c_all.shape[0]
    n = ada_w.shape[1]
    tn = D_MODEL
    return pl.pallas_call(
        _ada_kernel,
        grid=(n // tn,),
        in_specs=[
            pl.BlockSpec((m, D_MODEL), lambda j: (0, 0)),
            pl.BlockSpec((D_MODEL, tn), lambda j: (0, j)),
            pl.BlockSpec((1, tn), lambda j: (0, j)),
        ],
        out_specs=pl.BlockSpec((m, tn), lambda j: (0, j)),
        out_shape=jax.ShapeDtypeStruct((m, n), F32),
        compiler_params=pltpu.CompilerParams(
            dimension_semantics=("arbitrary",), vmem_limit_bytes=VMEM_LIMIT),
        name="ada_mod",
    )(c_all, ada_w, ada_b.reshape(1, n))


def _ffn_kernel(x_ref, sh_ref, sc_ref, g_ref, nw_ref, w1_ref, w3_ref, w2_ref, *rest,
                nf, final):
    if final:
        fn_ref, o_ref, u_ref = rest
    else:
        o_ref, u_ref = rest
    f = pl.program_id(1)

    @pl.when(f == 0)
    def _():
        xn = _rms(x_ref[...], nw_ref[...])
        u_ref[...] = (xn * (1.0 + sc_ref[0]) + sh_ref[0]).astype(BF16)
        o_ref[...] = jnp.zeros_like(o_ref)

    u = u_ref[...]
    h1 = jnp.dot(u, w1_ref[...].astype(BF16), preferred_element_type=F32)
    h3 = jnp.dot(u, w3_ref[...].astype(BF16), preferred_element_type=F32)
    a = (_silu(h1) * h3).astype(BF16)
    o_ref[...] += jnp.dot(a, w2_ref[...].astype(BF16), preferred_element_type=F32)

    @pl.when(f == nf - 1)
    def _():
        y = x_ref[...] + (0.5 * g_ref[0]) * o_ref[...]
        if final:
            y = _rms(y, fn_ref[...])
        o_ref[...] = y


def _mod_spec(mod3, rows_per_group, tm, k):
    if mod3.shape[1] == 1:
        tiles_per_group = rows_per_group // tm
        return pl.BlockSpec((1, 1, D_MODEL), lambda i, *_: (i // tiles_per_group, 0, k))
    return pl.BlockSpec((1, tm, D_MODEL), lambda i, *_: (0, i, k))


def _ffn(x, mod3, rows_per_group, kmod, norm_w, w1, w3, w2, final_norm=None):
    n = x.shape[0]
    tm = min(FFN_TM, rows_per_group)
    tf = FFN_TF
    nf = D_FF // tf
    final = final_norm is not None
    in_specs = [
        pl.BlockSpec((tm, D_MODEL), lambda i, f: (i, 0)),
        _mod_spec(mod3, rows_per_group, tm, kmod),
        _mod_spec(mod3, rows_per_group, tm, kmod + 1),
        _mod_spec(mod3, rows_per_group, tm, kmod + 2),
        pl.BlockSpec((1, D_MODEL), lambda i, f: (0, 0)),
        pl.BlockSpec((D_MODEL, tf), lambda i, f: (0, f)),
        pl.BlockSpec((D_MODEL, tf), lambda i, f: (0, f)),
        pl.BlockSpec((tf, D_MODEL), lambda i, f: (f, 0)),
    ]
    args = [x, mod3, mod3, mod3, norm_w.reshape(1, D_MODEL), w1, w3, w2]
    if final:
        in_specs.append(pl.BlockSpec((1, D_MODEL), lambda i, f: (0, 0)))
        args.append(final_norm.reshape(1, D_MODEL))
    return pl.pallas_call(
        functools.partial(_ffn_kernel, nf=nf, final=final),
        grid=(n // tm, nf),
        in_specs=in_specs,
        out_specs=pl.BlockSpec((tm, D_MODEL), lambda i, f: (i, 0)),
        out_shape=jax.ShapeDtypeStruct((n, D_MODEL), F32),
        scratch_shapes=[pltpu.VMEM((tm, D_MODEL), BF16)],
        compiler_params=pltpu.CompilerParams(
            dimension_semantics=("parallel", "arbitrary"), vmem_limit_bytes=VMEM_LIMIT),
        name="ffn_final" if final else "ffn",
    )(*args)


def _proj_kernel(x_ref, sh_ref, sc_ref, nw_ref, wh_ref, wt_ref, o_ref, u_ref, *, n_head):
    j = pl.program_id(1)

    @pl.when(j == 0)
    def _():
        xn = _rms(x_ref[...], nw_ref[...])
        u_ref[...] = (xn * (1.0 + sc_ref[0]) + sh_ref[0]).astype(BF16)

    @pl.when(j < n_head)
    def _():
        o_ref[...] = jnp.dot(u_ref[...], wh_ref[...].astype(BF16), preferred_element_type=F32)

    @pl.when(j >= n_head)
    def _():
        o_ref[...] = jnp.dot(u_ref[...], wt_ref[...], preferred_element_type=F32)


def _proj(x, mod3, rows_per_group, kmod, norm_w, w_in, w_tail):
    n = x.shape[0]
    tm = min(PROJ_TM, rows_per_group)
    tn = PROJ_TN
    n_head = P_HEAD // tn
    return pl.pallas_call(
        functools.partial(_proj_kernel, n_head=n_head),
        grid=(n // tm, P_WIDTH // tn),
        in_specs=[
            pl.BlockSpec((tm, D_MODEL), lambda i, j: (i, 0)),
            _mod_spec(mod3, rows_per_group, tm, kmod),
            _mod_spec(mod3, rows_per_group, tm, kmod + 1),
            pl.BlockSpec((1, D_MODEL), lambda i, j: (0, 0)),
            pl.BlockSpec((D_MODEL, tn), lambda i, j: (0, jnp.minimum(j, n_head - 1))),
            pl.BlockSpec((D_MODEL, tn), lambda i, j: (0, jnp.maximum(j - n_head, 0))),
        ],
        out_specs=pl.BlockSpec((tm, tn), lambda i, j: (i, j)),
        out_shape=jax.ShapeDtypeStruct((n, P_WIDTH), F32),
        scratch_shapes=[pltpu.VMEM((tm, D_MODEL), BF16)],
        compiler_params=pltpu.CompilerParams(
            dimension_semantics=("parallel", "arbitrary"), vmem_limit_bytes=VMEM_LIMIT),
        name="mix_proj",
    )(x, mod3, mod3, norm_w.reshape(1, D_MODEL), w_in, w_tail)


def _conv_silu(xp_ref, x_ref, w_ref, b_ref, L, dst_ref, slabs=None):
    if slabs is None:
        slabs = range(x_ref.shape[1] // CONV_SLAB)
    for s in slabs:
        cols = slice(s * CONV_SLAB, (s + 1) * CONV_SLAB)
        xp_ref[pl.ds(SUBLANES, L), cols] = x_ref[:, cols]
        xp = xp_ref[:, cols]
        acc = b_ref[:, cols] + xp[SUBLANES:] * w_ref[CONV_W - 1:CONV_W, cols]
        for back in range(1, CONV_W):
            shifted = pltpu.roll(xp, back, axis=0)[SUBLANES:]
            acc = acc + shifted * w_ref[CONV_W - 1 - back:CONV_W - back, cols]
        dst_ref[:, cols] = _silu(acc)
        xp_ref[pl.ds(0, SUBLANES), cols] = xp_ref[pl.ds(L, SUBLANES), cols]


def _init_conv(xp_ref, init_ref):
    xp_ref[pl.ds(0, SUBLANES), :] = jnp.zeros((SUBLANES, xp_ref.shape[1]), F32)
    if init_ref is not None:
        xp_ref[pl.ds(SUBLANES - (CONV_W - 1), CONV_W - 1), :] = init_ref[0]


def _run_stages(streams):
    live = list(streams)
    while live:
        for stream in list(live):
            try:
                next(stream)
            except StopIteration:
                live.remove(stream)


def _split3(a):
    hi = a.astype(BF16)
    r1 = a - hi.astype(F32)
    mid = r1.astype(BF16)
    lo = (r1 - mid.astype(F32)).astype(BF16)
    return hi, mid, lo


def _nt_dot(a, b):
    return lax.dot_general(a, b, (((1,), (1,)), ((), ())), preferred_element_type=F32)


def _tn_dot(a, b):
    return lax.dot_general(a, b, (((0,), (0,)), ((), ())), preferred_element_type=F32)


def _dot3(mask_b, a):
    return sum(jnp.dot(mask_b, p, preferred_element_type=F32) for p in _split3(a))


def _tn3(a, mask_b):
    return sum(_tn_dot(p, mask_b) for p in _split3(a))


def _log_sigmoid(x):
    return jnp.minimum(x, 0.0) - jnp.log1p(jnp.exp(-jnp.abs(x)))


def _softplus(x):
    return jnp.maximum(x, 0.0) + jnp.log1p(jnp.exp(-jnp.abs(x)))


def _mlstm_kernel(*refs, L, nseq, has_state):
    qk_ref, v_ref, og_ref, sm_ref, sb_ref, hg_ref, w_ref, b_ref = refs[:8]
    refs = refs[8:]
    if has_state:
        cv0_ref, c0_ref, n0_ref, m0_ref = refs[:4]
        refs = refs[4:]
    o_ref, c_ref, n_ref, m_ref, xp_ref, qk_s = refs

    @pl.when(pl.program_id(1) == 0)
    def _():
        if has_state:
            c_ref[...] = c0_ref[...]
            n_ref[...] = n0_ref[...]
            m_ref[...] = m0_ref[...]
        else:
            c_ref[...] = jnp.zeros_like(c_ref)
            n_ref[...] = jnp.zeros_like(n_ref)
            m_ref[...] = jnp.zeros_like(m_ref)
        for s in range(nseq):
            _init_conv(xp_ref.at[s], cv0_ref.at[pl.ds(s, 1)] if has_state else None)

    rr = lax.broadcasted_iota(jnp.int32, (L, L), 0)
    cc = lax.broadcasted_iota(jnp.int32, (L, L), 1)
    tril = cc <= rr
    gates = []
    for s in range(nseq):
        sm = sm_ref[pl.ds(s * L, L), :] + sb_ref[...]
        lf_all = _log_sigmoid(sm)
        b_all = _dot3(tril.astype(BF16), lf_all)
        b_all_t = _tn3(lf_all, (rr <= cc).astype(BF16))
        sm_t = _tn3(sm, (rr == cc).astype(BF16))
        gates.append((sm, sm_t, b_all, b_all_t))

    def head_stages(s, h):
        rows = pl.ds(s * L, L)
        sm, sm_t, b_all, b_all_t = gates[s]
        qk_seq = qk_s.at[s]
        _conv_silu(xp_ref.at[s], qk_ref.at[rows], w_ref, b_ref, L, qk_seq,
                   slabs=[h * M_QK // CONV_SLAB, (M_QK_W + h * M_QK) // CONV_SLAB])
        yield
        q = qk_seq[:, h * M_QK:(h + 1) * M_QK]
        k = qk_seq[:, M_QK_W + h * M_QK:M_QK_W + (h + 1) * M_QK] * K_SCALE
        v = v_ref[rows, h * M_V:(h + 1) * M_V]
        i_col = sm[:, SM_I + h:SM_I + h + 1]
        i_row = sm_t[SM_I + h:SM_I + h + 1, :]
        b_col = b_all[:, SM_F + h:SM_F + h + 1]
        b_row = b_all_t[SM_F + h:SM_F + h + 1, :]

        c_prev = c_ref[s, h]
        n_prev = n_ref[s, h]
        m_prev = m_ref[s, h]

        dmat = jnp.where(tril, b_col - b_row + i_row, -jnp.inf)
        inter = b_col + m_prev
        mt = jnp.maximum(inter, jnp.max(dmat, axis=1, keepdims=True))
        w_tok = jnp.exp(dmat - mt)
        w_st = jnp.exp(inter - mt)
        yield

        qb = q.astype(BF16)
        kb = k.astype(BF16)
        vb = v.astype(BF16)
        sc = _nt_dot(qb, kb) * w_tok
        yield
        num = (jnp.dot(sc.astype(BF16), vb, preferred_element_type=F32)
               + w_st * jnp.dot(qb, c_prev.astype(BF16), preferred_element_type=F32))
        den = (jnp.sum(sc, axis=1, keepdims=True)
               + w_st * jnp.sum(q * n_prev, axis=1, keepdims=True))
        yield
        hh = num / jnp.maximum(jnp.abs(den), jnp.exp(-mt))

        m_new = mt[L - 1:L, :]
        b_last = b_col[L - 1:L, :]
        ws = jnp.exp(b_last - b_col + i_col - m_new)
        decay = jnp.exp(b_last + m_prev - m_new)
        kw = k * ws
        c_ref[s, h] = decay * c_prev + _tn_dot(kw.astype(BF16), vb)
        n_ref[s, h] = decay * n_prev + jnp.sum(kw, axis=0, keepdims=True)
        m_ref[s, h] = m_new
        yield

        mu = jnp.mean(hh, axis=-1, keepdims=True)
        hc = hh - mu
        yield
        var = jnp.mean(hc * hc, axis=-1, keepdims=True)
        hsl = slice(h * M_V, (h + 1) * M_V)
        o_ref[rows, hsl] = (hc * lax.rsqrt(var + EPS) * hg_ref[:, hsl]
                            * _sigmoid(og_ref[rows, hsl]))

    _run_stages([head_stages(s, h) for s in range(nseq) for h in range(M_HEADS)])


def _seqs_per_step(bsz, nc):
    return SHORT_SEQS_PER_STEP if nc == 1 and bsz % SHORT_SEQS_PER_STEP == 0 else 1


def _mlstm(proj, bsz, T, m_conv_w, m_conv_b, small_bias, m_head_g, state=None):
    L = min(T, CHUNK)
    nc = T // L
    nseq = _seqs_per_step(bsz, nc)
    rows = nseq * L
    has_state = state is not None
    qkw = 2 * M_QK_W

    def row(b, c):
        return b * nc + c

    const = lambda b, c: (0, 0)
    per_b = lambda b, c: (b, 0, 0, 0)
    in_specs = [
        pl.BlockSpec((rows, qkw), lambda b, c: (row(b, c), P_QK // qkw)),
        pl.BlockSpec((rows, M_V_W), lambda b, c: (row(b, c), P_V // M_V_W)),
        pl.BlockSpec((rows, M_V_W), lambda b, c: (row(b, c), P_O // M_V_W)),
        pl.BlockSpec((rows, LANES), lambda b, c: (row(b, c), P_SMALL // LANES)),
        pl.BlockSpec((1, LANES), const),
        pl.BlockSpec((1, M_V_W), const),
        pl.BlockSpec((CONV_W, qkw), const),
        pl.BlockSpec((1, qkw), const),
    ]
    args = [proj, proj, proj, proj, small_bias, m_head_g.reshape(1, -1),
            m_conv_w, m_conv_b.reshape(1, -1)]
    scratch = [pltpu.VMEM((nseq, L + SUBLANES, qkw), F32), pltpu.VMEM((nseq, L, qkw), F32)]
    if has_state:
        conv0, c0, n0, m0 = state
        in_specs += [
            pl.BlockSpec((nseq, CONV_W - 1, qkw), lambda b, c: (b, 0, 0)),
            pl.BlockSpec((nseq, M_HEADS, M_QK, M_V), per_b),
            pl.BlockSpec((nseq, M_HEADS, 1, M_QK), per_b),
            pl.BlockSpec((nseq, M_HEADS, 1, 1), per_b),
        ]
        args += [conv0, c0, n0.reshape(bsz, M_HEADS, 1, M_QK), m0.reshape(bsz, M_HEADS, 1, 1)]
    out_shape = (
        jax.ShapeDtypeStruct((bsz * T, M_V_W), F32),
        jax.ShapeDtypeStruct((bsz, M_HEADS, M_QK, M_V), F32),
        jax.ShapeDtypeStruct((bsz, M_HEADS, 1, M_QK), F32),
        jax.ShapeDtypeStruct((bsz, M_HEADS, 1, 1), F32),
    )
    out_specs = (
        pl.BlockSpec((rows, M_V_W), lambda b, c: (row(b, c), 0)),
        pl.BlockSpec((nseq, M_HEADS, M_QK, M_V), per_b),
        pl.BlockSpec((nseq, M_HEADS, 1, M_QK), per_b),
        pl.BlockSpec((nseq, M_HEADS, 1, 1), per_b),
    )
    out, c1, n1, m1 = pl.pallas_call(
        functools.partial(_mlstm_kernel, L=L, nseq=nseq, has_state=has_state),
        grid=(bsz // nseq, nc),
        in_specs=in_specs,
        out_specs=out_specs,
        out_shape=out_shape,
        scratch_shapes=scratch,
        compiler_params=pltpu.CompilerParams(
            dimension_semantics=("arbitrary", "arbitrary"), vmem_limit_bytes=VMEM_LIMIT),
        name="mlstm_state" if has_state else "mlstm",
    )(*args)
    return out, c1, n1.reshape(bsz, M_HEADS, M_QK), m1.reshape(bsz, M_HEADS)


def _ssd_kernel(*refs, L, nseq, has_state):
    n_shared = 13
    streams = []
    for sq in range(nseq):
        rows, one = pl.ds(sq * L, L), pl.ds(sq, 1)
        views = [r.at[rows] for r in refs[:5]] + list(refs[5:n_shared])
        rest = refs[n_shared:]
        if has_state:
            views += [r.at[one] for r in rest[:2]]
            rest = rest[2:]
        o_ref, h_ref = rest[:2]
        views += [o_ref.at[rows], h_ref.at[one]] + [r.at[sq] for r in rest[2:]]
        streams += _ssd_seq_streams(*views, L=L, has_state=has_state)
    _run_stages(streams)


def _ssd_seq_streams(*refs, L, has_state):
    (z_ref, xs_ref, bp_ref, cp_ref, sm_ref, sb_ref, al_ref, d_ref, ng_ref, e_ref, et_ref,
     w_ref, b_ref) = refs[:13]
    refs = refs[13:]
    if has_state:
        cv0_ref, h0_ref = refs[:2]
        refs = refs[2:]
    o_ref, h_ref, xx_ref, xb_ref, xc_ref, xs_s, bm_s, cm_s = refs

    @pl.when(pl.program_id(1) == 0)
    def _():
        if has_state:
            h_ref[...] = h0_ref[...]
        else:
            h_ref[...] = jnp.zeros_like(h_ref)

    x_cols = slice(0, S_INNER)
    b_cols = slice(S_INNER, S_INNER + S_BCW)
    c_cols = slice(S_INNER + S_BCW, S_CONV_CH)
    for xp_ref, cols in ((xx_ref, x_cols), (xb_ref, b_cols), (xc_ref, c_cols)):
        @pl.when(pl.program_id(1) == 0)
        def _():
            _init_conv(xp_ref, cv0_ref.at[:, :, cols] if has_state else None)

    _conv_silu(xb_ref, bp_ref, w_ref.at[:, b_cols], b_ref.at[:, b_cols], L, bm_s)
    _conv_silu(xc_ref, cp_ref, w_ref.at[:, c_cols], b_ref.at[:, c_cols], L, cm_s)

    dt_all = _softplus(sm_ref[...] + sb_ref[...])
    da_all = dt_all * (-jnp.exp(al_ref[...]))
    lane = lax.broadcasted_iota(jnp.int32, (L, LANES), 1)
    rr = lax.broadcasted_iota(jnp.int32, (L, L), 0)
    cc = lax.broadcasted_iota(jnp.int32, (L, L), 1)
    tril = cc <= rr
    lo = lane < S_HEADDIM
    cum_all = _dot3(tril.astype(BF16), da_all)
    cum_all_t = _tn3(da_all, (rr <= cc).astype(BF16))
    cum_last = cum_all[L - 1:L, :]
    dec_all = dt_all * jnp.exp(cum_last - cum_all)
    ecum_all = jnp.exp(cum_all)
    mxu_spread = L >= LANES
    if mxu_spread:
        dt_src = _split3(dt_all)[:2]
        dec_src = _split3(dec_all)[:2]
        ecum_src = _split3(ecum_all)[:2]
        clast_src = _split3(jnp.broadcast_to(cum_all_t[:, L - 1:L], (LANES, LANES)))
    else:
        dt_src, dec_src, ecum_src = dt_all, dec_all, ecum_all
        lo_row = lax.broadcasted_iota(jnp.int32, (LANES, 1), 0) < S_HEADDIM

    def spread(src, pair):
        if mxu_spread:
            return sum(jnp.dot(p, e_ref[pair], preferred_element_type=F32) for p in src)
        idx = SM_DT + 2 * pair
        return jnp.where(lo, src[:, idx:idx + 1], src[:, idx + 1:idx + 2])

    def state_decay(pair):
        if mxu_spread:
            return jnp.exp(sum(jnp.dot(et_ref[pair], p, preferred_element_type=F32)
                               for p in clast_src))
        idx = SM_DT + 2 * pair
        return jnp.exp(jnp.where(lo_row, cum_last[:, idx:idx + 1], cum_last[:, idx + 1:idx + 2]))

    def group_stages(g):
        _conv_silu(xx_ref, xs_ref, w_ref.at[:, x_cols], b_ref.at[:, x_cols], L, xs_s,
                   slabs=range(g * S_GW // CONV_SLAB, (g + 1) * S_GW // CONV_SLAB))
        yield
        bmb = bm_s[:, g * S_STATE:(g + 1) * S_STATE].astype(BF16)
        cmb = cm_s[:, g * S_STATE:(g + 1) * S_STATE].astype(BF16)
        cb = _nt_dot(cmb, bmb)
        ssq = jnp.zeros((L, 1), F32)
        for p in range(S_HPG // 2):
            yield
            pair = g * (S_HPG // 2) + p
            psl = slice(g * S_GW + p * LANES, g * S_GW + (p + 1) * LANES)
            hsl = slice(p * LANES, (p + 1) * LANES)
            lmat = []
            for e in range(2):
                idx = SM_DT + 2 * pair + e
                lmat.append(jnp.exp(jnp.where(
                    tril, cum_all[:, idx:idx + 1] - cum_all_t[idx:idx + 1, :], -jnp.inf)))
            xs = xs_s[:, psl]
            xdt = xs * spread(dt_src, pair)
            xdd = xs * spread(dec_src, pair)
            h_prev = h_ref[0, g, hsl, :]
            yield
            y = (jnp.dot((cb * lmat[0]).astype(BF16), jnp.where(lo, xdt, 0.0).astype(BF16),
                         preferred_element_type=F32)
                 + jnp.dot((cb * lmat[1]).astype(BF16), jnp.where(lo, 0.0, xdt).astype(BF16),
                           preferred_element_type=F32)
                 + spread(ecum_src, pair) * _nt_dot(cmb, h_prev.astype(BF16)))
            h_ref[0, g, hsl, :] = state_decay(pair) * h_prev + _tn_dot(xdd.astype(BF16), bmb)
            yield
            gated = (y + d_ref[:, psl] * xs) * _silu(z_ref[:, psl])
            ssq = ssq + jnp.sum(gated * gated, axis=1, keepdims=True)
            o_ref[:, psl] = gated
        yield
        gsl = slice(g * S_GW, (g + 1) * S_GW)
        o_ref[:, gsl] = o_ref[:, gsl] * lax.rsqrt(ssq * (1.0 / S_GW) + EPS) * ng_ref[:, gsl]

    return [group_stages(g) for g in range(S_GROUPS)]


def _pair_onehots():
    n_pairs = S_HEADS // 2
    pair = jnp.arange(n_pairs)[:, None, None]
    src = jnp.arange(LANES)[None, :, None]
    dst = jnp.arange(LANES)[None, None, :]
    onehot = src == SM_DT + 2 * pair + dst // S_HEADDIM
    return onehot.astype(BF16), jnp.swapaxes(onehot, 1, 2).astype(BF16)


def _ssd(proj, bsz, T, s_conv_w, s_conv_b, small_bias, alog_row, d_row, s_norm_g, state=None):
    L = min(T, CHUNK)
    nc = T // L
    nseq = _seqs_per_step(bsz, nc)
    rows = nseq * L
    has_state = state is not None
    n_pairs = S_HEADS // 2
    onehot, onehot_t = _pair_onehots()

    def row(b, c):
        return b * nc + c

    const = lambda b, c: (0, 0)
    const3 = lambda b, c: (0, 0, 0)
    in_specs = [
        pl.BlockSpec((rows, S_INNER), lambda b, c: (row(b, c), P_Z // S_INNER)),
        pl.BlockSpec((rows, S_INNER), lambda b, c: (row(b, c), P_XBC // S_INNER)),
        pl.BlockSpec((rows, S_BCW), lambda b, c: (row(b, c), P_B // S_BCW)),
        pl.BlockSpec((rows, S_BCW), lambda b, c: (row(b, c), P_C // S_BCW)),
        pl.BlockSpec((rows, LANES), lambda b, c: (row(b, c), P_SMALL // LANES)),
        pl.BlockSpec((1, LANES), const),
        pl.BlockSpec((1, LANES), const),
        pl.BlockSpec((1, S_INNER), const),
        pl.BlockSpec((1, S_INNER), const),
        pl.BlockSpec((n_pairs, LANES, LANES), const3),
        pl.BlockSpec((n_pairs, LANES, LANES), const3),
        pl.BlockSpec((CONV_W, S_CONV_CH), const),
        pl.BlockSpec((1, S_CONV_CH), const),
    ]
    args = [proj, proj, proj, proj, proj, small_bias, alog_row, d_row, s_norm_g.reshape(1, -1),
            onehot, onehot_t, s_conv_w, s_conv_b.reshape(1, -1)]
    scratch = [pltpu.VMEM((nseq, L + SUBLANES, S_INNER), F32),
               pltpu.VMEM((nseq, L + SUBLANES, S_BCW), F32),
               pltpu.VMEM((nseq, L + SUBLANES, S_BCW), F32),
               pltpu.VMEM((nseq, L, S_INNER), F32),
               pltpu.VMEM((nseq, L, S_BCW), F32),
               pltpu.VMEM((nseq, L, S_BCW), F32)]
    if has_state:
        conv0, h0 = state
        in_specs += [
            pl.BlockSpec((nseq, CONV_W - 1, S_CONV_CH), lambda b, c: (b, 0, 0)),
            pl.BlockSpec((nseq, S_GROUPS, S_GW, S_STATE), lambda b, c: (b, 0, 0, 0)),
        ]
        args += [conv0, h0.reshape(bsz, S_GROUPS, S_GW, S_STATE)]
    out, h1 = pl.pallas_call(
        functools.partial(_ssd_kernel, L=L, nseq=nseq, has_state=has_state),
        grid=(bsz // nseq, nc),
        in_specs=in_specs,
        out_specs=(
            pl.BlockSpec((rows, S_INNER), lambda b, c: (row(b, c), 0)),
            pl.BlockSpec((nseq, S_GROUPS, S_GW, S_STATE), lambda b, c: (b, 0, 0, 0)),
        ),
        out_shape=(
            jax.ShapeDtypeStruct((bsz * T, S_INNER), F32),
            jax.ShapeDtypeStruct((bsz, S_GROUPS, S_GW, S_STATE), F32),
        ),
        scratch_shapes=scratch,
        compiler_params=pltpu.CompilerParams(
            dimension_semantics=("arbitrary", "arbitrary"), vmem_limit_bytes=VMEM_LIMIT),
        name="ssd_state" if has_state else "ssd",
    )(*args)
    return out, h1.reshape(bsz, S_HEADS, S_HEADDIM, S_STATE)


def _merge_kernel(x_ref, hm_ref, gt_ref, ga_ref, gb_ref, g_ref, wpa_ref, wpb_ref, wo_ref, o_ref,
                  wpa_s, wpb_s, wo_s):
    @pl.when(pl.program_id(0) == 0)
    def _():
        wpa_s[...] = wpa_ref[...].astype(BF16)
        wpb_s[...] = wpb_ref[...].astype(BF16)
        wo_s[...] = wo_ref[...].astype(BF16)

    ya = jnp.dot(hm_ref[...].astype(BF16), wpa_s[...], preferred_element_type=F32)
    yb = jnp.dot(gt_ref[...].astype(BF16), wpb_s[...], preferred_element_type=F32)
    merged = _sigmoid(ga_ref[...]) * ya + _sigmoid(gb_ref[...]) * yb
    o_ref[...] = x_ref[...] + g_ref[0] * jnp.dot(
        merged.astype(BF16), wo_s[...], preferred_element_type=F32)


def _merge(x, hm, gated, proj, mod3, rows_per_group, kmod, w_pa, w_pb, w_out):
    n = x.shape[0]
    tm = MERGE_TM
    const = lambda i: (0, 0)
    resident = functools.partial(pl.BlockSpec, index_map=const, pipeline_mode=pl.Buffered(1))
    return pl.pallas_call(
        _merge_kernel,
        grid=(n // tm,),
        in_specs=[
            pl.BlockSpec((tm, D_MODEL), lambda i: (i, 0)),
            pl.BlockSpec((tm, M_V_W), lambda i: (i, 0)),
            pl.BlockSpec((tm, S_INNER), lambda i: (i, 0)),
            pl.BlockSpec((tm, D_MODEL), lambda i: (i, P_GA // D_MODEL)),
            pl.BlockSpec((tm, D_MODEL), lambda i: (i, P_GB // D_MODEL)),
            _mod_spec(mod3, rows_per_group, tm, kmod),
            resident((M_V_W, D_MODEL)),
            resident((S_INNER, D_MODEL)),
            resident((D_MODEL, D_MODEL)),
        ],
        out_specs=pl.BlockSpec((tm, D_MODEL), lambda i: (i, 0)),
        out_shape=jax.ShapeDtypeStruct((n, D_MODEL), F32),
        scratch_shapes=[pltpu.VMEM((M_V_W, D_MODEL), BF16),
                        pltpu.VMEM((S_INNER, D_MODEL), BF16),
                        pltpu.VMEM((D_MODEL, D_MODEL), BF16)],
        compiler_params=pltpu.CompilerParams(
            dimension_semantics=("arbitrary",), vmem_limit_bytes=VMEM_LIMIT),
        name="merge",
    )(x, hm, gated, proj, proj, mod3, w_pa, w_pb, w_out)


def _layer(x, mod3, rows_per_group, bsz, T, wts, final_norm, state):
    n = bsz * T
    x = x.reshape(n, D_MODEL)
    x = _ffn(x, mod3, rows_per_group, 0, wts["norm_ff1"], wts["ff1_w1"], wts["ff1_w3"],
             wts["ff1_w2"])
    proj = _proj(x, mod3, rows_per_group, 3, wts["norm_mix"], wts["w_in"], wts["w_tail"])
    mixer_args = (proj, bsz, T)
    m_args = (wts["m_conv_w"], wts["m_conv_b"], wts["small_bias"], wts["m_head_g"])
    s_args = (wts["s_conv_w"], wts["s_conv_b"], wts["small_bias"], wts["alog_row"],
              wts["d_row"], wts["s_norm_g"])
    if state is None:
        m_state = s_state = None
    else:
        c0, n0, m0, mconv0, ssm0, sconv0 = state
        m_state = (mconv0, c0, n0, m0)
        s_state = (sconv0, ssm0)
    hm, c1, n1, m1 = _mlstm(*mixer_args, *m_args, state=m_state)
    gated, h1 = _ssd(*mixer_args, *s_args, state=s_state)
    x = _merge(x, hm, gated, proj, mod3, rows_per_group, 5, wts["w_pa"], wts["w_pb"],
               wts["w_out"])
    y = _ffn(x, mod3, rows_per_group, 6, wts["norm_ff2"], wts["ff2_w1"], wts["ff2_w3"],
             wts["ff2_w2"], final_norm=final_norm)
    last = proj.reshape(bsz, T, P_WIDTH)[:, T - (CONV_W - 1):]
    mconv1 = last[:, :, P_QK:P_QK + 2 * M_QK_W]
    sconv1 = last[:, :, P_XBC:P_XBC + S_CONV_CH]
    return y.reshape(bsz, T, D_MODEL), (c1, n1, m1, mconv1, h1, sconv1)


def _pack_w_tail(w):
    o_z = P_HEAD + 2 * M_HEADS
    o_dt = o_z + S_INNER + S_CONV_CH
    o_ga = o_dt + S_HEADS
    pad = P_WIDTH - (P_SMALL + 2 * M_HEADS + S_HEADS)
    return jnp.concatenate(
        [w[:, o_z:o_dt], w[:, o_ga:], w[:, P_HEAD:o_z], w[:, o_dt:o_ga],
         jnp.zeros((w.shape[0], pad), w.dtype)], axis=1).astype(BF16)


def kernel(x_prompt, x_sample, c_prompt, c_sample, state_mlstm_C, state_mlstm_n, state_mlstm_m, state_mlstm_conv, state_ssm, state_ssm_conv, ada_w, ada_b, norm_ff1, ff1_w1, ff1_w3, ff1_w2, norm_mix, w_in, m_conv_w, m_conv_b, m_if_b, m_head_g, w_pa, s_conv_w, s_conv_b, s_dt_bias, s_A_log, s_D, s_norm_g, w_pb, w_out, norm_ff2, ff2_w1, ff2_w3, ff2_w2, final_norm):
    bp, tp, _ = x_prompt.shape
    bs, ts, _ = x_sample.shape
    depth = ada_w.shape[0]
    assert depth == 1

    hp, hs = x_prompt, x_sample
    new_p, new_s = [], []
    for l in range(depth):
        c_all = jnp.concatenate([c_prompt, c_sample], axis=0)
        pad_rows = (-c_all.shape[0]) % (2 * SUBLANES)
        c_all = jnp.pad(c_all, ((0, pad_rows), (0, 0)))
        mod = _ada(c_all, ada_w[l], ada_b[l])
        mod_p = mod[:bp].reshape(bp, 1, N_MOD * D_MODEL)
        mod_s = jnp.repeat(mod[bp:bp + bs], ts, axis=0).reshape(1, bs * ts, N_MOD * D_MODEL)

        lane_pad = LANES - (2 * M_HEADS + S_HEADS)
        small_bias = jnp.concatenate(
            [m_if_b[l], s_dt_bias[l], jnp.zeros((lane_pad,), F32)]).reshape(1, LANES)
        alog_row = jnp.concatenate(
            [jnp.zeros((2 * M_HEADS,), F32), s_A_log[l], jnp.zeros((lane_pad,), F32)]
        ).reshape(1, LANES)
        wts = {
            "norm_ff1": norm_ff1[l], "ff1_w1": ff1_w1[l], "ff1_w3": ff1_w3[l],
            "ff1_w2": ff1_w2[l],
            "norm_mix": norm_mix[l], "w_in": w_in[l], "w_tail": _pack_w_tail(w_in[l]),
            "m_conv_w": m_conv_w[l], "m_conv_b": m_conv_b[l], "small_bias": small_bias,
            "m_head_g": m_head_g[l], "w_pa": w_pa[l],
            "s_conv_w": s_conv_w[l], "s_conv_b": s_conv_b[l], "alog_row": alog_row,
            "d_row": jnp.repeat(s_D[l], S_HEADDIM).reshape(1, S_INNER),
            "s_norm_g": s_norm_g[l], "w_pb": w_pb[l], "w_out": w_out[l],
            "norm_ff2": norm_ff2[l], "ff2_w1": ff2_w1[l], "ff2_w3": ff2_w3[l],
            "ff2_w2": ff2_w2[l],
        }
        st_s = (state_mlstm_C[l], state_mlstm_n[l], state_mlstm_m[l], state_mlstm_conv[l],
                state_ssm[l], state_ssm_conv[l])
        hp, sp = _layer(hp, mod_p, tp, bp, tp, wts, final_norm, None)
        hs, ss = _layer(hs, mod_s, bs * ts, bs, ts, wts, final_norm, st_s)
        new_p.append(sp)
        new_s.append(ss)
    outs_p = [jnp.stack([s[i] for s in new_p]) for i in range(6)]
    outs_s = [jnp.stack([s[i] for s in new_s]) for i in range(6)]
    return (hp, hs, *outs_p, *outs_s)
```

```python
import functools

import jax
import jax.numpy as jnp
from jax import lax
from jax.experimental import pallas as pl
from jax.experimental.pallas import tpu as pltpu

F32 = jnp.float32
BF16 = jnp.bfloat16

D_MODEL = 1024
M_HEADS = 4
M_QK = 256
M_V = 512
M_QK_W = M_HEADS * M_QK
M_V_W = M_HEADS * M_V
S_INNER = 2048
S_HEADDIM = 64
S_HEADS = 32
S_GROUPS = 4
S_HPG = S_HEADS // S_GROUPS
S_STATE = 128
S_GW = S_INNER // S_GROUPS
S_BCW = S_GROUPS * S_STATE
S_CONV_CH = S_INNER + 2 * S_BCW
CONV_W = 4
D_FF = 2816
CHUNK = 128
EPS = 1e-6
N_MOD = 9
K_SCALE = M_QK ** -0.5

LANES = 128
SUBLANES = 8
VMEM_LIMIT = 56 * 1024 * 1024

P_QK = 0
P_V = 2 * M_QK_W
P_O = P_V + M_V_W
P_HEAD = P_O + M_V_W
P_Z = P_HEAD
P_XBC = P_Z + S_INNER
P_B = P_XBC + S_INNER
P_C = P_B + S_BCW
P_GA = P_XBC + S_CONV_CH
P_GB = P_GA + D_MODEL
P_SMALL = P_GB + D_MODEL
PROJ_TN = 1536
PACK_TN = 512
P_WIDTH = 13824
SM_I = 0
SM_F = M_HEADS
SM_DT = 2 * M_HEADS

FFN_TM = 2048
FFN_TF = 256
PROJ_TM = 1024
MERGE_TM = 256
SHORT_SEQS_PER_STEP = 4
CONV_SLAB = 256


def _sigmoid(x):
    return 0.5 * jnp.tanh(0.5 * x) + 0.5


def _silu(x):
    h = 0.5 * x
    return h * jnp.tanh(h) + h


def _rms(x, g):
    return x * lax.rsqrt(jnp.mean(x * x, axis=-1, keepdims=True) + EPS) * g


def _ada_kernel(c_ref, w_ref, b_ref, o_ref):
    s = _silu(c_ref[...]).astype(BF16)
    o_ref[...] = jnp.dot(s, w_ref[...].astype(BF16), preferred_element_type=F32) + b_ref[...]


def _ada(c_all, ada_w, ada_b):
    m = c_all.shape[0]
    n = ada_w.shape[1]
    tn = D_MODEL
    return pl.pallas_call(
        _ada_kernel,
        grid=(n // tn,),
        in_specs=[
            pl.BlockSpec((m, D_MODEL), lambda j: (0, 0)),
            pl.BlockSpec((D_MODEL, tn), lambda j: (0, j)),
            pl.BlockSpec((1, tn), lambda j: (0, j)),
        ],
        out_specs=pl.BlockSpec((m, tn), lambda j: (0, j)),
        out_shape=jax.ShapeDtypeStruct((m, n), F32),
        compiler_params=pltpu.CompilerParams(
            dimension_semantics=("arbitrary",), vmem_limit_bytes=VMEM_LIMIT),
        name="ada_mod",
    )(c_all, ada_w, ada_b.reshape(1, n))


def _ffn_kernel(x_ref, sh_ref, sc_ref, g_ref, nw_ref, w1_ref, w3_ref, w2_ref, *rest,
                nf, final):
    if final:
        fn_ref, o_ref, u_ref = rest
    else:
        o_ref, u_ref = rest
    f = pl.program_id(1)

    @pl.when(f == 0)
    def _():
        xn = _rms(x_ref[...], nw_ref[...])
        u_ref[...] = (xn * (1.0 + sc_ref[0]) + sh_ref[0]).astype(BF16)
        o_ref[...] = jnp.zeros_like(o_ref)

    u = u_ref[...]
    h1 = jnp.dot(u, w1_ref[...].astype(BF16), preferred_element_type=F32)
    h3 = jnp.dot(u, w3_ref[...].astype(BF16), preferred_element_type=F32)
    a = (_silu(h1) * h3).astype(BF16)
    o_ref[...] += jnp.dot(a, w2_ref[...].astype(BF16), preferred_element_type=F32)

    @pl.when(f == nf - 1)
    def _():
        y = x_ref[...] + (0.5 * g_ref[0]) * o_ref[...]
        if final:
            y = _rms(y, fn_ref[...])
        o_ref[...] = y


def _mod_spec(mod3, rows_per_group, tm, k):
    if mod3.shape[1] == 1:
        tiles_per_group = rows_per_group // tm
        return pl.BlockSpec((1, 1, D_MODEL), lambda i, *_: (i // tiles_per_group, 0, k))
    return pl.BlockSpec((1, tm, D_MODEL), lambda i, *_: (0, i, k))


def _ffn(x, mod3, rows_per_group, kmod, norm_w, w1, w3, w2, final_norm=None):
    n = x.shape[0]
    tm = min(FFN_TM, rows_per_group)
    tf = FFN_TF
    nf = D_FF // tf
    final = final_norm is not None
    in_specs = [
        pl.BlockSpec((tm, D_MODEL), lambda i, f: (i, 0)),
        _mod_spec(mod3, rows_per_group, tm, kmod),
        _mod_spec(mod3, rows_per_group, tm, kmod + 1),
        _mod_spec(mod3, rows_per_group, tm, kmod + 2),
        pl.BlockSpec((1, D_MODEL), lambda i, f: (0, 0)),
        pl.BlockSpec((D_MODEL, tf), lambda i, f: (0, f)),
        pl.BlockSpec((D_MODEL, tf), lambda i, f: (0, f)),
        pl.BlockSpec((tf, D_MODEL), lambda i, f: (f, 0)),
    ]
    args = [x, mod3, mod3, mod3, norm_w.reshape(1, D_MODEL), w1, w3, w2]
    if final:
        in_specs.append(pl.BlockSpec((1, D_MODEL), lambda i, f: (0, 0)))
        args.append(final_norm.reshape(1, D_MODEL))
    return pl.pallas_call(
        functools.partial(_ffn_kernel, nf=nf, final=final),
        grid=(n // tm, nf),
        in_specs=in_specs,
        out_specs=pl.BlockSpec((tm, D_MODEL), lambda i, f: (i, 0)),
        out_shape=jax.ShapeDtypeStruct((n, D_MODEL), F32),
        scratch_shapes=[pltpu.VMEM((tm, D_MODEL), BF16)],
        compiler_params=pltpu.CompilerParams(
            dimension_semantics=("parallel", "arbitrary"), vmem_limit_bytes=VMEM_LIMIT),
        name="ffn_final" if final else "ffn",
    )(*args)


PACK_SHIFT_Z = 2 * M_HEADS
PACK_SHIFT_GA = PACK_SHIFT_Z + S_HEADS
SRC_IF = P_HEAD
SRC_DT = P_HEAD + PACK_SHIFT_Z + S_INNER + S_CONV_CH


def _pack_kernel(a_ref, b_ref, gif_ref, gdt_ref, o_ref):
    j = pl.program_id(0)

    def shifted(by):
        wide = jnp.concatenate([a_ref[...], b_ref[...]], axis=1)
        return pltpu.roll(wide, wide.shape[1] - by, axis=1)[:, :PACK_TN].astype(BF16)

    @pl.when(j < P_HEAD // PACK_TN)
    def _():
        o_ref[...] = a_ref[...].astype(BF16)

    @pl.when(jnp.logical_and(j >= P_HEAD // PACK_TN, j < P_GA // PACK_TN))
    def _():
        o_ref[...] = shifted(PACK_SHIFT_Z)

    @pl.when(jnp.logical_and(j >= P_GA // PACK_TN, j < P_SMALL // PACK_TN))
    def _():
        o_ref[...] = shifted(PACK_SHIFT_GA)

    @pl.when(j == P_SMALL // PACK_TN)
    def _():
        lane = lax.broadcasted_iota(jnp.int32, gif_ref.shape, 1)
        small = jnp.where(lane < SM_DT, gif_ref[...],
                          jnp.where(lane < SM_DT + S_HEADS, gdt_ref[...], 0.0))
        pad = jnp.zeros((small.shape[0], PACK_TN - LANES), F32)
        o_ref[...] = jnp.concatenate([small, pad], axis=1).astype(BF16)


def _pack_w_in(w):
    d = w.shape[0]
    assert SRC_IF % LANES == SM_I and SRC_DT % LANES == SM_DT
    last_b = (P_SMALL - PACK_TN) // LANES + PACK_TN // LANES
    return pl.pallas_call(
        _pack_kernel,
        grid=(P_WIDTH // PACK_TN,),
        in_specs=[
            pl.BlockSpec((d, PACK_TN), lambda j: (0, j)),
            pl.BlockSpec((d, LANES),
                         lambda j: (0, jnp.minimum((j + 1) * (PACK_TN // LANES), last_b))),
            pl.BlockSpec((d, LANES), lambda j: (0, SRC_IF // LANES)),
            pl.BlockSpec((d, LANES), lambda j: (0, SRC_DT // LANES)),
        ],
        out_specs=pl.BlockSpec((d, PACK_TN), lambda j: (0, j)),
        out_shape=jax.ShapeDtypeStruct((d, P_WIDTH), BF16),
        compiler_params=pltpu.CompilerParams(
            dimension_semantics=("arbitrary",), vmem_limit_bytes=VMEM_LIMIT),
        name="pack_w_in",
    )(w, w, w, w)


def _proj_kernel(x_ref, sh_ref, sc_ref, nw_ref, w_ref, o_ref, u_ref):
    @pl.when(pl.program_id(1) == 0)
    def _():
        xn = _rms(x_ref[...], nw_ref[...])
        u_ref[...] = (xn * (1.0 + sc_ref[0]) + sh_ref[0]).astype(BF16)

    o_ref[...] = jnp.dot(u_ref[...], w_ref[...], preferred_element_type=F32)


def _proj(x, mod3, rows_per_group, kmod, norm_w, w_packed):
    n = x.shape[0]
    tm = min(PROJ_TM, rows_per_group)
    tn = PROJ_TN
    return pl.pallas_call(
        _proj_kernel,
        grid=(n // tm, P_WIDTH // tn),
        in_specs=[
            pl.BlockSpec((tm, D_MODEL), lambda i, j: (i, 0)),
            _mod_spec(mod3, rows_per_group, tm, kmod),
            _mod_spec(mod3, rows_per_group, tm, kmod + 1),
            pl.BlockSpec((1, D_MODEL), lambda i, j: (0, 0)),
            pl.BlockSpec((D_MODEL, tn), lambda i, j: (0, j)),
        ],
        out_specs=pl.BlockSpec((tm, tn), lambda i, j: (i, j)),
        out_shape=jax.ShapeDtypeStruct((n, P_WIDTH), F32),
        scratch_shapes=[pltpu.VMEM((tm, D_MODEL), BF16)],
        compiler_params=pltpu.CompilerParams(
            dimension_semantics=("parallel", "arbitrary"), vmem_limit_bytes=VMEM_LIMIT),
        name="mix_proj",
    )(x, mod3, mod3, norm_w.reshape(1, D_MODEL), w_packed)


def _conv_silu(xp_ref, x_ref, w_ref, b_ref, L, dst_ref, slabs=None):
    if slabs is None:
        slabs = range(x_ref.shape[1] // CONV_SLAB)
    for s in slabs:
        cols = slice(s * CONV_SLAB, (s + 1) * CONV_SLAB)
        xp_ref[pl.ds(SUBLANES, L), cols] = x_ref[:, cols]
        xp = xp_ref[:, cols]
        acc = b_ref[:, cols] + xp[SUBLANES:] * w_ref[CONV_W - 1:CONV_W, cols]
        for back in range(1, CONV_W):
            shifted = pltpu.roll(xp, back, axis=0)[SUBLANES:]
            acc = acc + shifted * w_ref[CONV_W - 1 - back:CONV_W - back, cols]
        dst_ref[:, cols] = _silu(acc)
        xp_ref[pl.ds(0, SUBLANES), cols] = xp_ref[pl.ds(L, SUBLANES), cols]


def _init_conv(xp_ref, init_ref):
    xp_ref[pl.ds(0, SUBLANES), :] = jnp.zeros((SUBLANES, xp_ref.shape[1]), F32)
    if init_ref is not None:
        xp_ref[pl.ds(SUBLANES - (CONV_W - 1), CONV_W - 1), :] = init_ref[0]


def _run_stages(streams):
    live = list(streams)
    while live:
        for stream in list(live):
            try:
                next(stream)
            except StopIteration:
                live.remove(stream)


def _split3(a):
    hi = a.astype(BF16)
    r1 = a - hi.astype(F32)
    mid = r1.astype(BF16)
    lo = (r1 - mid.astype(F32)).astype(BF16)
    return hi, mid, lo


def _nt_dot(a, b):
    return lax.dot_general(a, b, (((1,), (1,)), ((), ())), preferred_element_type=F32)


def _tn_dot(a, b):
    return lax.dot_general(a, b, (((0,), (0,)), ((), ())), preferred_element_type=F32)


def _dot3(mask_b, a):
    return sum(jnp.dot(mask_b, p, preferred_element_type=F32) for p in _split3(a))


def _tn3(a, mask_b):
    return sum(_tn_dot(p, mask_b) for p in _split3(a))


def _log_sigmoid(x):
    return jnp.minimum(x, 0.0) - jnp.log1p(jnp.exp(-jnp.abs(x)))


def _softplus(x):
    return jnp.maximum(x, 0.0) + jnp.log1p(jnp.exp(-jnp.abs(x)))


def _mlstm_kernel(*refs, L, nseq, has_state):
    qk_ref, v_ref, og_ref, sm_ref, sb_ref, hg_ref, w_ref, b_ref = refs[:8]
    refs = refs[8:]
    if has_state:
        cv0_ref, c0_ref, n0_ref, m0_ref = refs[:4]
        refs = refs[4:]
    o_ref, c_ref, n_ref, m_ref, xp_ref, qk_s = refs

    @pl.when(pl.program_id(1) == 0)
    def _():
        if has_state:
            c_ref[...] = c0_ref[...]
            n_ref[...] = n0_ref[...]
            m_ref[...] = m0_ref[...]
        else:
            c_ref[...] = jnp.zeros_like(c_ref)
            n_ref[...] = jnp.zeros_like(n_ref)
            m_ref[...] = jnp.zeros_like(m_ref)
        for s in range(nseq):
            _init_conv(xp_ref.at[s], cv0_ref.at[pl.ds(s, 1)] if has_state else None)

    rr = lax.broadcasted_iota(jnp.int32, (L, L), 0)
    cc = lax.broadcasted_iota(jnp.int32, (L, L), 1)
    tril = cc <= rr
    gates = []
    for s in range(nseq):
        sm = sm_ref[pl.ds(s * L, L), :] + sb_ref[...]
        lf_all = _log_sigmoid(sm)
        b_all = _dot3(tril.astype(BF16), lf_all)
        b_all_t = _tn3(lf_all, (rr <= cc).astype(BF16))
        sm_t = _tn3(sm, (rr == cc).astype(BF16))
        gates.append((sm, sm_t, b_all, b_all_t))

    def head_stages(s, h):
        rows = pl.ds(s * L, L)
        sm, sm_t, b_all, b_all_t = gates[s]
        qk_seq = qk_s.at[s]
        _conv_silu(xp_ref.at[s], qk_ref.at[rows], w_ref, b_ref, L, qk_seq,
                   slabs=[h * M_QK // CONV_SLAB, (M_QK_W + h * M_QK) // CONV_SLAB])
        yield
        q = qk_seq[:, h * M_QK:(h + 1) * M_QK]
        k = qk_seq[:, M_QK_W + h * M_QK:M_QK_W + (h + 1) * M_QK] * K_SCALE
        v = v_ref[rows, h * M_V:(h + 1) * M_V]
        i_col = sm[:, SM_I + h:SM_I + h + 1]
        i_row = sm_t[SM_I + h:SM_I + h + 1, :]
        b_col = b_all[:, SM_F + h:SM_F + h + 1]
        b_row = b_all_t[SM_F + h:SM_F + h + 1, :]

        c_prev = c_ref[s, h]
        n_prev = n_ref[s, h]
        m_prev = m_ref[s, h]

        dmat = jnp.where(tril, b_col - b_row + i_row, -jnp.inf)
        inter = b_col + m_prev
        mt = jnp.maximum(inter, jnp.max(dmat, axis=1, keepdims=True))
        w_tok = jnp.exp(dmat - mt)
        w_st = jnp.exp(inter - mt)
        yield

        qb = q.astype(BF16)
        kb = k.astype(BF16)
        vb = v.astype(BF16)
        sc = _nt_dot(qb, kb) * w_tok
        yield
        num = (jnp.dot(sc.astype(BF16), vb, preferred_element_type=F32)
               + w_st * jnp.dot(qb, c_prev.astype(BF16), preferred_element_type=F32))
        den = (jnp.sum(sc, axis=1, keepdims=True)
               + w_st * jnp.sum(q * n_prev, axis=1, keepdims=True))
        yield
        hh = num / jnp.maximum(jnp.abs(den), jnp.exp(-mt))

        m_new = mt[L - 1:L, :]
        b_last = b_col[L - 1:L, :]
        ws = jnp.exp(b_last - b_col + i_col - m_new)
        decay = jnp.exp(b_last + m_prev - m_new)
        kw = k * ws
        c_ref[s, h] = decay * c_prev + _tn_dot(kw.astype(BF16), vb)
        n_ref[s, h] = decay * n_prev + jnp.sum(kw, axis=0, keepdims=True)
        m_ref[s, h] = m_new
        yield

        mu = jnp.mean(hh, axis=-1, keepdims=True)
        hc = hh - mu
        yield
        var = jnp.mean(hc * hc, axis=-1, keepdims=True)
        hsl = slice(h * M_V, (h + 1) * M_V)
        o_ref[rows, hsl] = (hc * lax.rsqrt(var + EPS) * hg_ref[:, hsl]
                            * _sigmoid(og_ref[rows, hsl]))

    _run_stages([head_stages(s, h) for s in range(nseq) for h in range(M_HEADS)])


def _seqs_per_step(bsz, nc):
    return SHORT_SEQS_PER_STEP if nc == 1 and bsz % SHORT_SEQS_PER_STEP == 0 else 1


def _mlstm(proj, bsz, T, m_conv_w, m_conv_b, small_bias, m_head_g, state=None):
    L = min(T, CHUNK)
    nc = T // L
    nseq = _seqs_per_step(bsz, nc)
    rows = nseq * L
    has_state = state is not None
    qkw = 2 * M_QK_W

    def row(b, c):
        return b * nc + c

    const = lambda b, c: (0, 0)
    per_b = lambda b, c: (b, 0, 0, 0)
    in_specs = [
        pl.BlockSpec((rows, qkw), lambda b, c: (row(b, c), P_QK // qkw)),
        pl.BlockSpec((rows, M_V_W), lambda b, c: (row(b, c), P_V // M_V_W)),
        pl.BlockSpec((rows, M_V_W), lambda b, c: (row(b, c), P_O // M_V_W)),
        pl.BlockSpec((rows, LANES), lambda b, c: (row(b, c), P_SMALL // LANES)),
        pl.BlockSpec((1, LANES), const),
        pl.BlockSpec((1, M_V_W), const),
        pl.BlockSpec((CONV_W, qkw), const),
        pl.BlockSpec((1, qkw), const),
    ]
    args = [proj, proj, proj, proj, small_bias, m_head_g.reshape(1, -1),
            m_conv_w, m_conv_b.reshape(1, -1)]
    scratch = [pltpu.VMEM((nseq, L + SUBLANES, qkw), F32), pltpu.VMEM((nseq, L, qkw), F32)]
    if has_state:
        conv0, c0, n0, m0 = state
        in_specs += [
            pl.BlockSpec((nseq, CONV_W - 1, qkw), lambda b, c: (b, 0, 0)),
            pl.BlockSpec((nseq, M_HEADS, M_QK, M_V), per_b),
            pl.BlockSpec((nseq, M_HEADS, 1, M_QK), per_b),
            pl.BlockSpec((nseq, M_HEADS, 1, 1), per_b),
        ]
        args += [conv0, c0, n0.reshape(bsz, M_HEADS, 1, M_QK), m0.reshape(bsz, M_HEADS, 1, 1)]
    out_shape = (
        jax.ShapeDtypeStruct((bsz * T, M_V_W), F32),
        jax.ShapeDtypeStruct((bsz, M_HEADS, M_QK, M_V), F32),
        jax.ShapeDtypeStruct((bsz, M_HEADS, 1, M_QK), F32),
        jax.ShapeDtypeStruct((bsz, M_HEADS, 1, 1), F32),
    )
    out_specs = (
        pl.BlockSpec((rows, M_V_W), lambda b, c: (row(b, c), 0)),
        pl.BlockSpec((nseq, M_HEADS, M_QK, M_V), per_b),
        pl.BlockSpec((nseq, M_HEADS, 1, M_QK), per_b),
        pl.BlockSpec((nseq, M_HEADS, 1, 1), per_b),
    )
    out, c1, n1, m1 = pl.pallas_call(
        functools.partial(_mlstm_kernel, L=L, nseq=nseq, has_state=has_state),
        grid=(bsz // nseq, nc),
        in_specs=in_specs,
        out_specs=out_specs,
        out_shape=out_shape,
        scratch_shapes=scratch,
        compiler_params=pltpu.CompilerParams(
            dimension_semantics=("arbitrary", "arbitrary"), vmem_limit_bytes=VMEM_LIMIT),
        name="mlstm_state" if has_state else "mlstm",
    )(*args)
    return out, c1, n1.reshape(bsz, M_HEADS, M_QK), m1.reshape(bsz, M_HEADS)


def _ssd_kernel(*refs, L, nseq, has_state):
    n_shared = 13
    streams = []
    for sq in range(nseq):
        rows, one = pl.ds(sq * L, L), pl.ds(sq, 1)
        views = [r.at[rows] for r in refs[:5]] + list(refs[5:n_shared])
        rest = refs[n_shared:]
        if has_state:
            views += [r.at[one] for r in rest[:2]]
            rest = rest[2:]
        o_ref, h_ref = rest[:2]
        views += [o_ref.at[rows], h_ref.at[one]] + [r.at[sq] for r in rest[2:]]
        streams += _ssd_seq_streams(*views, L=L, has_state=has_state)
    _run_stages(streams)


def _ssd_seq_streams(*refs, L, has_state):
    (z_ref, xs_ref, bp_ref, cp_ref, sm_ref, sb_ref, al_ref, d_ref, ng_ref, e_ref, et_ref,
     w_ref, b_ref) = refs[:13]
    refs = refs[13:]
    if has_state:
        cv0_ref, h0_ref = refs[:2]
        refs = refs[2:]
    o_ref, h_ref, xx_ref, xb_ref, xc_ref, xs_s, bm_s, cm_s = refs

    @pl.when(pl.program_id(1) == 0)
    def _():
        if has_state:
            h_ref[...] = h0_ref[...]
        else:
            h_ref[...] = jnp.zeros_like(h_ref)

    x_cols = slice(0, S_INNER)
    b_cols = slice(S_INNER, S_INNER + S_BCW)
    c_cols = slice(S_INNER + S_BCW, S_CONV_CH)
    for xp_ref, cols in ((xx_ref, x_cols), (xb_ref, b_cols), (xc_ref, c_cols)):
        @pl.when(pl.program_id(1) == 0)
        def _():
            _init_conv(xp_ref, cv0_ref.at[:, :, cols] if has_state else None)

    _conv_silu(xb_ref, bp_ref, w_ref.at[:, b_cols], b_ref.at[:, b_cols], L, bm_s)
    _conv_silu(xc_ref, cp_ref, w_ref.at[:, c_cols], b_ref.at[:, c_cols], L, cm_s)

    dt_all = _softplus(sm_ref[...] + sb_ref[...])
    da_all = dt_all * (-jnp.exp(al_ref[...]))
    lane = lax.broadcasted_iota(jnp.int32, (L, LANES), 1)
    rr = lax.broadcasted_iota(jnp.int32, (L, L), 0)
    cc = lax.broadcasted_iota(jnp.int32, (L, L), 1)
    tril = cc <= rr
    lo = lane < S_HEADDIM
    cum_all = _dot3(tril.astype(BF16), da_all)
    cum_all_t = _tn3(da_all, (rr <= cc).astype(BF16))
    cum_last = cum_all[L - 1:L, :]
    dec_all = dt_all * jnp.exp(cum_last - cum_all)
    ecum_all = jnp.exp(cum_all)
    mxu_spread = L >= LANES
    if mxu_spread:
        dt_src = _split3(dt_all)[:2]
        dec_src = _split3(dec_all)[:2]
        ecum_src = _split3(ecum_all)[:2]
        clast_src = _split3(jnp.broadcast_to(cum_all_t[:, L - 1:L], (LANES, LANES)))
    else:
        dt_src, dec_src, ecum_src = dt_all, dec_all, ecum_all
        lo_row = lax.broadcasted_iota(jnp.int32, (LANES, 1), 0) < S_HEADDIM

    def spread(src, pair):
        if mxu_spread:
            return sum(jnp.dot(p, e_ref[pair], preferred_element_type=F32) for p in src)
        idx = SM_DT + 2 * pair
        return jnp.where(lo, src[:, idx:idx + 1], src[:, idx + 1:idx + 2])

    def state_decay(pair):
        if mxu_spread:
            return jnp.exp(sum(jnp.dot(et_ref[pair], p, preferred_element_type=F32)
                               for p in clast_src))
        idx = SM_DT + 2 * pair
        return jnp.exp(jnp.where(lo_row, cum_last[:, idx:idx + 1], cum_last[:, idx + 1:idx + 2]))

    def group_stages(g):
        _conv_silu(xx_ref, xs_ref, w_ref.at[:, x_cols], b_ref.at[:, x_cols], L, xs_s,
                   slabs=range(g * S_GW // CONV_SLAB, (g + 1) * S_GW // CONV_SLAB))
        yield
        bmb = bm_s[:, g * S_STATE:(g + 1) * S_STATE].astype(BF16)
        cmb = cm_s[:, g * S_STATE:(g + 1) * S_STATE].astype(BF16)
        cb = _nt_dot(cmb, bmb)
        ssq = jnp.zeros((L, 1), F32)
        for p in range(S_HPG // 2):
            yield
            pair = g * (S_HPG // 2) + p
            psl = slice(g * S_GW + p * LANES, g * S_GW + (p + 1) * LANES)
            hsl = slice(p * LANES, (p + 1) * LANES)
            lmat = []
            for e in range(2):
                idx = SM_DT + 2 * pair + e
                lmat.append(jnp.exp(jnp.where(
                    tril, cum_all[:, idx:idx + 1] - cum_all_t[idx:idx + 1, :], -jnp.inf)))
            xs = xs_s[:, psl]
            xdt = xs * spread(dt_src, pair)
            xdd = xs * spread(dec_src, pair)
            h_prev = h_ref[0, g, hsl, :]
            yield
            y = (jnp.dot((cb * lmat[0]).astype(BF16), jnp.where(lo, xdt, 0.0).astype(BF16),
                         preferred_element_type=F32)
                 + jnp.dot((cb * lmat[1]).astype(BF16), jnp.where(lo, 0.0, xdt).astype(BF16),
                           preferred_element_type=F32)
                 + spread(ecum_src, pair) * _nt_dot(cmb, h_prev.astype(BF16)))
            h_ref[0, g, hsl, :] = state_decay(pair) * h_prev + _tn_dot(xdd.astype(BF16), bmb)
            yield
            gated = (y + d_ref[:, psl] * xs) * _silu(z_ref[:, psl])
            ssq = ssq + jnp.sum(gated * gated, axis=1, keepdims=True)
            o_ref[:, psl] = gated
        yield
        gsl = slice(g * S_GW, (g + 1) * S_GW)
        o_ref[:, gsl] = o_ref[:, gsl] * lax.rsqrt(ssq * (1.0 / S_GW) + EPS) * ng_ref[:, gsl]

    return [group_stages(g) for g in range(S_GROUPS)]


def _pair_onehots():
    n_pairs = S_HEADS // 2
    pair = jnp.arange(n_pairs)[:, None, None]
    src = jnp.arange(LANES)[None, :, None]
    dst = jnp.arange(LANES)[None, None, :]
    onehot = src == SM_DT + 2 * pair + dst // S_HEADDIM
    return onehot.astype(BF16), jnp.swapaxes(onehot, 1, 2).astype(BF16)


def _ssd(proj, bsz, T, s_conv_w, s_conv_b, small_bias, alog_row, d_row, s_norm_g, state=None):
    L = min(T, CHUNK)
    nc = T // L
    nseq = _seqs_per_step(bsz, nc)
    rows = nseq * L
    has_state = state is not None
    n_pairs = S_HEADS // 2
    onehot, onehot_t = _pair_onehots()

    def row(b, c):
        return b * nc + c

    const = lambda b, c: (0, 0)
    const3 = lambda b, c: (0, 0, 0)
    in_specs = [
        pl.BlockSpec((rows, S_INNER), lambda b, c: (row(b, c), P_Z // S_INNER)),
        pl.BlockSpec((rows, S_INNER), lambda b, c: (row(b, c), P_XBC // S_INNER)),
        pl.BlockSpec((rows, S_BCW), lambda b, c: (row(b, c), P_B // S_BCW)),
        pl.BlockSpec((rows, S_BCW), lambda b, c: (row(b, c), P_C // S_BCW)),
        pl.BlockSpec((rows, LANES), lambda b, c: (row(b, c), P_SMALL // LANES)),
        pl.BlockSpec((1, LANES), const),
        pl.BlockSpec((1, LANES), const),
        pl.BlockSpec((1, S_INNER), const),
        pl.BlockSpec((1, S_INNER), const),
        pl.BlockSpec((n_pairs, LANES, LANES), const3),
        pl.BlockSpec((n_pairs, LANES, LANES), const3),
        pl.BlockSpec((CONV_W, S_CONV_CH), const),
        pl.BlockSpec((1, S_CONV_CH), const),
    ]
    args = [proj, proj, proj, proj, proj, small_bias, alog_row, d_row, s_norm_g.reshape(1, -1),
            onehot, onehot_t, s_conv_w, s_conv_b.reshape(1, -1)]
    scratch = [pltpu.VMEM((nseq, L + SUBLANES, S_INNER), F32),
               pltpu.VMEM((nseq, L + SUBLANES, S_BCW), F32),
               pltpu.VMEM((nseq, L + SUBLANES, S_BCW), F32),
               pltpu.VMEM((nseq, L, S_INNER), F32),
               pltpu.VMEM((nseq, L, S_BCW), F32),
               pltpu.VMEM((nseq, L, S_BCW), F32)]
    if has_state:
        conv0, h0 = state
        in_specs += [
            pl.BlockSpec((nseq, CONV_W - 1, S_CONV_CH), lambda b, c: (b, 0, 0)),
            pl.BlockSpec((nseq, S_GROUPS, S_GW, S_STATE), lambda b, c: (b, 0, 0, 0)),
        ]
        args += [conv0, h0.reshape(bsz, S_GROUPS, S_GW, S_STATE)]
    out, h1 = pl.pallas_call(
        functools.partial(_ssd_kernel, L=L, nseq=nseq, has_state=has_state),
        grid=(bsz // nseq, nc),
        in_specs=in_specs,
        out_specs=(
            pl.BlockSpec((rows, S_INNER), lambda b, c: (row(b, c), 0)),
            pl.BlockSpec((nseq, S_GROUPS, S_GW, S_STATE), lambda b, c: (b, 0, 0, 0)),
        ),
        out_shape=(
            jax.ShapeDtypeStruct((bsz * T, S_INNER), F32),
            jax.ShapeDtypeStruct((bsz, S_GROUPS, S_GW, S_STATE), F32),
        ),
        scratch_shapes=scratch,
        compiler_params=pltpu.CompilerParams(
            dimension_semantics=("arbitrary", "arbitrary"), vmem_limit_bytes=VMEM_LIMIT),
        name="ssd_state" if has_state else "ssd",
    )(*args)
    return out, h1.reshape(bsz, S_HEADS, S_HEADDIM, S_STATE)


def _merge_kernel(x_ref, hm_ref, gt_ref, ga_ref, gb_ref, g_ref, wpa_ref, wpb_ref, wo_ref, o_ref,
                  wpa_s, wpb_s, wo_s):
    @pl.when(pl.program_id(0) == 0)
    def _():
        wpa_s[...] = wpa_ref[...].astype(BF16)
        wpb_s[...] = wpb_ref[...].astype(BF16)
        wo_s[...] = wo_ref[...].astype(BF16)

    ya = jnp.dot(hm_ref[...].astype(BF16), wpa_s[...], preferred_element_type=F32)
    yb = jnp.dot(gt_ref[...].astype(BF16), wpb_s[...], preferred_element_type=F32)
    merged = _sigmoid(ga_ref[...]) * ya + _sigmoid(gb_ref[...]) * yb
    o_ref[...] = x_ref[...] + g_ref[0] * jnp.dot(
        merged.astype(BF16), wo_s[...], preferred_element_type=F32)


def _merge(x, hm, gated, proj, mod3, rows_per_group, kmod, w_pa, w_pb, w_out):
    n = x.shape[0]
    tm = MERGE_TM
    const = lambda i: (0, 0)
    resident = functools.partial(pl.BlockSpec, index_map=const, pipeline_mode=pl.Buffered(1))
    return pl.pallas_call(
        _merge_kernel,
        grid=(n // tm,),
        in_specs=[
            pl.BlockSpec((tm, D_MODEL), lambda i: (i, 0)),
            pl.BlockSpec((tm, M_V_W), lambda i: (i, 0)),
            pl.BlockSpec((tm, S_INNER), lambda i: (i, 0)),
            pl.BlockSpec((tm, D_MODEL), lambda i: (i, P_GA // D_MODEL)),
            pl.BlockSpec((tm, D_MODEL), lambda i: (i, P_GB // D_MODEL)),
            _mod_spec(mod3, rows_per_group, tm, kmod),
            resident((M_V_W, D_MODEL)),
            resident((S_INNER, D_MODEL)),
            resident((D_MODEL, D_MODEL)),
        ],
        out_specs=pl.BlockSpec((tm, D_MODEL), lambda i: (i, 0)),
        out_shape=jax.ShapeDtypeStruct((n, D_MODEL), F32),
        scratch_shapes=[pltpu.VMEM((M_V_W, D_MODEL), BF16),
                        pltpu.VMEM((S_INNER, D_MODEL), BF16),
                        pltpu.VMEM((D_MODEL, D_MODEL), BF16)],
        compiler_params=pltpu.CompilerParams(
            dimension_semantics=("arbitrary",), vmem_limit_bytes=VMEM_LIMIT),
        name="merge",
    )(x, hm, gated, proj, proj, mod3, w_pa, w_pb, w_out)


def _layer(x, mod3, rows_per_group, bsz, T, wts, final_norm, state):
    n = bsz * T
    x = x.reshape(n, D_MODEL)
    x = _ffn(x, mod3, rows_per_group, 0, wts["norm_ff1"], wts["ff1_w1"], wts["ff1_w3"],
             wts["ff1_w2"])
    proj = _proj(x, mod3, rows_per_group, 3, wts["norm_mix"], wts["w_packed"])
    mixer_args = (proj, bsz, T)
    m_args = (wts["m_conv_w"], wts["m_conv_b"], wts["small_bias"], wts["m_head_g"])
    s_args = (wts["s_conv_w"], wts["s_conv_b"], wts["small_bias"], wts["alog_row"],
              wts["d_row"], wts["s_norm_g"])
    if state is None:
        m_state = s_state = None
    else:
        c0, n0, m0, mconv0, ssm0, sconv0 = state
        m_state = (mconv0, c0, n0, m0)
        s_state = (sconv0, ssm0)
    hm, c1, n1, m1 = _mlstm(*mixer_args, *m_args, state=m_state)
    gated, h1 = _ssd(*mixer_args, *s_args, state=s_state)
    x = _merge(x, hm, gated, proj, mod3, rows_per_group, 5, wts["w_pa"], wts["w_pb"],
               wts["w_out"])
    y = _ffn(x, mod3, rows_per_group, 6, wts["norm_ff2"], wts["ff2_w1"], wts["ff2_w3"],
             wts["ff2_w2"], final_norm=final_norm)
    last = proj.reshape(bsz, T, P_WIDTH)[:, T - (CONV_W - 1):]
    mconv1 = last[:, :, P_QK:P_QK + 2 * M_QK_W]
    sconv1 = last[:, :, P_XBC:P_XBC + S_CONV_CH]
    return y.reshape(bsz, T, D_MODEL), (c1, n1, m1, mconv1, h1, sconv1)


def kernel(x_prompt, x_sample, c_prompt, c_sample, state_mlstm_C, state_mlstm_n, state_mlstm_m, state_mlstm_conv, state_ssm, state_ssm_conv, ada_w, ada_b, norm_ff1, ff1_w1, ff1_w3, ff1_w2, norm_mix, w_in, m_conv_w, m_conv_b, m_if_b, m_head_g, w_pa, s_conv_w, s_conv_b, s_dt_bias, s_A_log, s_D, s_norm_g, w_pb, w_out, norm_ff2, ff2_w1, ff2_w3, ff2_w2, final_norm):
    bp, tp, _ = x_prompt.shape
    bs, ts, _ = x_sample.shape
    depth = ada_w.shape[0]
    assert depth == 1

    hp, hs = x_prompt, x_sample
    new_p, new_s = [], []
    for l in range(depth):
        c_all = jnp.concatenate([c_prompt, c_sample], axis=0)
        pad_rows = (-c_all.shape[0]) % (2 * SUBLANES)
        c_all = jnp.pad(c_all, ((0, pad_rows), (0, 0)))
        mod = _ada(c_all, ada_w[l], ada_b[l])
        mod_p = mod[:bp].reshape(bp, 1, N_MOD * D_MODEL)
        mod_s = jnp.repeat(mod[bp:bp + bs], ts, axis=0).reshape(1, bs * ts, N_MOD * D_MODEL)

        lane_pad = LANES - (2 * M_HEADS + S_HEADS)
        small_bias = jnp.concatenate(
            [m_if_b[l], s_dt_bias[l], jnp.zeros((lane_pad,), F32)]).reshape(1, LANES)
        alog_row = jnp.concatenate(
            [jnp.zeros((2 * M_HEADS,), F32), s_A_log[l], jnp.zeros((lane_pad,), F32)]
        ).reshape(1, LANES)
        wts = {
            "norm_ff1": norm_ff1[l], "ff1_w1": ff1_w1[l], "ff1_w3": ff1_w3[l],
            "ff1_w2": ff1_w2[l],
            "norm_mix": norm_mix[l], "w_packed": _pack_w_in(w_in[l]),
            "m_conv_w": m_conv_w[l], "m_conv_b": m_conv_b[l], "small_bias": small_bias,
            "m_head_g": m_head_g[l], "w_pa": w_pa[l],
            "s_conv_w": s_conv_w[l], "s_conv_b": s_conv_b[l], "alog_row": alog_row,
            "d_row": jnp.repeat(s_D[l], S_HEADDIM).reshape(1, S_INNER),
            "s_norm_g": s_norm_g[l], "w_pb": w_pb[l], "w_out": w_out[l],
            "norm_ff2": norm_ff2[l], "ff2_w1": ff2_w1[l], "ff2_w3": ff2_w3[l],
            "ff2_w2": ff2_w2[l],
        }
        st_s = (state_mlstm_C[l], state_mlstm_n[l], state_mlstm_m[l], state_mlstm_conv[l],
                state_ssm[l], state_ssm_conv[l])
        hp, sp = _layer(hp, mod_p, tp, bp, tp, wts, final_norm, None)
        hs, ss = _layer(hs, mod_s, bs * ts, bs, ts, wts, final_norm, st_s)
        new_p.append(sp)
        new_s.append(ss)
    outs_p = [jnp.stack([s[i] for s in new_p]) for i in range(6)]
    outs_s = [jnp.stack([s[i] for s in new_s]) for i in range(6)]
    return (hp, hs, *outs_p, *outs_s)
```

```python
import functools

import jax
import jax.numpy as jnp
from jax import lax
from jax.experimental import pallas as pl
from jax.experimental.pallas import tpu as pltpu

F32 = jnp.float32
BF16 = jnp.bfloat16

D_MODEL = 1024
M_HEADS = 4
M_QK = 256
M_V = 512
M_QK_W = M_HEADS * M_QK
M_V_W = M_HEADS * M_V
S_INNER = 2048
S_HEADDIM = 64
S_HEADS = 32
S_GROUPS = 4
S_HPG = S_HEADS // S_GROUPS
S_STATE = 128
S_GW = S_INNER // S_GROUPS
S_BCW = S_GROUPS * S_STATE
S_CONV_CH = S_INNER + 2 * S_BCW
CONV_W = 4
D_FF = 2816
CHUNK = 128
EPS = 1e-6
N_MOD = 9
K_SCALE = M_QK ** -0.5

LANES = 128
SUBLANES = 8
VMEM_LIMIT = 56 * 1024 * 1024

P_QK = 0
P_V = 2 * M_QK_W
P_O = P_V + M_V_W
P_HEAD = P_O + M_V_W
P_Z = P_HEAD
P_XBC = P_Z + S_INNER
P_B = P_XBC + S_INNER
P_C = P_B + S_BCW
P_GA = P_XBC + S_CONV_CH
P_GB = P_GA + D_MODEL
P_SMALL = P_GB + D_MODEL
PROJ_TN = 1536
PACK_TN = 512
P_WIDTH = 13824
SM_I = 0
SM_F = M_HEADS
SM_DT = 2 * M_HEADS

FFN_TM = 2048
FFN_TF = 256
PROJ_TM = 1024
MERGE_TM = 256
SHORT_SEQS_PER_STEP = 4
CONV_SLAB = 256


def _sigmoid(x):
    return 0.5 * jnp.tanh(0.5 * x) + 0.5


def _silu(x):
    h = 0.5 * x
    return h * jnp.tanh(h) + h


def _rms(x, g):
    return x * lax.rsqrt(jnp.mean(x * x, axis=-1, keepdims=True) + EPS) * g


def _ada_kernel(c_ref, w_ref, b_ref, o_ref):
    s = _silu(c_ref[...]).astype(BF16)
    o_ref[...] = jnp.dot(s, w_ref[...].astype(BF16), preferred_element_type=F32) + b_ref[...]


def _ada(c_all, ada_w, ada_b):
    m = c_all.shape[0]
    n = ada_w.shape[1]
    tn = D_MODEL
    return pl.pallas_call(
        _ada_kernel,
        grid=(n // tn,),
        in_specs=[
            pl.BlockSpec((m, D_MODEL), lambda j: (0, 0)),
            pl.BlockSpec((D_MODEL, tn), lambda j: (0, j)),
            pl.BlockSpec((1, tn), lambda j: (0, j)),
        ],
        out_specs=pl.BlockSpec((m, tn), lambda j: (0, j)),
        out_shape=jax.ShapeDtypeStruct((m, n), F32),
        compiler_params=pltpu.CompilerParams(
            dimension_semantics=("arbitrary",), vmem_limit_bytes=VMEM_LIMIT),
        name="ada_mod",
    )(c_all, ada_w, ada_b.reshape(1, n))


def _ffn_kernel(x_ref, sh_ref, sc_ref, g_ref, nw_ref, w1_ref, w3_ref, w2_ref, *rest,
                nf, final):
    if final:
        fn_ref, o_ref, u_ref = rest
    else:
        o_ref, u_ref = rest
    f = pl.program_id(1)

    @pl.when(f == 0)
    def _():
        xn = _rms(x_ref[...], nw_ref[...])
        u_ref[...] = (xn * (1.0 + sc_ref[0]) + sh_ref[0]).astype(BF16)
        o_ref[...] = jnp.zeros_like(o_ref)

    u = u_ref[...]
    h1 = jnp.dot(u, w1_ref[...].astype(BF16), preferred_element_type=F32)
    h3 = jnp.dot(u, w3_ref[...].astype(BF16), preferred_element_type=F32)
    a = (_silu(h1) * h3).astype(BF16)
    o_ref[...] += jnp.dot(a, w2_ref[...].astype(BF16), preferred_element_type=F32)

    @pl.when(f == nf - 1)
    def _():
        y = x_ref[...] + (0.5 * g_ref[0]) * o_ref[...]
        if final:
            y = _rms(y, fn_ref[...])
        o_ref[...] = y


def _mod_spec(mod3, rows_per_group, tm, k):
    if mod3.shape[1] == 1:
        tiles_per_group = rows_per_group // tm
        return pl.BlockSpec((1, 1, D_MODEL), lambda i, *_: (i // tiles_per_group, 0, k))
    return pl.BlockSpec((1, tm, D_MODEL), lambda i, *_: (0, i, k))


def _ffn(x, mod3, rows_per_group, kmod, norm_w, w1, w3, w2, final_norm=None):
    n = x.shape[0]
    tm = min(FFN_TM, rows_per_group)
    tf = FFN_TF
    nf = D_FF // tf
    final = final_norm is not None
    in_specs = [
        pl.BlockSpec((tm, D_MODEL), lambda i, f: (i, 0)),
        _mod_spec(mod3, rows_per_group, tm, kmod),
        _mod_spec(mod3, rows_per_group, tm, kmod + 1),
        _mod_spec(mod3, rows_per_group, tm, kmod + 2),
        pl.BlockSpec((1, D_MODEL), lambda i, f: (0, 0)),
        pl.BlockSpec((D_MODEL, tf), lambda i, f: (0, f)),
        pl.BlockSpec((D_MODEL, tf), lambda i, f: (0, f)),
        pl.BlockSpec((tf, D_MODEL), lambda i, f: (f, 0)),
    ]
    args = [x, mod3, mod3, mod3, norm_w.reshape(1, D_MODEL), w1, w3, w2]
    if final:
        in_specs.append(pl.BlockSpec((1, D_MODEL), lambda i, f: (0, 0)))
        args.append(final_norm.reshape(1, D_MODEL))
    return pl.pallas_call(
        functools.partial(_ffn_kernel, nf=nf, final=final),
        grid=(n // tm, nf),
        in_specs=in_specs,
        out_specs=pl.BlockSpec((tm, D_MODEL), lambda i, f: (i, 0)),
        out_shape=jax.ShapeDtypeStruct((n, D_MODEL), F32),
        scratch_shapes=[pltpu.VMEM((tm, D_MODEL), BF16)],
        compiler_params=pltpu.CompilerParams(
            dimension_semantics=("parallel", "arbitrary"), vmem_limit_bytes=VMEM_LIMIT),
        name="ffn_final" if final else "ffn",
    )(*args)


PACK_SHIFT_Z = 2 * M_HEADS
PACK_SHIFT_GA = PACK_SHIFT_Z + S_HEADS
SRC_IF = P_HEAD
SRC_DT = P_HEAD + PACK_SHIFT_Z + S_INNER + S_CONV_CH


def _pack_kernel(a_ref, gif_ref, gdt_ref, o_ref):
    j = pl.program_id(0)

    @pl.when(j < P_SMALL // PACK_TN)
    def _():
        o_ref[...] = a_ref[...].astype(BF16)

    @pl.when(j == P_SMALL // PACK_TN)
    def _():
        pad = jnp.zeros((PACK_TN - SM_DT - S_HEADS, o_ref.shape[1]), F32)
        o_ref[...] = jnp.concatenate([gif_ref[...], gdt_ref[...], pad], axis=0).astype(BF16)


def _pack_w_in(w_t):
    d = w_t.shape[1]
    n_main = P_SMALL // PACK_TN
    last_start = w_t.shape[0] - PACK_TN

    def src_row(j):
        shift = jnp.where(j < P_HEAD // PACK_TN, 0,
                          jnp.where(j < P_GA // PACK_TN, PACK_SHIFT_Z, PACK_SHIFT_GA))
        tile = jnp.minimum((j * PACK_TN + shift) // SUBLANES, last_start // SUBLANES)
        return tile * SUBLANES

    return pl.pallas_call(
        _pack_kernel,
        grid=(n_main + 1,),
        in_specs=[
            pl.BlockSpec((pl.Element(PACK_TN), pl.Element(d)), lambda j: (src_row(j), 0)),
            pl.BlockSpec((pl.Element(SM_DT), pl.Element(d)), lambda j: (SRC_IF, 0)),
            pl.BlockSpec((pl.Element(S_HEADS), pl.Element(d)), lambda j: (SRC_DT, 0)),
        ],
        out_specs=pl.BlockSpec((PACK_TN, d), lambda j: (j, 0)),
        out_shape=jax.ShapeDtypeStruct((P_WIDTH, d), BF16),
        compiler_params=pltpu.CompilerParams(
            dimension_semantics=("arbitrary",), vmem_limit_bytes=VMEM_LIMIT),
        name="pack_w_in",
    )(w_t, w_t, w_t)


def _proj_kernel(x_ref, sh_ref, sc_ref, nw_ref, w_ref, o_ref, u_ref):
    @pl.when(pl.program_id(1) == 0)
    def _():
        xn = _rms(x_ref[...], nw_ref[...])
        u_ref[...] = (xn * (1.0 + sc_ref[0]) + sh_ref[0]).astype(BF16)

    o_ref[...] = _nt_dot(u_ref[...], w_ref[...])


def _proj(x, mod3, rows_per_group, kmod, norm_w, w_packed):
    n = x.shape[0]
    tm = min(PROJ_TM, rows_per_group)
    tn = PROJ_TN
    return pl.pallas_call(
        _proj_kernel,
        grid=(n // tm, P_WIDTH // tn),
        in_specs=[
            pl.BlockSpec((tm, D_MODEL), lambda i, j: (i, 0)),
            _mod_spec(mod3, rows_per_group, tm, kmod),
            _mod_spec(mod3, rows_per_group, tm, kmod + 1),
            pl.BlockSpec((1, D_MODEL), lambda i, j: (0, 0)),
            pl.BlockSpec((tn, D_MODEL), lambda i, j: (j, 0)),
        ],
        out_specs=pl.BlockSpec((tm, tn), lambda i, j: (i, j)),
        out_shape=jax.ShapeDtypeStruct((n, P_WIDTH), F32),
        scratch_shapes=[pltpu.VMEM((tm, D_MODEL), BF16)],
        compiler_params=pltpu.CompilerParams(
            dimension_semantics=("parallel", "arbitrary"), vmem_limit_bytes=VMEM_LIMIT),
        name="mix_proj",
    )(x, mod3, mod3, norm_w.reshape(1, D_MODEL), w_packed)


def _conv_silu(xp_ref, x_ref, w_ref, b_ref, L, dst_ref, slabs=None):
    if slabs is None:
        slabs = range(x_ref.shape[1] // CONV_SLAB)
    for s in slabs:
        cols = slice(s * CONV_SLAB, (s + 1) * CONV_SLAB)
        xp_ref[pl.ds(SUBLANES, L), cols] = x_ref[:, cols]
        xp = xp_ref[:, cols]
        acc = b_ref[:, cols] + xp[SUBLANES:] * w_ref[CONV_W - 1:CONV_W, cols]
        for back in range(1, CONV_W):
            shifted = pltpu.roll(xp, back, axis=0)[SUBLANES:]
            acc = acc + shifted * w_ref[CONV_W - 1 - back:CONV_W - back, cols]
        dst_ref[:, cols] = _silu(acc)
        xp_ref[pl.ds(0, SUBLANES), cols] = xp_ref[pl.ds(L, SUBLANES), cols]


def _init_conv(xp_ref, init_ref):
    xp_ref[pl.ds(0, SUBLANES), :] = jnp.zeros((SUBLANES, xp_ref.shape[1]), F32)
    if init_ref is not None:
        xp_ref[pl.ds(SUBLANES - (CONV_W - 1), CONV_W - 1), :] = init_ref[0]


def _run_stages(streams):
    live = list(streams)
    while live:
        for stream in list(live):
            try:
                next(stream)
            except StopIteration:
                live.remove(stream)


def _split3(a):
    hi = a.astype(BF16)
    r1 = a - hi.astype(F32)
    mid = r1.astype(BF16)
    lo = (r1 - mid.astype(F32)).astype(BF16)
    return hi, mid, lo


def _nt_dot(a, b):
    return lax.dot_general(a, b, (((1,), (1,)), ((), ())), preferred_element_type=F32)


def _tn_dot(a, b):
    return lax.dot_general(a, b, (((0,), (0,)), ((), ())), preferred_element_type=F32)


def _dot3(mask_b, a):
    return sum(jnp.dot(mask_b, p, preferred_element_type=F32) for p in _split3(a))


def _tn3(a, mask_b):
    return sum(_tn_dot(p, mask_b) for p in _split3(a))


def _log_sigmoid(x):
    return jnp.minimum(x, 0.0) - jnp.log1p(jnp.exp(-jnp.abs(x)))


def _softplus(x):
    return jnp.maximum(x, 0.0) + jnp.log1p(jnp.exp(-jnp.abs(x)))


def _mlstm_kernel(*refs, L, nseq, has_state):
    qk_ref, v_ref, og_ref, sm_ref, sb_ref, hg_ref, w_ref, b_ref = refs[:8]
    refs = refs[8:]
    if has_state:
        cv0_ref, c0_ref, n0_ref, m0_ref = refs[:4]
        refs = refs[4:]
    o_ref, c_ref, n_ref, m_ref, xp_ref, qk_s = refs

    @pl.when(pl.program_id(1) == 0)
    def _():
        if has_state:
            c_ref[...] = c0_ref[...]
            n_ref[...] = n0_ref[...]
            m_ref[...] = m0_ref[...]
        else:
            c_ref[...] = jnp.zeros_like(c_ref)
            n_ref[...] = jnp.zeros_like(n_ref)
            m_ref[...] = jnp.zeros_like(m_ref)
        for s in range(nseq):
            _init_conv(xp_ref.at[s], cv0_ref.at[pl.ds(s, 1)] if has_state else None)

    rr = lax.broadcasted_iota(jnp.int32, (L, L), 0)
    cc = lax.broadcasted_iota(jnp.int32, (L, L), 1)
    tril = cc <= rr
    gates = []
    for s in range(nseq):
        sm = sm_ref[pl.ds(s * L, L), :] + sb_ref[...]
        lf_all = _log_sigmoid(sm)
        b_all = _dot3(tril.astype(BF16), lf_all)
        b_all_t = _tn3(lf_all, (rr <= cc).astype(BF16))
        sm_t = _tn3(sm, (rr == cc).astype(BF16))
        gates.append((sm, sm_t, b_all, b_all_t))

    def head_stages(s, h):
        rows = pl.ds(s * L, L)
        sm, sm_t, b_all, b_all_t = gates[s]
        qk_seq = qk_s.at[s]
        _conv_silu(xp_ref.at[s], qk_ref.at[rows], w_ref, b_ref, L, qk_seq,
                   slabs=[h * M_QK // CONV_SLAB, (M_QK_W + h * M_QK) // CONV_SLAB])
        yield
        q = qk_seq[:, h * M_QK:(h + 1) * M_QK]
        k = qk_seq[:, M_QK_W + h * M_QK:M_QK_W + (h + 1) * M_QK] * K_SCALE
        v = v_ref[rows, h * M_V:(h + 1) * M_V]
        i_col = sm[:, SM_I + h:SM_I + h + 1]
        i_row = sm_t[SM_I + h:SM_I + h + 1, :]
        b_col = b_all[:, SM_F + h:SM_F + h + 1]
        b_row = b_all_t[SM_F + h:SM_F + h + 1, :]

        c_prev = c_ref[s, h]
        n_prev = n_ref[s, h]
        m_prev = m_ref[s, h]

        dmat = jnp.where(tril, b_col - b_row + i_row, -jnp.inf)
        inter = b_col + m_prev
        mt = jnp.maximum(inter, jnp.max(dmat, axis=1, keepdims=True))
        w_tok = jnp.exp(dmat - mt)
        w_st = jnp.exp(inter - mt)
        yield

        qb = q.astype(BF16)
        kb = k.astype(BF16)
        vb = v.astype(BF16)
        sc = _nt_dot(qb, kb) * w_tok
        yield
        num = (jnp.dot(sc.astype(BF16), vb, preferred_element_type=F32)
               + w_st * jnp.dot(qb, c_prev.astype(BF16), preferred_element_type=F32))
        den = (jnp.sum(sc, axis=1, keepdims=True)
               + w_st * jnp.sum(q * n_prev, axis=1, keepdims=True))
        yield
        hh = num / jnp.maximum(jnp.abs(den), jnp.exp(-mt))

        m_new = mt[L - 1:L, :]
        b_last = b_col[L - 1:L, :]
        ws = jnp.exp(b_last - b_col + i_col - m_new)
        decay = jnp.exp(b_last + m_prev - m_new)
        kw = k * ws
        c_ref[s, h] = decay * c_prev + _tn_dot(kw.astype(BF16), vb)
        n_ref[s, h] = decay * n_prev + jnp.sum(kw, axis=0, keepdims=True)
        m_ref[s, h] = m_new
        yield

        mu = jnp.mean(hh, axis=-1, keepdims=True)
        hc = hh - mu
        yield
        var = jnp.mean(hc * hc, axis=-1, keepdims=True)
        hsl = slice(h * M_V, (h + 1) * M_V)
        o_ref[rows, hsl] = (hc * lax.rsqrt(var + EPS) * hg_ref[:, hsl]
                            * _sigmoid(og_ref[rows, hsl]))

    _run_stages([head_stages(s, h) for s in range(nseq) for h in range(M_HEADS)])


def _seqs_per_step(bsz, nc):
    return SHORT_SEQS_PER_STEP if nc == 1 and bsz % SHORT_SEQS_PER_STEP == 0 else 1


def _mlstm(proj, bsz, T, m_conv_w, m_conv_b, small_bias, m_head_g, state=None):
    L = min(T, CHUNK)
    nc = T // L
    nseq = _seqs_per_step(bsz, nc)
    rows = nseq * L
    has_state = state is not None
    qkw = 2 * M_QK_W

    def row(b, c):
        return b * nc + c

    const = lambda b, c: (0, 0)
    per_b = lambda b, c: (b, 0, 0, 0)
    in_specs = [
        pl.BlockSpec((rows, qkw), lambda b, c: (row(b, c), P_QK // qkw)),
        pl.BlockSpec((rows, M_V_W), lambda b, c: (row(b, c), P_V // M_V_W)),
        pl.BlockSpec((rows, M_V_W), lambda b, c: (row(b, c), P_O // M_V_W)),
        pl.BlockSpec((rows, LANES), lambda b, c: (row(b, c), P_SMALL // LANES)),
        pl.BlockSpec((1, LANES), const),
        pl.BlockSpec((1, M_V_W), const),
        pl.BlockSpec((CONV_W, qkw), const),
        pl.BlockSpec((1, qkw), const),
    ]
    args = [proj, proj, proj, proj, small_bias, m_head_g.reshape(1, -1),
            m_conv_w, m_conv_b.reshape(1, -1)]
    scratch = [pltpu.VMEM((nseq, L + SUBLANES, qkw), F32), pltpu.VMEM((nseq, L, qkw), F32)]
    if has_state:
        conv0, c0, n0, m0 = state
        in_specs += [
            pl.BlockSpec((nseq, CONV_W - 1, qkw), lambda b, c: (b, 0, 0)),
            pl.BlockSpec((nseq, M_HEADS, M_QK, M_V), per_b),
            pl.BlockSpec((nseq, M_HEADS, 1, M_QK), per_b),
            pl.BlockSpec((nseq, M_HEADS, 1, 1), per_b),
        ]
        args += [conv0, c0, n0.reshape(bsz, M_HEADS, 1, M_QK), m0.reshape(bsz, M_HEADS, 1, 1)]
    out_shape = (
        jax.ShapeDtypeStruct((bsz * T, M_V_W), F32),
        jax.ShapeDtypeStruct((bsz, M_HEADS, M_QK, M_V), F32),
        jax.ShapeDtypeStruct((bsz, M_HEADS, 1, M_QK), F32),
        jax.ShapeDtypeStruct((bsz, M_HEADS, 1, 1), F32),
    )
    out_specs = (
        pl.BlockSpec((rows, M_V_W), lambda b, c: (row(b, c), 0)),
        pl.BlockSpec((nseq, M_HEADS, M_QK, M_V), per_b),
        pl.BlockSpec((nseq, M_HEADS, 1, M_QK), per_b),
        pl.BlockSpec((nseq, M_HEADS, 1, 1), per_b),
    )
    out, c1, n1, m1 = pl.pallas_call(
        functools.partial(_mlstm_kernel, L=L, nseq=nseq, has_state=has_state),
        grid=(bsz // nseq, nc),
        in_specs=in_specs,
        out_specs=out_specs,
        out_shape=out_shape,
        scratch_shapes=scratch,
        compiler_params=pltpu.CompilerParams(
            dimension_semantics=("arbitrary", "arbitrary"), vmem_limit_bytes=VMEM_LIMIT),
        name="mlstm_state" if has_state else "mlstm",
    )(*args)
    return out, c1, n1.reshape(bsz, M_HEADS, M_QK), m1.reshape(bsz, M_HEADS)


def _ssd_kernel(*refs, L, nseq, has_state):
    n_shared = 13
    streams = []
    for sq in range(nseq):
        rows, one = pl.ds(sq * L, L), pl.ds(sq, 1)
        views = [r.at[rows] for r in refs[:5]] + list(refs[5:n_shared])
        rest = refs[n_shared:]
        if has_state:
            views += [r.at[one] for r in rest[:2]]
            rest = rest[2:]
        o_ref, h_ref = rest[:2]
        views += [o_ref.at[rows], h_ref.at[one]] + [r.at[sq] for r in rest[2:]]
        streams += _ssd_seq_streams(*views, L=L, has_state=has_state)
    _run_stages(streams)


def _ssd_seq_streams(*refs, L, has_state):
    (z_ref, xs_ref, bp_ref, cp_ref, sm_ref, sb_ref, al_ref, d_ref, ng_ref, e_ref, et_ref,
     w_ref, b_ref) = refs[:13]
    refs = refs[13:]
    if has_state:
        cv0_ref, h0_ref = refs[:2]
        refs = refs[2:]
    o_ref, h_ref, xx_ref, xb_ref, xc_ref, xs_s, bm_s, cm_s = refs

    @pl.when(pl.program_id(1) == 0)
    def _():
        if has_state:
            h_ref[...] = h0_ref[...]
        else:
            h_ref[...] = jnp.zeros_like(h_ref)

    x_cols = slice(0, S_INNER)
    b_cols = slice(S_INNER, S_INNER + S_BCW)
    c_cols = slice(S_INNER + S_BCW, S_CONV_CH)
    for xp_ref, cols in ((xx_ref, x_cols), (xb_ref, b_cols), (xc_ref, c_cols)):
        @pl.when(pl.program_id(1) == 0)
        def _():
            _init_conv(xp_ref, cv0_ref.at[:, :, cols] if has_state else None)

    _conv_silu(xb_ref, bp_ref, w_ref.at[:, b_cols], b_ref.at[:, b_cols], L, bm_s)
    _conv_silu(xc_ref, cp_ref, w_ref.at[:, c_cols], b_ref.at[:, c_cols], L, cm_s)

    dt_all = _softplus(sm_ref[...] + sb_ref[...])
    da_all = dt_all * (-jnp.exp(al_ref[...]))
    lane = lax.broadcasted_iota(jnp.int32, (L, LANES), 1)
    rr = lax.broadcasted_iota(jnp.int32, (L, L), 0)
    cc = lax.broadcasted_iota(jnp.int32, (L, L), 1)
    tril = cc <= rr
    lo = lane < S_HEADDIM
    cum_all = _dot3(tril.astype(BF16), da_all)
    cum_all_t = _tn3(da_all, (rr <= cc).astype(BF16))
    cum_last = cum_all[L - 1:L, :]
    dec_all = dt_all * jnp.exp(cum_last - cum_all)
    ecum_all = jnp.exp(cum_all)
    mxu_spread = L >= LANES
    if mxu_spread:
        dt_src = _split3(dt_all)[:2]
        dec_src = _split3(dec_all)[:2]
        ecum_src = _split3(ecum_all)[:2]
        clast_src = _split3(jnp.broadcast_to(cum_all_t[:, L - 1:L], (LANES, LANES)))
    else:
        dt_src, dec_src, ecum_src = dt_all, dec_all, ecum_all
        lo_row = lax.broadcasted_iota(jnp.int32, (LANES, 1), 0) < S_HEADDIM

    def spread(src, pair):
        if mxu_spread:
            return sum(jnp.dot(p, e_ref[pair], preferred_element_type=F32) for p in src)
        idx = SM_DT + 2 * pair
        return jnp.where(lo, src[:, idx:idx + 1], src[:, idx + 1:idx + 2])

    def state_decay(pair):
        if mxu_spread:
            return jnp.exp(sum(jnp.dot(et_ref[pair], p, preferred_element_type=F32)
                               for p in clast_src))
        idx = SM_DT + 2 * pair
        return jnp.exp(jnp.where(lo_row, cum_last[:, idx:idx + 1], cum_last[:, idx + 1:idx + 2]))

    def group_stages(g):
        _conv_silu(xx_ref, xs_ref, w_ref.at[:, x_cols], b_ref.at[:, x_cols], L, xs_s,
                   slabs=range(g * S_GW // CONV_SLAB, (g + 1) * S_GW // CONV_SLAB))
        yield
        bmb = bm_s[:, g * S_STATE:(g + 1) * S_STATE].astype(BF16)
        cmb = cm_s[:, g * S_STATE:(g + 1) * S_STATE].astype(BF16)
        cb = _nt_dot(cmb, bmb)
        ssq = jnp.zeros((L, 1), F32)
        for p in range(S_HPG // 2):
            yield
            pair = g * (S_HPG // 2) + p
            psl = slice(g * S_GW + p * LANES, g * S_GW + (p + 1) * LANES)
            hsl = slice(p * LANES, (p + 1) * LANES)
            lmat = []
            for e in range(2):
                idx = SM_DT + 2 * pair + e
                lmat.append(jnp.exp(jnp.where(
                    tril, cum_all[:, idx:idx + 1] - cum_all_t[idx:idx + 1, :], -jnp.inf)))
            xs = xs_s[:, psl]
            xdt = xs * spread(dt_src, pair)
            xdd = xs * spread(dec_src, pair)
            h_prev = h_ref[0, g, hsl, :]
            yield
            y = (jnp.dot((cb * lmat[0]).astype(BF16), jnp.where(lo, xdt, 0.0).astype(BF16),
                         preferred_element_type=F32)
                 + jnp.dot((cb * lmat[1]).astype(BF16), jnp.where(lo, 0.0, xdt).astype(BF16),
                           preferred_element_type=F32)
                 + spread(ecum_src, pair) * _nt_dot(cmb, h_prev.astype(BF16)))
            h_ref[0, g, hsl, :] = state_decay(pair) * h_prev + _tn_dot(xdd.astype(BF16), bmb)
            yield
            gated = (y + d_ref[:, psl] * xs) * _silu(z_ref[:, psl])
            ssq = ssq + jnp.sum(gated * gated, axis=1, keepdims=True)
            o_ref[:, psl] = gated
        yield
        gsl = slice(g * S_GW, (g + 1) * S_GW)
        o_ref[:, gsl] = o_ref[:, gsl] * lax.rsqrt(ssq * (1.0 / S_GW) + EPS) * ng_ref[:, gsl]

    return [group_stages(g) for g in range(S_GROUPS)]


def _pair_onehots():
    n_pairs = S_HEADS // 2
    pair = jnp.arange(n_pairs)[:, None, None]
    src = jnp.arange(LANES)[None, :, None]
    dst = jnp.arange(LANES)[None, None, :]
    onehot = src == SM_DT + 2 * pair + dst // S_HEADDIM
    return onehot.astype(BF16), jnp.swapaxes(onehot, 1, 2).astype(BF16)


def _ssd(proj, bsz, T, s_conv_w, s_conv_b, small_bias, alog_row, d_row, s_norm_g, state=None):
    L = min(T, CHUNK)
    nc = T // L
    nseq = _seqs_per_step(bsz, nc)
    rows = nseq * L
    has_state = state is not None
    n_pairs = S_HEADS // 2
    onehot, onehot_t = _pair_onehots()

    def row(b, c):
        return b * nc + c

    const = lambda b, c: (0, 0)
    const3 = lambda b, c: (0, 0, 0)
    in_specs = [
        pl.BlockSpec((rows, S_INNER), lambda b, c: (row(b, c), P_Z // S_INNER)),
        pl.BlockSpec((rows, S_INNER), lambda b, c: (row(b, c), P_XBC // S_INNER)),
        pl.BlockSpec((rows, S_BCW), lambda b, c: (row(b, c), P_B // S_BCW)),
        pl.BlockSpec((rows, S_BCW), lambda b, c: (row(b, c), P_C // S_BCW)),
        pl.BlockSpec((rows, LANES), lambda b, c: (row(b, c), P_SMALL // LANES)),
        pl.BlockSpec((1, LANES), const),
        pl.BlockSpec((1, LANES), const),
        pl.BlockSpec((1, S_INNER), const),
        pl.BlockSpec((1, S_INNER), const),
        pl.BlockSpec((n_pairs, LANES, LANES), const3),
        pl.BlockSpec((n_pairs, LANES, LANES), const3),
        pl.BlockSpec((CONV_W, S_CONV_CH), const),
        pl.BlockSpec((1, S_CONV_CH), const),
    ]
    args = [proj, proj, proj, proj, proj, small_bias, alog_row, d_row, s_norm_g.reshape(1, -1),
            onehot, onehot_t, s_conv_w, s_conv_b.reshape(1, -1)]
    scratch = [pltpu.VMEM((nseq, L + SUBLANES, S_INNER), F32),
               pltpu.VMEM((nseq, L + SUBLANES, S_BCW), F32),
               pltpu.VMEM((nseq, L + SUBLANES, S_BCW), F32),
               pltpu.VMEM((nseq, L, S_INNER), F32),
               pltpu.VMEM((nseq, L, S_BCW), F32),
               pltpu.VMEM((nseq, L, S_BCW), F32)]
    if has_state:
        conv0, h0 = state
        in_specs += [
            pl.BlockSpec((nseq, CONV_W - 1, S_CONV_CH), lambda b, c: (b, 0, 0)),
            pl.BlockSpec((nseq, S_GROUPS, S_GW, S_STATE), lambda b, c: (b, 0, 0, 0)),
        ]
        args += [conv0, h0.reshape(bsz, S_GROUPS, S_GW, S_STATE)]
    out, h1 = pl.pallas_call(
        functools.partial(_ssd_kernel, L=L, nseq=nseq, has_state=has_state),
        grid=(bsz // nseq, nc),
        in_specs=in_specs,
        out_specs=(
            pl.BlockSpec((rows, S_INNER), lambda b, c: (row(b, c), 0)),
            pl.BlockSpec((nseq, S_GROUPS, S_GW, S_STATE), lambda b, c: (b, 0, 0, 0)),
        ),
        out_shape=(
            jax.ShapeDtypeStruct((bsz * T, S_INNER), F32),
            jax.ShapeDtypeStruct((bsz, S_GROUPS, S_GW, S_STATE), F32),
        ),
        scratch_shapes=scratch,
        compiler_params=pltpu.CompilerParams(
            dimension_semantics=("arbitrary", "arbitrary"), vmem_limit_bytes=VMEM_LIMIT),
        name="ssd_state" if has_state else "ssd",
    )(*args)
    return out, h1.reshape(bsz, S_HEADS, S_HEADDIM, S_STATE)


def _merge_kernel(x_ref, hm_ref, gt_ref, ga_ref, gb_ref, g_ref, wpa_ref, wpb_ref, wo_ref, o_ref,
                  wpa_s, wpb_s, wo_s):
    @pl.when(pl.program_id(0) == 0)
    def _():
        wpa_s[...] = wpa_ref[...].astype(BF16)
        wpb_s[...] = wpb_ref[...].astype(BF16)
        wo_s[...] = wo_ref[...].astype(BF16)

    ya = jnp.dot(hm_ref[...].astype(BF16), wpa_s[...], preferred_element_type=F32)
    yb = jnp.dot(gt_ref[...].astype(BF16), wpb_s[...], preferred_element_type=F32)
    merged = _sigmoid(ga_ref[...]) * ya + _sigmoid(gb_ref[...]) * yb
    o_ref[...] = x_ref[...] + g_ref[0] * jnp.dot(
        merged.astype(BF16), wo_s[...], preferred_element_type=F32)


def _merge(x, hm, gated, proj, mod3, rows_per_group, kmod, w_pa, w_pb, w_out):
    n = x.shape[0]
    tm = MERGE_TM
    const = lambda i: (0, 0)
    resident = functools.partial(pl.BlockSpec, index_map=const, pipeline_mode=pl.Buffered(1))
    return pl.pallas_call(
        _merge_kernel,
        grid=(n // tm,),
        in_specs=[
            pl.BlockSpec((tm, D_MODEL), lambda i: (i, 0)),
            pl.BlockSpec((tm, M_V_W), lambda i: (i, 0)),
            pl.BlockSpec((tm, S_INNER), lambda i: (i, 0)),
            pl.BlockSpec((tm, D_MODEL), lambda i: (i, P_GA // D_MODEL)),
            pl.BlockSpec((tm, D_MODEL), lambda i: (i, P_GB // D_MODEL)),
            _mod_spec(mod3, rows_per_group, tm, kmod),
            resident((M_V_W, D_MODEL)),
            resident((S_INNER, D_MODEL)),
            resident((D_MODEL, D_MODEL)),
        ],
        out_specs=pl.BlockSpec((tm, D_MODEL), lambda i: (i, 0)),
        out_shape=jax.ShapeDtypeStruct((n, D_MODEL), F32),
        scratch_shapes=[pltpu.VMEM((M_V_W, D_MODEL), BF16),
                        pltpu.VMEM((S_INNER, D_MODEL), BF16),
                        pltpu.VMEM((D_MODEL, D_MODEL), BF16)],
        compiler_params=pltpu.CompilerParams(
            dimension_semantics=("arbitrary",), vmem_limit_bytes=VMEM_LIMIT),
        name="merge",
    )(x, hm, gated, proj, proj, mod3, w_pa, w_pb, w_out)


def _layer(x, mod3, rows_per_group, bsz, T, wts, final_norm, state):
    n = bsz * T
    x = x.reshape(n, D_MODEL)
    x = _ffn(x, mod3, rows_per_group, 0, wts["norm_ff1"], wts["ff1_w1"], wts["ff1_w3"],
             wts["ff1_w2"])
    proj = _proj(x, mod3, rows_per_group, 3, wts["norm_mix"], wts["w_packed"])
    mixer_args = (proj, bsz, T)
    m_args = (wts["m_conv_w"], wts["m_conv_b"], wts["small_bias"], wts["m_head_g"])
    s_args = (wts["s_conv_w"], wts["s_conv_b"], wts["small_bias"], wts["alog_row"],
              wts["d_row"], wts["s_norm_g"])
    if state is None:
        m_state = s_state = None
    else:
        c0, n0, m0, mconv0, ssm0, sconv0 = state
        m_state = (mconv0, c0, n0, m0)
        s_state = (sconv0, ssm0)
    hm, c1, n1, m1 = _mlstm(*mixer_args, *m_args, state=m_state)
    gated, h1 = _ssd(*mixer_args, *s_args, state=s_state)
    x = _merge(x, hm, gated, proj, mod3, rows_per_group, 5, wts["w_pa"], wts["w_pb"],
               wts["w_out"])
    y = _ffn(x, mod3, rows_per_group, 6, wts["norm_ff2"], wts["ff2_w1"], wts["ff2_w3"],
             wts["ff2_w2"], final_norm=final_norm)
    last = proj.reshape(bsz, T, P_WIDTH)[:, T - (CONV_W - 1):]
    mconv1 = last[:, :, P_QK:P_QK + 2 * M_QK_W]
    sconv1 = last[:, :, P_XBC:P_XBC + S_CONV_CH]
    return y.reshape(bsz, T, D_MODEL), (c1, n1, m1, mconv1, h1, sconv1)


def kernel(x_prompt, x_sample, c_prompt, c_sample, state_mlstm_C, state_mlstm_n, state_mlstm_m, state_mlstm_conv, state_ssm, state_ssm_conv, ada_w, ada_b, norm_ff1, ff1_w1, ff1_w3, ff1_w2, norm_mix, w_in, m_conv_w, m_conv_b, m_if_b, m_head_g, w_pa, s_conv_w, s_conv_b, s_dt_bias, s_A_log, s_D, s_norm_g, w_pb, w_out, norm_ff2, ff2_w1, ff2_w3, ff2_w2, final_norm):
    bp, tp, _ = x_prompt.shape
    bs, ts, _ = x_sample.shape
    depth = ada_w.shape[0]
    assert depth == 1

    hp, hs = x_prompt, x_sample
    new_p, new_s = [], []
    for l in range(depth):
        c_all = jnp.concatenate([c_prompt, c_sample], axis=0)
        pad_rows = (-c_all.shape[0]) % (2 * SUBLANES)
        c_all = jnp.pad(c_all, ((0, pad_rows), (0, 0)))
        mod = _ada(c_all, ada_w[l], ada_b[l])
        mod_p = mod[:bp].reshape(bp, 1, N_MOD * D_MODEL)
        mod_s = jnp.repeat(mod[bp:bp + bs], ts, axis=0).reshape(1, bs * ts, N_MOD * D_MODEL)

        lane_pad = LANES - (2 * M_HEADS + S_HEADS)
        small_bias = jnp.concatenate(
            [m_if_b[l], s_dt_bias[l], jnp.zeros((lane_pad,), F32)]).reshape(1, LANES)
        alog_row = jnp.concatenate(
            [jnp.zeros((2 * M_HEADS,), F32), s_A_log[l], jnp.zeros((lane_pad,), F32)]
        ).reshape(1, LANES)
        wts = {
            "norm_ff1": norm_ff1[l], "ff1_w1": ff1_w1[l], "ff1_w3": ff1_w3[l],
            "ff1_w2": ff1_w2[l],
            "norm_mix": norm_mix[l], "w_packed": _pack_w_in(jnp.swapaxes(w_in, 1, 2)[l]),
            "m_conv_w": m_conv_w[l], "m_conv_b": m_conv_b[l], "small_bias": small_bias,
            "m_head_g": m_head_g[l], "w_pa": w_pa[l],
            "s_conv_w": s_conv_w[l], "s_conv_b": s_conv_b[l], "alog_row": alog_row,
            "d_row": jnp.repeat(s_D[l], S_HEADDIM).reshape(1, S_INNER),
            "s_norm_g": s_norm_g[l], "w_pb": w_pb[l], "w_out": w_out[l],
            "norm_ff2": norm_ff2[l], "ff2_w1": ff2_w1[l], "ff2_w3": ff2_w3[l],
            "ff2_w2": ff2_w2[l],
        }
        st_s = (state_mlstm_C[l], state_mlstm_n[l], state_mlstm_m[l], state_mlstm_conv[l],
                state_ssm[l], state_ssm_conv[l])
        hp, sp = _layer(hp, mod_p, tp, bp, tp, wts, final_norm, None)
        hs, ss = _layer(hs, mod_s, bs * ts, bs, ts, wts, final_norm, st_s)
        new_p.append(sp)
        new_s.append(ss)
    outs_p = [jnp.stack([s[i] for s in new_p]) for i in range(6)]
    outs_s = [jnp.stack([s[i] for s in new_s]) for i in range(6)]
    return (hp, hs, *outs_p, *outs_s)
```

```python
import functools

import jax
import jax.numpy as jnp
from jax import lax
from jax.experimental import pallas as pl
from jax.experimental.pallas import tpu as pltpu

F32 = jnp.float32
BF16 = jnp.bfloat16

D_MODEL = 1024
M_HEADS = 4
M_QK = 256
M_V = 512
M_QK_W = M_HEADS * M_QK
M_V_W = M_HEADS * M_V
S_INNER = 2048
S_HEADDIM = 64
S_HEADS = 32
S_GROUPS = 4
S_HPG = S_HEADS // S_GROUPS
S_STATE = 128
S_GW = S_INNER // S_GROUPS
S_BCW = S_GROUPS * S_STATE
S_CONV_CH = S_INNER + 2 * S_BCW
CONV_W = 4
D_FF = 2816
CHUNK = 128
EPS = 1e-6
N_MOD = 9
K_SCALE = M_QK ** -0.5
LOG2_E = 1.4426950408889634

LANES = 128
SUBLANES = 8
VMEM_LIMIT = 56 * 1024 * 1024

P_QK = 0
P_V = 2 * M_QK_W
P_O = P_V + M_V_W
P_HEAD = P_O + M_V_W
P_Z = P_HEAD
P_XBC = P_Z + S_INNER
P_B = P_XBC + S_INNER
P_C = P_B + S_BCW
P_GA = P_XBC + S_CONV_CH
P_GB = P_GA + D_MODEL
P_SMALL = P_GB + D_MODEL
PROJ_TN = 2304
PACK_TN = 512
P_WIDTH = 13824
SM_I = 0
SM_F = M_HEADS
SM_DT = 2 * M_HEADS

FFN_TM = 2048
FFN_TF = 256
PROJ_TM = 1024
MERGE_TM = 256
MLSTM_SHORT_SEQS = 4
SSD_SHORT_SEQS = 8
LONG_SEQS_PER_STEP = 1
CONV_SLAB = 256


def _sigmoid(x):
    return 0.5 * jnp.tanh(0.5 * x) + 0.5


def _silu(x):
    h = 0.5 * x
    return h * jnp.tanh(h) + h


def _rms(x, g):
    return x * lax.rsqrt(jnp.mean(x * x, axis=-1, keepdims=True) + EPS) * g


def _ada_kernel(c_ref, w_ref, b_ref, o_ref):
    s = _silu(c_ref[...]).astype(BF16)
    o_ref[...] = jnp.dot(s, w_ref[...].astype(BF16), preferred_element_type=F32) + b_ref[...]


def _ada(c_all, ada_w, ada_b):
    m = c_all.shape[0]
    n = ada_w.shape[1]
    tn = D_MODEL
    return pl.pallas_call(
        _ada_kernel,
        grid=(n // tn,),
        in_specs=[
            pl.BlockSpec((m, D_MODEL), lambda j: (0, 0)),
            pl.BlockSpec((D_MODEL, tn), lambda j: (0, j)),
            pl.BlockSpec((1, tn), lambda j: (0, j)),
        ],
        out_specs=pl.BlockSpec((m, tn), lambda j: (0, j)),
        out_shape=jax.ShapeDtypeStruct((m, n), F32),
        compiler_params=pltpu.CompilerParams(
            dimension_semantics=("arbitrary",), vmem_limit_bytes=VMEM_LIMIT),
        name="ada_mod",
    )(c_all, ada_w, ada_b.reshape(1, n))


def _ffn_kernel(x_ref, sh_ref, sc_ref, g_ref, nw_ref, w1_ref, w3_ref, w2_ref, *rest,
                nf, final):
    if final:
        fn_ref, o_ref, u_ref = rest
    else:
        o_ref, u_ref = rest
    f = pl.program_id(1)

    @pl.when(f == 0)
    def _():
        xn = _rms(x_ref[...], nw_ref[...])
        u_ref[...] = (xn * (1.0 + sc_ref[0]) + sh_ref[0]).astype(BF16)
        o_ref[...] = jnp.zeros_like(o_ref)

    u = u_ref[...]
    h1 = jnp.dot(u, w1_ref[...].astype(BF16), preferred_element_type=F32)
    h3 = jnp.dot(u, w3_ref[...].astype(BF16), preferred_element_type=F32)
    a = (_silu(h1) * h3).astype(BF16)
    o_ref[...] += jnp.dot(a, w2_ref[...].astype(BF16), preferred_element_type=F32)

    @pl.when(f == nf - 1)
    def _():
        y = x_ref[...] + (0.5 * g_ref[0]) * o_ref[...]
        if final:
            y = _rms(y, fn_ref[...])
        o_ref[...] = y


def _mod_spec(mod3, rows_per_group, tm, k):
    if mod3.shape[1] == 1:
        tiles_per_group = rows_per_group // tm
        return pl.BlockSpec((1, 1, D_MODEL), lambda i, *_: (i // tiles_per_group, 0, k))
    return pl.BlockSpec((1, tm, D_MODEL), lambda i, *_: (0, i, k))


def _ffn(x, mod3, rows_per_group, kmod, norm_w, w1, w3, w2, final_norm=None):
    n = x.shape[0]
    tm = min(FFN_TM, rows_per_group)
    tf = FFN_TF
    nf = D_FF // tf
    final = final_norm is not None
    in_specs = [
        pl.BlockSpec((tm, D_MODEL), lambda i, f: (i, 0)),
        _mod_spec(mod3, rows_per_group, tm, kmod),
        _mod_spec(mod3, rows_per_group, tm, kmod + 1),
        _mod_spec(mod3, rows_per_group, tm, kmod + 2),
        pl.BlockSpec((1, D_MODEL), lambda i, f: (0, 0)),
        pl.BlockSpec((D_MODEL, tf), lambda i, f: (0, f)),
        pl.BlockSpec((D_MODEL, tf), lambda i, f: (0, f)),
        pl.BlockSpec((tf, D_MODEL), lambda i, f: (f, 0)),
    ]
    args = [x, mod3, mod3, mod3, norm_w.reshape(1, D_MODEL), w1, w3, w2]
    if final:
        in_specs.append(pl.BlockSpec((1, D_MODEL), lambda i, f: (0, 0)))
        args.append(final_norm.reshape(1, D_MODEL))
    return pl.pallas_call(
        functools.partial(_ffn_kernel, nf=nf, final=final),
        grid=(n // tm, nf),
        in_specs=in_specs,
        out_specs=pl.BlockSpec((tm, D_MODEL), lambda i, f: (i, 0)),
        out_shape=jax.ShapeDtypeStruct((n, D_MODEL), F32),
        scratch_shapes=[pltpu.VMEM((tm, D_MODEL), BF16)],
        compiler_params=pltpu.CompilerParams(
            dimension_semantics=("parallel", "arbitrary"), vmem_limit_bytes=VMEM_LIMIT),
        name="ffn_final" if final else "ffn",
    )(*args)


PACK_SHIFT_Z = 2 * M_HEADS
PACK_SHIFT_GA = PACK_SHIFT_Z + S_HEADS
SRC_IF = P_HEAD
SRC_DT = P_HEAD + PACK_SHIFT_Z + S_INNER + S_CONV_CH


def _pack_kernel(a_ref, gif_ref, gdt_ref, o_ref):
    j = pl.program_id(0)

    @pl.when(j < P_SMALL // PACK_TN)
    def _():
        o_ref[...] = a_ref[...].astype(BF16)

    @pl.when(j == P_SMALL // PACK_TN)
    def _():
        pad = jnp.zeros((PACK_TN - SM_DT - S_HEADS, o_ref.shape[1]), F32)
        o_ref[...] = jnp.concatenate([gif_ref[...], gdt_ref[...], pad], axis=0).astype(BF16)


def _pack_w_in(w_t):
    d = w_t.shape[1]
    n_main = P_SMALL // PACK_TN
    last_start = w_t.shape[0] - PACK_TN

    def src_row(j):
        shift = jnp.where(j < P_HEAD // PACK_TN, 0,
                          jnp.where(j < P_GA // PACK_TN, PACK_SHIFT_Z, PACK_SHIFT_GA))
        tile = jnp.minimum((j * PACK_TN + shift) // SUBLANES, last_start // SUBLANES)
        return tile * SUBLANES

    return pl.pallas_call(
        _pack_kernel,
        grid=(n_main + 1,),
        in_specs=[
            pl.BlockSpec((pl.Element(PACK_TN), pl.Element(d)), lambda j: (src_row(j), 0)),
            pl.BlockSpec((pl.Element(SM_DT), pl.Element(d)), lambda j: (SRC_IF, 0)),
            pl.BlockSpec((pl.Element(S_HEADS), pl.Element(d)), lambda j: (SRC_DT, 0)),
        ],
        out_specs=pl.BlockSpec((PACK_TN, d), lambda j: (j, 0)),
        out_shape=jax.ShapeDtypeStruct((P_WIDTH, d), BF16),
        compiler_params=pltpu.CompilerParams(
            dimension_semantics=("arbitrary",), vmem_limit_bytes=VMEM_LIMIT),
        name="pack_w_in",
    )(w_t, w_t, w_t)


def _proj_kernel(x_ref, sh_ref, sc_ref, nw_ref, w_ref, o_ref, u_ref):
    @pl.when(pl.program_id(1) == 0)
    def _():
        xn = _rms(x_ref[...], nw_ref[...])
        u_ref[...] = (xn * (1.0 + sc_ref[0]) + sh_ref[0]).astype(BF16)

    o_ref[...] = _nt_dot(u_ref[...], w_ref[...])


def _proj(x, mod3, rows_per_group, kmod, norm_w, w_packed):
    n = x.shape[0]
    tm = min(PROJ_TM, rows_per_group)
    tn = PROJ_TN
    return pl.pallas_call(
        _proj_kernel,
        grid=(n // tm, P_WIDTH // tn),
        in_specs=[
            pl.BlockSpec((tm, D_MODEL), lambda i, j: (i, 0)),
            _mod_spec(mod3, rows_per_group, tm, kmod),
            _mod_spec(mod3, rows_per_group, tm, kmod + 1),
            pl.BlockSpec((1, D_MODEL), lambda i, j: (0, 0)),
            pl.BlockSpec((tn, D_MODEL), lambda i, j: (j, 0)),
        ],
        out_specs=pl.BlockSpec((tm, tn), lambda i, j: (i, j)),
        out_shape=jax.ShapeDtypeStruct((n, P_WIDTH), F32),
        scratch_shapes=[pltpu.VMEM((tm, D_MODEL), BF16)],
        compiler_params=pltpu.CompilerParams(
            dimension_semantics=("parallel", "arbitrary"), vmem_limit_bytes=VMEM_LIMIT),
        name="mix_proj",
    )(x, mod3, mod3, norm_w.reshape(1, D_MODEL), w_packed)


def _conv_silu(xp_ref, x_ref, w_ref, b_ref, L, dst_ref, slabs=None):
    if slabs is None:
        slabs = range(x_ref.shape[1] // CONV_SLAB)
    for s in slabs:
        cols = slice(s * CONV_SLAB, (s + 1) * CONV_SLAB)
        xp_ref[pl.ds(SUBLANES, L), cols] = x_ref[:, cols]
        xp = xp_ref[:, cols]
        acc = b_ref[:, cols] + xp[SUBLANES:] * w_ref[CONV_W - 1:CONV_W, cols]
        for back in range(1, CONV_W):
            shifted = pltpu.roll(xp, back, axis=0)[SUBLANES:]
            acc = acc + shifted * w_ref[CONV_W - 1 - back:CONV_W - back, cols]
        dst_ref[:, cols] = _silu(acc)
        xp_ref[pl.ds(0, SUBLANES), cols] = xp_ref[pl.ds(L, SUBLANES), cols]


def _init_conv(xp_ref, init_ref):
    xp_ref[pl.ds(0, SUBLANES), :] = jnp.zeros((SUBLANES, xp_ref.shape[1]), F32)
    if init_ref is not None:
        xp_ref[pl.ds(SUBLANES - (CONV_W - 1), CONV_W - 1), :] = init_ref[0]


def _run_stages(streams):
    live = list(streams)
    while live:
        for stream in list(live):
            try:
                next(stream)
            except StopIteration:
                live.remove(stream)


def _split3(a):
    hi = a.astype(BF16)
    r1 = a - hi.astype(F32)
    mid = r1.astype(BF16)
    lo = (r1 - mid.astype(F32)).astype(BF16)
    return hi, mid, lo


def _nt_dot(a, b):
    return lax.dot_general(a, b, (((1,), (1,)), ((), ())), preferred_element_type=F32)


def _tn_dot(a, b):
    return lax.dot_general(a, b, (((0,), (0,)), ((), ())), preferred_element_type=F32)


def _dot3(mask_b, a):
    return sum(jnp.dot(mask_b, p, preferred_element_type=F32) for p in _split3(a))


def _tn3(a, mask_b):
    return sum(_tn_dot(p, mask_b) for p in _split3(a))


def _log_sigmoid(x):
    return jnp.minimum(x, 0.0) - jnp.log1p(jnp.exp(-jnp.abs(x)))


def _softplus(x):
    return jnp.maximum(x, 0.0) + jnp.log1p(jnp.exp(-jnp.abs(x)))


def _mlstm_kernel(*refs, L, nseq, has_state):
    qk_ref, v_ref, og_ref, sm_ref, sb_ref, hg_ref, w_ref, b_ref = refs[:8]
    refs = refs[8:]
    if has_state:
        cv0_ref, c0_ref, n0_ref, m0_ref = refs[:4]
        refs = refs[4:]
    o_ref, c_ref, n_ref, m_ref, xp_ref, qk_s = refs

    @pl.when(pl.program_id(1) == 0)
    def _():
        if has_state:
            c_ref[...] = c0_ref[...]
            n_ref[...] = n0_ref[...]
            m_ref[...] = m0_ref[...]
        else:
            c_ref[...] = jnp.zeros_like(c_ref)
            n_ref[...] = jnp.zeros_like(n_ref)
            m_ref[...] = jnp.zeros_like(m_ref)
        for s in range(nseq):
            _init_conv(xp_ref.at[s], cv0_ref.at[pl.ds(s, 1)] if has_state else None)

    rr = lax.broadcasted_iota(jnp.int32, (L, L), 0)
    cc = lax.broadcasted_iota(jnp.int32, (L, L), 1)
    tril = cc <= rr
    gates = []
    for s in range(nseq):
        sm = sm_ref[s] + sb_ref[...]
        lf_all = _log_sigmoid(sm)
        b_all = _dot3(tril.astype(BF16), lf_all)
        b_all_t = _tn3(lf_all, (rr <= cc).astype(BF16))
        sm_t = _tn3(sm, (rr == cc).astype(BF16))
        gates.append((sm, sm_t, b_all, b_all_t))

    def head_stages(s, h):
        sm, sm_t, b_all, b_all_t = gates[s]
        qk_seq = qk_s.at[s]
        _conv_silu(xp_ref.at[s], qk_ref.at[s], w_ref, b_ref, L, qk_seq,
                   slabs=[h * M_QK // CONV_SLAB, (M_QK_W + h * M_QK) // CONV_SLAB])
        yield
        q = qk_seq[:, h * M_QK:(h + 1) * M_QK]
        k = qk_seq[:, M_QK_W + h * M_QK:M_QK_W + (h + 1) * M_QK] * K_SCALE
        v = v_ref[s, :, h * M_V:(h + 1) * M_V]
        i_col = sm[:, SM_I + h:SM_I + h + 1]
        i_row = sm_t[SM_I + h:SM_I + h + 1, :]
        b_col = b_all[:, SM_F + h:SM_F + h + 1]
        b_row = b_all_t[SM_F + h:SM_F + h + 1, :]

        c_prev = c_ref[s, h]
        n_prev = n_ref[s, h]
        m_prev = m_ref[s, h]

        dmat = jnp.where(tril, b_col - b_row + i_row, -jnp.inf)
        inter = b_col + m_prev
        mt = jnp.maximum(inter, jnp.max(dmat, axis=1, keepdims=True))
        w_tok = jnp.exp(dmat - mt)
        w_st = jnp.exp(inter - mt)
        yield

        qb = q.astype(BF16)
        kb = k.astype(BF16)
        vb = v.astype(BF16)
        sc = _nt_dot(qb, kb) * w_tok
        yield
        num = (jnp.dot(sc.astype(BF16), vb, preferred_element_type=F32)
               + w_st * jnp.dot(qb, c_prev.astype(BF16), preferred_element_type=F32))
        den = (jnp.sum(sc, axis=1, keepdims=True)
               + w_st * jnp.sum(q * n_prev, axis=1, keepdims=True))
        yield
        hh = num / jnp.maximum(jnp.abs(den), jnp.exp(-mt))

        m_new = mt[L - 1:L, :]
        b_last = b_col[L - 1:L, :]
        ws = jnp.exp(b_last - b_col + i_col - m_new)
        decay = jnp.exp(b_last + m_prev - m_new)
        kw = k * ws
        c_ref[s, h] = decay * c_prev + _tn_dot(kw.astype(BF16), vb)
        n_ref[s, h] = decay * n_prev + jnp.sum(kw, axis=0, keepdims=True)
        m_ref[s, h] = m_new
        yield

        mu = jnp.mean(hh, axis=-1, keepdims=True)
        hc = hh - mu
        yield
        var = jnp.mean(hc * hc, axis=-1, keepdims=True)
        hsl = slice(h * M_V, (h + 1) * M_V)
        o_ref[s, :, hsl] = (hc * lax.rsqrt(var + EPS) * hg_ref[:, hsl]
                            * _sigmoid(og_ref[s, :, hsl]))

    _run_stages([head_stages(s, h) for s in range(nseq) for h in range(M_HEADS)])


def _seqs_per_step(bsz, nc, short):
    want = short if nc == 1 else LONG_SEQS_PER_STEP
    return want if bsz % want == 0 else 1


def _mlstm(proj, bsz, T, m_conv_w, m_conv_b, small_bias, m_head_g, state=None):
    L = min(T, CHUNK)
    nc = T // L
    nseq = _seqs_per_step(bsz, nc, MLSTM_SHORT_SEQS)
    has_state = state is not None
    qkw = 2 * M_QK_W

    const = lambda b, c: (0, 0)
    per_b = lambda b, c: (b, 0, 0, 0)
    in_specs = [
        pl.BlockSpec((nseq, L, qkw), lambda b, c: (b, c, P_QK // qkw)),
        pl.BlockSpec((nseq, L, M_V_W), lambda b, c: (b, c, P_V // M_V_W)),
        pl.BlockSpec((nseq, L, M_V_W), lambda b, c: (b, c, P_O // M_V_W)),
        pl.BlockSpec((nseq, L, LANES), lambda b, c: (b, c, P_SMALL // LANES)),
        pl.BlockSpec((1, LANES), const),
        pl.BlockSpec((1, M_V_W), const),
        pl.BlockSpec((CONV_W, qkw), const),
        pl.BlockSpec((1, qkw), const),
    ]
    args = [proj, proj, proj, proj, small_bias, m_head_g.reshape(1, -1),
            m_conv_w, m_conv_b.reshape(1, -1)]
    scratch = [pltpu.VMEM((nseq, L + SUBLANES, qkw), F32), pltpu.VMEM((nseq, L, qkw), F32)]
    if has_state:
        conv0, c0, n0, m0 = state
        in_specs += [
            pl.BlockSpec((nseq, CONV_W - 1, qkw), lambda b, c: (b, 0, 0)),
            pl.BlockSpec((nseq, M_HEADS, M_QK, M_V), per_b),
            pl.BlockSpec((nseq, M_HEADS, 1, M_QK), per_b),
            pl.BlockSpec((nseq, M_HEADS, 1, 1), per_b),
        ]
        args += [conv0, c0, n0.reshape(bsz, M_HEADS, 1, M_QK), m0.reshape(bsz, M_HEADS, 1, 1)]
    out_shape = (
        jax.ShapeDtypeStruct((bsz, T, M_V_W), F32),
        jax.ShapeDtypeStruct((bsz, M_HEADS, M_QK, M_V), F32),
        jax.ShapeDtypeStruct((bsz, M_HEADS, 1, M_QK), F32),
        jax.ShapeDtypeStruct((bsz, M_HEADS, 1, 1), F32),
    )
    out_specs = (
        pl.BlockSpec((nseq, L, M_V_W), lambda b, c: (b, c, 0)),
        pl.BlockSpec((nseq, M_HEADS, M_QK, M_V), per_b),
        pl.BlockSpec((nseq, M_HEADS, 1, M_QK), per_b),
        pl.BlockSpec((nseq, M_HEADS, 1, 1), per_b),
    )
    out, c1, n1, m1 = pl.pallas_call(
        functools.partial(_mlstm_kernel, L=L, nseq=nseq, has_state=has_state),
        grid=(bsz // nseq, nc),
        in_specs=in_specs,
        out_specs=out_specs,
        out_shape=out_shape,
        scratch_shapes=scratch,
        compiler_params=pltpu.CompilerParams(
            dimension_semantics=("arbitrary", "arbitrary"), vmem_limit_bytes=VMEM_LIMIT),
        name="mlstm_state" if has_state else "mlstm",
    )(*args)
    return out, c1, n1.reshape(bsz, M_HEADS, M_QK), m1.reshape(bsz, M_HEADS)


def _ssd_kernel(*refs, L, nseq, has_state):
    n_shared = 13
    streams = []
    for sq in range(nseq):
        one = pl.ds(sq, 1)
        views = [r.at[sq] for r in refs[:5]] + list(refs[5:n_shared])
        rest = refs[n_shared:]
        if has_state:
            views += [r.at[one] for r in rest[:2]]
            rest = rest[2:]
        o_ref, h_ref = rest[:2]
        views += [o_ref.at[sq], h_ref.at[one]] + [r.at[sq] for r in rest[2:]]
        streams += _ssd_seq_streams(*views, L=L, has_state=has_state)
    _run_stages(streams)


def _ssd_seq_streams(*refs, L, has_state):
    (z_ref, xs_ref, bp_ref, cp_ref, sm_ref, sb_ref, al_ref, d_ref, ng_ref, e_ref, et_ref,
     w_ref, b_ref) = refs[:13]
    refs = refs[13:]
    if has_state:
        cv0_ref, h0_ref = refs[:2]
        refs = refs[2:]
    o_ref, h_ref, xx_ref, xb_ref, xc_ref, xs_s, bm_s, cm_s = refs

    @pl.when(pl.program_id(1) == 0)
    def _():
        if has_state:
            h_ref[...] = h0_ref[...]
        else:
            h_ref[...] = jnp.zeros_like(h_ref)

    x_cols = slice(0, S_INNER)
    b_cols = slice(S_INNER, S_INNER + S_BCW)
    c_cols = slice(S_INNER + S_BCW, S_CONV_CH)
    for xp_ref, cols in ((xx_ref, x_cols), (xb_ref, b_cols), (xc_ref, c_cols)):
        @pl.when(pl.program_id(1) == 0)
        def _():
            _init_conv(xp_ref, cv0_ref.at[:, :, cols] if has_state else None)

    _conv_silu(xb_ref, bp_ref, w_ref.at[:, b_cols], b_ref.at[:, b_cols], L, bm_s)
    _conv_silu(xc_ref, cp_ref, w_ref.at[:, c_cols], b_ref.at[:, c_cols], L, cm_s)

    dt_all = _softplus(sm_ref[...] + sb_ref[...])
    da_all = dt_all * (-jnp.exp(al_ref[...]))
    lane = lax.broadcasted_iota(jnp.int32, (L, LANES), 1)
    rr = lax.broadcasted_iota(jnp.int32, (L, L), 0)
    cc = lax.broadcasted_iota(jnp.int32, (L, L), 1)
    tril = cc <= rr
    lo = lane < S_HEADDIM
    cum_all = _dot3(tril.astype(BF16), da_all)
    cum_all_t = _tn3(da_all, (rr <= cc).astype(BF16))
    cum2, cum2_t = cum_all * LOG2_E, cum_all_t * LOG2_E
    cum_last = cum_all[L - 1:L, :]
    dec_all = dt_all * jnp.exp(cum_last - cum_all)
    ecum_all = jnp.exp(cum_all)
    mxu_spread = L >= LANES
    if mxu_spread:
        dt_src = _split3(dt_all)[:2]
        dec_src = _split3(dec_all)[:2]
        ecum_src = _split3(ecum_all)[:2]
        clast_src = _split3(jnp.broadcast_to(cum_all_t[:, L - 1:L], (LANES, LANES)))
    else:
        dt_src, dec_src, ecum_src = dt_all, dec_all, ecum_all
        lo_row = lax.broadcasted_iota(jnp.int32, (LANES, 1), 0) < S_HEADDIM

    def spread(src, pair):
        if mxu_spread:
            return sum(jnp.dot(p, e_ref[pair], preferred_element_type=F32) for p in src)
        idx = SM_DT + 2 * pair
        return jnp.where(lo, src[:, idx:idx + 1], src[:, idx + 1:idx + 2])

    def state_decay(pair):
        if mxu_spread:
            return jnp.exp(sum(jnp.dot(et_ref[pair], p, preferred_element_type=F32)
                               for p in clast_src))
        idx = SM_DT + 2 * pair
        return jnp.exp(jnp.where(lo_row, cum_last[:, idx:idx + 1], cum_last[:, idx + 1:idx + 2]))

    def group_stages(g):
        _conv_silu(xx_ref, xs_ref, w_ref.at[:, x_cols], b_ref.at[:, x_cols], L, xs_s,
                   slabs=range(g * S_GW // CONV_SLAB, (g + 1) * S_GW // CONV_SLAB))
        yield
        bmb = bm_s[:, g * S_STATE:(g + 1) * S_STATE].astype(BF16)
        cmb = cm_s[:, g * S_STATE:(g + 1) * S_STATE].astype(BF16)
        cb = _nt_dot(cmb, bmb)
        ssq = jnp.zeros((L, 1), F32)
        for p in range(S_HPG // 2):
            yield
            pair = g * (S_HPG // 2) + p
            psl = slice(g * S_GW + p * LANES, g * S_GW + (p + 1) * LANES)
            hsl = slice(p * LANES, (p + 1) * LANES)
            lmat = []
            for e in range(2):
                idx = SM_DT + 2 * pair + e
                lmat.append(jnp.exp2(jnp.where(
                    tril, cum2[:, idx:idx + 1] - cum2_t[idx:idx + 1, :], -jnp.inf)))
            yield
            xs = xs_s[:, psl]
            xdt = xs * spread(dt_src, pair)
            xdd = xs * spread(dec_src, pair)
            h_prev = h_ref[0, g, hsl, :]
            yield
            xdt_b = xdt.astype(BF16)
            y = (jnp.where(lo,
                           jnp.dot((cb * lmat[0]).astype(BF16), xdt_b,
                                   preferred_element_type=F32),
                           jnp.dot((cb * lmat[1]).astype(BF16), xdt_b,
                                   preferred_element_type=F32))
                 + spread(ecum_src, pair) * _nt_dot(cmb, h_prev.astype(BF16)))
            h_ref[0, g, hsl, :] = state_decay(pair) * h_prev + _tn_dot(xdd.astype(BF16), bmb)
            yield
            gated = (y + d_ref[:, psl] * xs) * _silu(z_ref[:, psl])
            ssq = ssq + jnp.sum(gated * gated, axis=1, keepdims=True)
            o_ref[:, psl] = gated
        yield
        gsl = slice(g * S_GW, (g + 1) * S_GW)
        o_ref[:, gsl] = o_ref[:, gsl] * lax.rsqrt(ssq * (1.0 / S_GW) + EPS) * ng_ref[:, gsl]

    return [group_stages(g) for g in range(S_GROUPS)]


def _pair_onehots():
    n_pairs = S_HEADS // 2
    pair = jnp.arange(n_pairs)[:, None, None]
    src = jnp.arange(LANES)[None, :, None]
    dst = jnp.arange(LANES)[None, None, :]
    onehot = src == SM_DT + 2 * pair + dst // S_HEADDIM
    return onehot.astype(BF16), jnp.swapaxes(onehot, 1, 2).astype(BF16)


def _ssd(proj, bsz, T, s_conv_w, s_conv_b, small_bias, alog_row, d_row, s_norm_g, state=None):
    L = min(T, CHUNK)
    nc = T // L
    nseq = _seqs_per_step(bsz, nc, SSD_SHORT_SEQS)
    has_state = state is not None
    n_pairs = S_HEADS // 2
    onehot, onehot_t = _pair_onehots()

    const = lambda b, c: (0, 0)
    const3 = lambda b, c: (0, 0, 0)
    in_specs = [
        pl.BlockSpec((nseq, L, S_INNER), lambda b, c: (b, c, P_Z // S_INNER)),
        pl.BlockSpec((nseq, L, S_INNER), lambda b, c: (b, c, P_XBC // S_INNER)),
        pl.BlockSpec((nseq, L, S_BCW), lambda b, c: (b, c, P_B // S_BCW)),
        pl.BlockSpec((nseq, L, S_BCW), lambda b, c: (b, c, P_C // S_BCW)),
        pl.BlockSpec((nseq, L, LANES), lambda b, c: (b, c, P_SMALL // LANES)),
        pl.BlockSpec((1, LANES), const),
        pl.BlockSpec((1, LANES), const),
        pl.BlockSpec((1, S_INNER), const),
        pl.BlockSpec((1, S_INNER), const),
        pl.BlockSpec((n_pairs, LANES, LANES), const3),
        pl.BlockSpec((n_pairs, LANES, LANES), const3),
        pl.BlockSpec((CONV_W, S_CONV_CH), const),
        pl.BlockSpec((1, S_CONV_CH), const),
    ]
    args = [proj, proj, proj, proj, proj, small_bias, alog_row, d_row, s_norm_g.reshape(1, -1),
            onehot, onehot_t, s_conv_w, s_conv_b.reshape(1, -1)]
    scratch = [pltpu.VMEM((nseq, L + SUBLANES, S_INNER), F32),
               pltpu.VMEM((nseq, L + SUBLANES, S_BCW), F32),
               pltpu.VMEM((nseq, L + SUBLANES, S_BCW), F32),
               pltpu.VMEM((nseq, L, S_INNER), F32),
               pltpu.VMEM((nseq, L, S_BCW), F32),
               pltpu.VMEM((nseq, L, S_BCW), F32)]
    if has_state:
        conv0, h0 = state
        in_specs += [
            pl.BlockSpec((nseq, CONV_W - 1, S_CONV_CH), lambda b, c: (b, 0, 0)),
            pl.BlockSpec((nseq, S_GROUPS, S_GW, S_STATE), lambda b, c: (b, 0, 0, 0)),
        ]
        args += [conv0, h0.reshape(bsz, S_GROUPS, S_GW, S_STATE)]
    out, h1 = pl.pallas_call(
        functools.partial(_ssd_kernel, L=L, nseq=nseq, has_state=has_state),
        grid=(bsz // nseq, nc),
        in_specs=in_specs,
        out_specs=(
            pl.BlockSpec((nseq, L, S_INNER), lambda b, c: (b, c, 0)),
            pl.BlockSpec((nseq, S_GROUPS, S_GW, S_STATE), lambda b, c: (b, 0, 0, 0)),
        ),
        out_shape=(
            jax.ShapeDtypeStruct((bsz, T, S_INNER), F32),
            jax.ShapeDtypeStruct((bsz, S_GROUPS, S_GW, S_STATE), F32),
        ),
        scratch_shapes=scratch,
        compiler_params=pltpu.CompilerParams(
            dimension_semantics=("arbitrary", "arbitrary"), vmem_limit_bytes=VMEM_LIMIT),
        name="ssd_state" if has_state else "ssd",
    )(*args)
    return out, h1.reshape(bsz, S_HEADS, S_HEADDIM, S_STATE)


def _merge_kernel(x_ref, hm_ref, gt_ref, ga_ref, gb_ref, g_ref, wpa_ref, wpb_ref, wo_ref, o_ref,
                  wpa_s, wpb_s, wo_s):
    @pl.when(pl.program_id(0) == 0)
    def _():
        wpa_s[...] = wpa_ref[...].astype(BF16)
        wpb_s[...] = wpb_ref[...].astype(BF16)
        wo_s[...] = wo_ref[...].astype(BF16)

    ya = jnp.dot(hm_ref[...].astype(BF16), wpa_s[...], preferred_element_type=F32)
    yb = jnp.dot(gt_ref[...].astype(BF16), wpb_s[...], preferred_element_type=F32)
    merged = _sigmoid(ga_ref[...]) * ya + _sigmoid(gb_ref[...]) * yb
    o_ref[...] = x_ref[...] + g_ref[0] * jnp.dot(
        merged.astype(BF16), wo_s[...], preferred_element_type=F32)


def _merge(x, hm, gated, proj, mod3, rows_per_group, kmod, w_pa, w_pb, w_out):
    n = x.shape[0]
    tm = MERGE_TM
    const = lambda i: (0, 0)
    resident = functools.partial(pl.BlockSpec, index_map=const, pipeline_mode=pl.Buffered(1))
    return pl.pallas_call(
        _merge_kernel,
        grid=(n // tm,),
        in_specs=[
            pl.BlockSpec((tm, D_MODEL), lambda i: (i, 0)),
            pl.BlockSpec((tm, M_V_W), lambda i: (i, 0)),
            pl.BlockSpec((tm, S_INNER), lambda i: (i, 0)),
            pl.BlockSpec((tm, D_MODEL), lambda i: (i, P_GA // D_MODEL)),
            pl.BlockSpec((tm, D_MODEL), lambda i: (i, P_GB // D_MODEL)),
            _mod_spec(mod3, rows_per_group, tm, kmod),
            resident((M_V_W, D_MODEL)),
            resident((S_INNER, D_MODEL)),
            resident((D_MODEL, D_MODEL)),
        ],
        out_specs=pl.BlockSpec((tm, D_MODEL), lambda i: (i, 0)),
        out_shape=jax.ShapeDtypeStruct((n, D_MODEL), F32),
        scratch_shapes=[pltpu.VMEM((M_V_W, D_MODEL), BF16),
                        pltpu.VMEM((S_INNER, D_MODEL), BF16),
                        pltpu.VMEM((D_MODEL, D_MODEL), BF16)],
        compiler_params=pltpu.CompilerParams(
            dimension_semantics=("arbitrary",), vmem_limit_bytes=VMEM_LIMIT),
        name="merge",
    )(x, hm, gated, proj, proj, mod3, w_pa, w_pb, w_out)


def _layer(x, mod3, rows_per_group, bsz, T, wts, final_norm, state):
    n = bsz * T
    x = x.reshape(n, D_MODEL)
    x = _ffn(x, mod3, rows_per_group, 0, wts["norm_ff1"], wts["ff1_w1"], wts["ff1_w3"],
             wts["ff1_w2"])
    proj = _proj(x, mod3, rows_per_group, 3, wts["norm_mix"], wts["w_packed"])
    proj3 = proj.reshape(bsz, T, P_WIDTH)
    mixer_args = (proj3, bsz, T)
    m_args = (wts["m_conv_w"], wts["m_conv_b"], wts["small_bias"], wts["m_head_g"])
    s_args = (wts["s_conv_w"], wts["s_conv_b"], wts["small_bias"], wts["alog_row"],
              wts["d_row"], wts["s_norm_g"])
    if state is None:
        m_state = s_state = None
    else:
        c0, n0, m0, mconv0, ssm0, sconv0 = state
        m_state = (mconv0, c0, n0, m0)
        s_state = (sconv0, ssm0)
    hm, c1, n1, m1 = _mlstm(*mixer_args, *m_args, state=m_state)
    gated, h1 = _ssd(*mixer_args, *s_args, state=s_state)
    x = _merge(x, hm.reshape(n, M_V_W), gated.reshape(n, S_INNER), proj, mod3, rows_per_group,
               5, wts["w_pa"], wts["w_pb"], wts["w_out"])
    y = _ffn(x, mod3, rows_per_group, 6, wts["norm_ff2"], wts["ff2_w1"], wts["ff2_w3"],
             wts["ff2_w2"], final_norm=final_norm)
    last = proj3[:, T - (CONV_W - 1):]
    mconv1 = last[:, :, P_QK:P_QK + 2 * M_QK_W]
    sconv1 = last[:, :, P_XBC:P_XBC + S_CONV_CH]
    return y.reshape(bsz, T, D_MODEL), (c1, n1, m1, mconv1, h1, sconv1)


def kernel(x_prompt, x_sample, c_prompt, c_sample, state_mlstm_C, state_mlstm_n, state_mlstm_m, state_mlstm_conv, state_ssm, state_ssm_conv, ada_w, ada_b, norm_ff1, ff1_w1, ff1_w3, ff1_w2, norm_mix, w_in, m_conv_w, m_conv_b, m_if_b, m_head_g, w_pa, s_conv_w, s_conv_b, s_dt_bias, s_A_log, s_D, s_norm_g, w_pb, w_out, norm_ff2, ff2_w1, ff2_w3, ff2_w2, final_norm):
    bp, tp, _ = x_prompt.shape
    bs, ts, _ = x_sample.shape
    depth = ada_w.shape[0]
    assert depth == 1

    hp, hs = x_prompt, x_sample
    new_p, new_s = [], []
    for l in range(depth):
        c_all = jnp.concatenate([c_prompt, c_sample], axis=0)
        pad_rows = (-c_all.shape[0]) % (2 * SUBLANES)
        c_all = jnp.pad(c_all, ((0, pad_rows), (0, 0)))
        mod = _ada(c_all, ada_w[l], ada_b[l])
        mod_p = mod[:bp].reshape(bp, 1, N_MOD * D_MODEL)
        mod_s = jnp.repeat(mod[bp:bp + bs], ts, axis=0).reshape(1, bs * ts, N_MOD * D_MODEL)

        lane_pad = LANES - (2 * M_HEADS + S_HEADS)
        small_bias = jnp.concatenate(
            [m_if_b[l], s_dt_bias[l], jnp.zeros((lane_pad,), F32)]).reshape(1, LANES)
        alog_row = jnp.concatenate(
            [jnp.zeros((2 * M_HEADS,), F32), s_A_log[l], jnp.zeros((lane_pad,), F32)]
        ).reshape(1, LANES)
        wts = {
            "norm_ff1": norm_ff1[l], "ff1_w1": ff1_w1[l], "ff1_w3": ff1_w3[l],
            "ff1_w2": ff1_w2[l],
            "norm_mix": norm_mix[l], "w_packed": _pack_w_in(jnp.swapaxes(w_in, 1, 2)[l]),
            "m_conv_w": m_conv_w[l], "m_conv_b": m_conv_b[l], "small_bias": small_bias,
            "m_head_g": m_head_g[l], "w_pa": w_pa[l],
            "s_conv_w": s_conv_w[l], "s_conv_b": s_conv_b[l], "alog_row": alog_row,
            "d_row": jnp.repeat(s_D[l], S_HEADDIM).reshape(1, S_INNER),
            "s_norm_g": s_norm_g[l], "w_pb": w_pb[l], "w_out": w_out[l],
            "norm_ff2": norm_ff2[l], "ff2_w1": ff2_w1[l], "ff2_w3": ff2_w3[l],
            "ff2_w2": ff2_w2[l],
        }
        st_s = (state_mlstm_C[l], state_mlstm_n[l], state_mlstm_m[l], state_mlstm_conv[l],
                state_ssm[l], state_ssm_conv[l])
        hp, sp = _layer(hp, mod_p, tp, bp, tp, wts, final_norm, None)
        hs, ss = _layer(hs, mod_s, bs * ts, bs, ts, wts, final_norm, st_s)
        new_p.append(sp)
        new_s.append(ss)
    outs_p = [jnp.stack([s[i] for s in new_p]) for i in range(6)]
    outs_s = [jnp.stack([s[i] for s in new_s]) for i in range(6)]
    return (hp, hs, *outs_p, *outs_s)
```

```python
import functools

import jax
import jax.numpy as jnp
from jax import lax
from jax.experimental import pallas as pl
from jax.experimental.pallas import tpu as pltpu

F32 = jnp.float32
BF16 = jnp.bfloat16

D_MODEL = 1024
M_HEADS = 4
M_QK = 256
M_V = 512
M_QK_W = M_HEADS * M_QK
M_V_W = M_HEADS * M_V
S_INNER = 2048
S_HEADDIM = 64
S_HEADS = 32
S_GROUPS = 4
S_HPG = S_HEADS // S_GROUPS
S_STATE = 128
S_GW = S_INNER // S_GROUPS
S_BCW = S_GROUPS * S_STATE
S_CONV_CH = S_INNER + 2 * S_BCW
CONV_W = 4
D_FF = 2816
CHUNK = 128
EPS = 1e-6
N_MOD = 9
K_SCALE = M_QK ** -0.5
LOG2_E = 1.4426950408889634

LANES = 128
SUBLANES = 8
VMEM_LIMIT = 56 * 1024 * 1024

P_QK = 0
P_V = 2 * M_QK_W
P_O = P_V + M_V_W
P_HEAD = P_O + M_V_W
P_Z = P_HEAD
P_XBC = P_Z + S_INNER
P_B = P_XBC + S_INNER
P_C = P_B + S_BCW
P_GA = P_XBC + S_CONV_CH
P_GB = P_GA + D_MODEL
P_SMALL = P_GB + D_MODEL
PROJ_TN = 2304
PACK_TN = 512
P_WIDTH = 13824
SM_I = 0
SM_F = M_HEADS
SM_DT = 2 * M_HEADS

FFN_TM = 2048
FFN_TF = 256
PROJ_TM = 1024
MERGE_TM = 256
MLSTM_SHORT_SEQS = 4
SSD_SHORT_SEQS = 8
LONG_SEQS_PER_STEP = 1
CONV_STRIDE = 4
CONV_ROWS = SUBLANES * CONV_STRIDE


def _sigmoid(x):
    return 0.5 * jnp.tanh(0.5 * x) + 0.5


def _silu(x):
    h = 0.5 * x
    return h * jnp.tanh(h) + h


def _rms(x, g):
    return x * lax.rsqrt(jnp.mean(x * x, axis=-1, keepdims=True) + EPS) * g


def _ada_kernel(c_ref, w_ref, b_ref, o_ref):
    s = _silu(c_ref[...]).astype(BF16)
    o_ref[...] = jnp.dot(s, w_ref[...].astype(BF16), preferred_element_type=F32) + b_ref[...]


def _ada(c_all, ada_w, ada_b):
    m = c_all.shape[0]
    n = ada_w.shape[1]
    tn = D_MODEL
    return pl.pallas_call(
        _ada_kernel,
        grid=(n // tn,),
        in_specs=[
            pl.BlockSpec((m, D_MODEL), lambda j: (0, 0)),
            pl.BlockSpec((D_MODEL, tn), lambda j: (0, j)),
            pl.BlockSpec((1, tn), lambda j: (0, j)),
        ],
        out_specs=pl.BlockSpec((m, tn), lambda j: (0, j)),
        out_shape=jax.ShapeDtypeStruct((m, n), F32),
        compiler_params=pltpu.CompilerParams(
            dimension_semantics=("arbitrary",), vmem_limit_bytes=VMEM_LIMIT),
        name="ada_mod",
    )(c_all, ada_w, ada_b.reshape(1, n))


def _ffn_kernel(x_ref, sh_ref, sc_ref, g_ref, nw_ref, w1_ref, w3_ref, w2_ref, *rest,
                nf, final):
    if final:
        fn_ref, o_ref, u_ref = rest
    else:
        o_ref, u_ref = rest
    f = pl.program_id(1)

    @pl.when(f == 0)
    def _():
        xn = _rms(x_ref[...], nw_ref[...])
        u_ref[...] = (xn * (1.0 + sc_ref[0]) + sh_ref[0]).astype(BF16)
        o_ref[...] = jnp.zeros_like(o_ref)

    u = u_ref[...]
    h1 = jnp.dot(u, w1_ref[...].astype(BF16), preferred_element_type=F32)
    h3 = jnp.dot(u, w3_ref[...].astype(BF16), preferred_element_type=F32)
    a = (_silu(h1) * h3).astype(BF16)
    o_ref[...] += jnp.dot(a, w2_ref[...].astype(BF16), preferred_element_type=F32)

    @pl.when(f == nf - 1)
    def _():
        y = x_ref[...] + (0.5 * g_ref[0]) * o_ref[...]
        if final:
            y = _rms(y, fn_ref[...])
        o_ref[...] = y


def _mod_spec(mod3, rows_per_group, tm, k):
    if mod3.shape[1] == 1:
        tiles_per_group = rows_per_group // tm
        return pl.BlockSpec((1, 1, D_MODEL), lambda i, *_: (i // tiles_per_group, 0, k))
    return pl.BlockSpec((1, tm, D_MODEL), lambda i, *_: (0, i, k))


def _ffn(x, mod3, rows_per_group, kmod, norm_w, w1, w3, w2, final_norm=None):
    n = x.shape[0]
    tm = min(FFN_TM, rows_per_group)
    tf = FFN_TF
    nf = D_FF // tf
    final = final_norm is not None
    in_specs = [
        pl.BlockSpec((tm, D_MODEL), lambda i, f: (i, 0)),
        _mod_spec(mod3, rows_per_group, tm, kmod),
        _mod_spec(mod3, rows_per_group, tm, kmod + 1),
        _mod_spec(mod3, rows_per_group, tm, kmod + 2),
        pl.BlockSpec((1, D_MODEL), lambda i, f: (0, 0)),
        pl.BlockSpec((D_MODEL, tf), lambda i, f: (0, f)),
        pl.BlockSpec((D_MODEL, tf), lambda i, f: (0, f)),
        pl.BlockSpec((tf, D_MODEL), lambda i, f: (f, 0)),
    ]
    args = [x, mod3, mod3, mod3, norm_w.reshape(1, D_MODEL), w1, w3, w2]
    if final:
        in_specs.append(pl.BlockSpec((1, D_MODEL), lambda i, f: (0, 0)))
        args.append(final_norm.reshape(1, D_MODEL))
    return pl.pallas_call(
        functools.partial(_ffn_kernel, nf=nf, final=final),
        grid=(n // tm, nf),
        in_specs=in_specs,
        out_specs=pl.BlockSpec((tm, D_MODEL), lambda i, f: (i, 0)),
        out_shape=jax.ShapeDtypeStruct((n, D_MODEL), F32),
        scratch_shapes=[pltpu.VMEM((tm, D_MODEL), BF16)],
        compiler_params=pltpu.CompilerParams(
            dimension_semantics=("parallel", "arbitrary"), vmem_limit_bytes=VMEM_LIMIT),
        name="ffn_final" if final else "ffn",
    )(*args)


PACK_SHIFT_Z = 2 * M_HEADS
PACK_SHIFT_GA = PACK_SHIFT_Z + S_HEADS
SRC_IF = P_HEAD
SRC_DT = P_HEAD + PACK_SHIFT_Z + S_INNER + S_CONV_CH


def _pack_kernel(a_ref, gif_ref, gdt_ref, o_ref):
    j = pl.program_id(0)

    @pl.when(j < P_SMALL // PACK_TN)
    def _():
        o_ref[...] = a_ref[...].astype(BF16)

    @pl.when(j == P_SMALL // PACK_TN)
    def _():
        pad = jnp.zeros((PACK_TN - SM_DT - S_HEADS, o_ref.shape[1]), F32)
        o_ref[...] = jnp.concatenate([gif_ref[...], gdt_ref[...], pad], axis=0).astype(BF16)


def _pack_w_in(w_t):
    d = w_t.shape[1]
    n_main = P_SMALL // PACK_TN
    last_start = w_t.shape[0] - PACK_TN

    def src_row(j):
        shift = jnp.where(j < P_HEAD // PACK_TN, 0,
                          jnp.where(j < P_GA // PACK_TN, PACK_SHIFT_Z, PACK_SHIFT_GA))
        tile = jnp.minimum((j * PACK_TN + shift) // SUBLANES, last_start // SUBLANES)
        return tile * SUBLANES

    return pl.pallas_call(
        _pack_kernel,
        grid=(n_main + 1,),
        in_specs=[
            pl.BlockSpec((pl.Element(PACK_TN), pl.Element(d)), lambda j: (src_row(j), 0)),
            pl.BlockSpec((pl.Element(SM_DT), pl.Element(d)), lambda j: (SRC_IF, 0)),
            pl.BlockSpec((pl.Element(S_HEADS), pl.Element(d)), lambda j: (SRC_DT, 0)),
        ],
        out_specs=pl.BlockSpec((PACK_TN, d), lambda j: (j, 0)),
        out_shape=jax.ShapeDtypeStruct((P_WIDTH, d), BF16),
        compiler_params=pltpu.CompilerParams(
            dimension_semantics=("arbitrary",), vmem_limit_bytes=VMEM_LIMIT),
        name="pack_w_in",
    )(w_t, w_t, w_t)


def _proj_kernel(x_ref, sh_ref, sc_ref, nw_ref, w_ref, o_ref, u_ref):
    @pl.when(pl.program_id(1) == 0)
    def _():
        xn = _rms(x_ref[...], nw_ref[...])
        u_ref[...] = (xn * (1.0 + sc_ref[0]) + sh_ref[0]).astype(BF16)

    o_ref[...] = _nt_dot(u_ref[...], w_ref[...])


def _proj(x, mod3, rows_per_group, kmod, norm_w, w_packed):
    n = x.shape[0]
    tm = min(PROJ_TM, rows_per_group)
    tn = PROJ_TN
    return pl.pallas_call(
        _proj_kernel,
        grid=(n // tm, P_WIDTH // tn),
        in_specs=[
            pl.BlockSpec((tm, D_MODEL), lambda i, j: (i, 0)),
            _mod_spec(mod3, rows_per_group, tm, kmod),
            _mod_spec(mod3, rows_per_group, tm, kmod + 1),
            pl.BlockSpec((1, D_MODEL), lambda i, j: (0, 0)),
            pl.BlockSpec((tn, D_MODEL), lambda i, j: (j, 0)),
        ],
        out_specs=pl.BlockSpec((tm, tn), lambda i, j: (i, j)),
        out_shape=jax.ShapeDtypeStruct((n, P_WIDTH), F32),
        scratch_shapes=[pltpu.VMEM((tm, D_MODEL), BF16)],
        compiler_params=pltpu.CompilerParams(
            dimension_semantics=("parallel", "arbitrary"), vmem_limit_bytes=VMEM_LIMIT),
        name="mix_proj",
    )(x, mod3, mod3, norm_w.reshape(1, D_MODEL), w_packed)


def _conv_silu(xp_ref, x_ref, w_ref, b_ref, L, dst_ref, slabs=None):
    if slabs is None:
        slabs = range(x_ref.shape[1] // LANES)
    for s in slabs:
        cols = slice(s * LANES, (s + 1) * LANES)
        xp, dst = xp_ref.at[s], dst_ref.at[s]
        xp[pl.ds(SUBLANES, L), :] = x_ref[:, cols]
        bias = b_ref[:, cols]
        taps = [w_ref[t:t + 1, cols] for t in range(CONV_W)]
        if L % CONV_ROWS == 0:
            for r in range(0, L, CONV_ROWS):
                for b in range(CONV_STRIDE):
                    acc = bias
                    for back in range(CONV_W):
                        src = pl.ds(SUBLANES + r + b - back, SUBLANES, stride=CONV_STRIDE)
                        acc = acc + xp[src, :] * taps[CONV_W - 1 - back]
                    dst[pl.ds(r + b, SUBLANES, stride=CONV_STRIDE), :] = _silu(acc)
        else:
            x_all = xp[...]
            acc = bias + x_all[SUBLANES:] * taps[CONV_W - 1]
            for back in range(1, CONV_W):
                acc = acc + pltpu.roll(x_all, back, axis=0)[SUBLANES:] * taps[CONV_W - 1 - back]
            dst[...] = _silu(acc)
        xp[pl.ds(0, SUBLANES), :] = xp[pl.ds(L, SUBLANES), :]


def _init_conv(xp_ref, init_ref):
    n_slabs = xp_ref.shape[0]
    xp_ref[:, pl.ds(0, SUBLANES), :] = jnp.zeros((n_slabs, SUBLANES, LANES), F32)
    if init_ref is not None:
        for s in range(n_slabs):
            xp_ref[s, pl.ds(SUBLANES - (CONV_W - 1), CONV_W - 1), :] = (
                init_ref[0, :, s * LANES:(s + 1) * LANES])


def _run_stages(streams):
    live = list(streams)
    while live:
        for stream in list(live):
            try:
                next(stream)
            except StopIteration:
                live.remove(stream)


def _split3(a):
    hi = a.astype(BF16)
    r1 = a - hi.astype(F32)
    mid = r1.astype(BF16)
    lo = (r1 - mid.astype(F32)).astype(BF16)
    return hi, mid, lo


def _nt_dot(a, b):
    return lax.dot_general(a, b, (((1,), (1,)), ((), ())), preferred_element_type=F32)


def _tn_dot(a, b):
    return lax.dot_general(a, b, (((0,), (0,)), ((), ())), preferred_element_type=F32)


def _dot3(mask_b, a):
    return sum(jnp.dot(mask_b, p, preferred_element_type=F32) for p in _split3(a))


def _tn3(a, mask_b):
    return sum(_tn_dot(p, mask_b) for p in _split3(a))


def _log_sigmoid(x):
    return jnp.minimum(x, 0.0) - jnp.log1p(jnp.exp(-jnp.abs(x)))


def _softplus(x):
    return jnp.maximum(x, 0.0) + jnp.log1p(jnp.exp(-jnp.abs(x)))


def _mlstm_kernel(*refs, L, nseq, has_state):
    qk_ref, v_ref, og_ref, sm_ref, sb_ref, hg_ref, w_ref, b_ref = refs[:8]
    refs = refs[8:]
    if has_state:
        cv0_ref, c0_ref, n0_ref, m0_ref = refs[:4]
        refs = refs[4:]
    o_ref, c_ref, n_ref, m_ref, xp_ref, qk_s = refs

    @pl.when(pl.program_id(1) == 0)
    def _():
        if has_state:
            c_ref[...] = c0_ref[...]
            n_ref[...] = n0_ref[...]
            m_ref[...] = m0_ref[...]
        else:
            c_ref[...] = jnp.zeros_like(c_ref)
            n_ref[...] = jnp.zeros_like(n_ref)
            m_ref[...] = jnp.zeros_like(m_ref)
        for s in range(nseq):
            _init_conv(xp_ref.at[s], cv0_ref.at[pl.ds(s, 1)] if has_state else None)

    rr = lax.broadcasted_iota(jnp.int32, (L, L), 0)
    cc = lax.broadcasted_iota(jnp.int32, (L, L), 1)
    tril = cc <= rr
    gates = []
    for s in range(nseq):
        sm = sm_ref[s] + sb_ref[...]
        lf_all = _log_sigmoid(sm)
        b_all = _dot3(tril.astype(BF16), lf_all)
        b_all_t = _tn3(lf_all, (rr <= cc).astype(BF16))
        sm_t = _tn3(sm, (rr == cc).astype(BF16))
        gates.append((sm, sm_t, b_all, b_all_t))

    def head_stages(s, h):
        sm, sm_t, b_all, b_all_t = gates[s]
        qk_seq = qk_s.at[s]
        q_slabs = range(h * M_QK // LANES, (h + 1) * M_QK // LANES)
        k_slabs = range((M_QK_W + h * M_QK) // LANES, (M_QK_W + (h + 1) * M_QK) // LANES)
        _conv_silu(xp_ref.at[s], qk_ref.at[s], w_ref, b_ref, L, qk_seq,
                   slabs=[*q_slabs, *k_slabs])
        yield
        q = jnp.concatenate([qk_seq[i] for i in q_slabs], axis=1)
        k = jnp.concatenate([qk_seq[i] for i in k_slabs], axis=1) * K_SCALE
        v = v_ref[s, :, h * M_V:(h + 1) * M_V]
        i_col = sm[:, SM_I + h:SM_I + h + 1]
        i_row = sm_t[SM_I + h:SM_I + h + 1, :]
        b_col = b_all[:, SM_F + h:SM_F + h + 1]
        b_row = b_all_t[SM_F + h:SM_F + h + 1, :]

        c_prev = c_ref[s, h]
        n_prev = n_ref[s, h]
        m_prev = m_ref[s, h]

        dmat = jnp.where(tril, b_col - b_row + i_row, -jnp.inf)
        inter = b_col + m_prev
        mt = jnp.maximum(inter, jnp.max(dmat, axis=1, keepdims=True))
        w_tok = jnp.exp(dmat - mt)
        w_st = jnp.exp(inter - mt)
        yield

        qb = q.astype(BF16)
        kb = k.astype(BF16)
        vb = v.astype(BF16)
        sc = _nt_dot(qb, kb) * w_tok
        yield
        num = (jnp.dot(sc.astype(BF16), vb, preferred_element_type=F32)
               + w_st * jnp.dot(qb, c_prev.astype(BF16), preferred_element_type=F32))
        den = (jnp.sum(sc, axis=1, keepdims=True)
               + w_st * jnp.sum(q * n_prev, axis=1, keepdims=True))
        yield
        hh = num / jnp.maximum(jnp.abs(den), jnp.exp(-mt))

        m_new = mt[L - 1:L, :]
        b_last = b_col[L - 1:L, :]
        ws = jnp.exp(b_last - b_col + i_col - m_new)
        decay = jnp.exp(b_last + m_prev - m_new)
        kw = k * ws
        c_ref[s, h] = decay * c_prev + _tn_dot(kw.astype(BF16), vb)
        n_ref[s, h] = decay * n_prev + jnp.sum(kw, axis=0, keepdims=True)
        m_ref[s, h] = m_new
        yield

        mu = jnp.mean(hh, axis=-1, keepdims=True)
        hc = hh - mu
        yield
        var = jnp.mean(hc * hc, axis=-1, keepdims=True)
        hsl = slice(h * M_V, (h + 1) * M_V)
        gate = (0.5 * hg_ref[:, hsl]) * (jnp.tanh(0.5 * og_ref[s, :, hsl]) + 1.0)
        o_ref[s, :, hsl] = (hc * lax.rsqrt(var + EPS)) * gate

    _run_stages([head_stages(s, h) for s in range(nseq) for h in range(M_HEADS)])


def _seqs_per_step(bsz, nc, short):
    want = short if nc == 1 else LONG_SEQS_PER_STEP
    return want if bsz % want == 0 else 1


def _mlstm(proj, bsz, T, m_conv_w, m_conv_b, small_bias, m_head_g, state=None):
    L = min(T, CHUNK)
    nc = T // L
    nseq = _seqs_per_step(bsz, nc, MLSTM_SHORT_SEQS)
    has_state = state is not None
    qkw = 2 * M_QK_W

    const = lambda b, c: (0, 0)
    per_b = lambda b, c: (b, 0, 0, 0)
    in_specs = [
        pl.BlockSpec((nseq, L, qkw), lambda b, c: (b, c, P_QK // qkw)),
        pl.BlockSpec((nseq, L, M_V_W), lambda b, c: (b, c, P_V // M_V_W)),
        pl.BlockSpec((nseq, L, M_V_W), lambda b, c: (b, c, P_O // M_V_W)),
        pl.BlockSpec((nseq, L, LANES), lambda b, c: (b, c, P_SMALL // LANES)),
        pl.BlockSpec((1, LANES), const),
        pl.BlockSpec((1, M_V_W), const),
        pl.BlockSpec((CONV_W, qkw), const),
        pl.BlockSpec((1, qkw), const),
    ]
    args = [proj, proj, proj, proj, small_bias, m_head_g.reshape(1, -1),
            m_conv_w, m_conv_b.reshape(1, -1)]
    scratch = [pltpu.VMEM((nseq, qkw // LANES, L + SUBLANES, LANES), F32),
               pltpu.VMEM((nseq, qkw // LANES, L, LANES), F32)]
    if has_state:
        conv0, c0, n0, m0 = state
        in_specs += [
            pl.BlockSpec((nseq, CONV_W - 1, qkw), lambda b, c: (b, 0, 0)),
            pl.BlockSpec((nseq, M_HEADS, M_QK, M_V), per_b),
            pl.BlockSpec((nseq, M_HEADS, 1, M_QK), per_b),
            pl.BlockSpec((nseq, M_HEADS, 1, 1), per_b),
        ]
        args += [conv0, c0, n0.reshape(bsz, M_HEADS, 1, M_QK), m0.reshape(bsz, M_HEADS, 1, 1)]
    out_shape = (
        jax.ShapeDtypeStruct((bsz, T, M_V_W), F32),
        jax.ShapeDtypeStruct((bsz, M_HEADS, M_QK, M_V), F32),
        jax.ShapeDtypeStruct((bsz, M_HEADS, 1, M_QK), F32),
        jax.ShapeDtypeStruct((bsz, M_HEADS, 1, 1), F32),
    )
    out_specs = (
        pl.BlockSpec((nseq, L, M_V_W), lambda b, c: (b, c, 0)),
        pl.BlockSpec((nseq, M_HEADS, M_QK, M_V), per_b),
        pl.BlockSpec((nseq, M_HEADS, 1, M_QK), per_b),
        pl.BlockSpec((nseq, M_HEADS, 1, 1), per_b),
    )
    out, c1, n1, m1 = pl.pallas_call(
        functools.partial(_mlstm_kernel, L=L, nseq=nseq, has_state=has_state),
        grid=(bsz // nseq, nc),
        in_specs=in_specs,
        out_specs=out_specs,
        out_shape=out_shape,
        scratch_shapes=scratch,
        compiler_params=pltpu.CompilerParams(
            dimension_semantics=("arbitrary", "arbitrary"), vmem_limit_bytes=VMEM_LIMIT),
        name="mlstm_state" if has_state else "mlstm",
    )(*args)
    return out, c1, n1.reshape(bsz, M_HEADS, M_QK), m1.reshape(bsz, M_HEADS)


def _ssd_kernel(*refs, L, nseq, has_state):
    n_shared = 13
    streams = []
    for sq in range(nseq):
        one = pl.ds(sq, 1)
        views = [r.at[sq] for r in refs[:5]] + list(refs[5:n_shared])
        rest = refs[n_shared:]
        if has_state:
            views += [r.at[one] for r in rest[:2]]
            rest = rest[2:]
        o_ref, h_ref = rest[:2]
        views += [o_ref.at[sq], h_ref.at[one]] + [r.at[sq] for r in rest[2:]]
        streams += _ssd_seq_streams(*views, L=L, has_state=has_state)
    _run_stages(streams)


def _ssd_seq_streams(*refs, L, has_state):
    (z_ref, xs_ref, bp_ref, cp_ref, sm_ref, sb_ref, al_ref, d_ref, ng_ref, e_ref, et_ref,
     w_ref, b_ref) = refs[:13]
    refs = refs[13:]
    if has_state:
        cv0_ref, h0_ref = refs[:2]
        refs = refs[2:]
    o_ref, h_ref, xx_ref, xb_ref, xc_ref, xs_s, bm_s, cm_s = refs

    @pl.when(pl.program_id(1) == 0)
    def _():
        if has_state:
            h_ref[...] = h0_ref[...]
        else:
            h_ref[...] = jnp.zeros_like(h_ref)

    x_cols = slice(0, S_INNER)
    b_cols = slice(S_INNER, S_INNER + S_BCW)
    c_cols = slice(S_INNER + S_BCW, S_CONV_CH)
    for xp_ref, cols in ((xx_ref, x_cols), (xb_ref, b_cols), (xc_ref, c_cols)):
        @pl.when(pl.program_id(1) == 0)
        def _():
            _init_conv(xp_ref, cv0_ref.at[:, :, cols] if has_state else None)

    _conv_silu(xb_ref, bp_ref, w_ref.at[:, b_cols], b_ref.at[:, b_cols], L, bm_s)
    _conv_silu(xc_ref, cp_ref, w_ref.at[:, c_cols], b_ref.at[:, c_cols], L, cm_s)

    dt_all = _softplus(sm_ref[...] + sb_ref[...])
    da_all = dt_all * (-jnp.exp(al_ref[...]))
    lane = lax.broadcasted_iota(jnp.int32, (L, LANES), 1)
    rr = lax.broadcasted_iota(jnp.int32, (L, L), 0)
    cc = lax.broadcasted_iota(jnp.int32, (L, L), 1)
    tril = cc <= rr
    lo = lane < S_HEADDIM
    cum_all = _dot3(tril.astype(BF16), da_all)
    cum_all_t = _tn3(da_all, (rr <= cc).astype(BF16))
    cum2, cum2_t = cum_all * LOG2_E, cum_all_t * LOG2_E
    cum_last = cum_all[L - 1:L, :]
    dec_all = dt_all * jnp.exp(cum_last - cum_all)
    ecum_all = jnp.exp(cum_all)
    mxu_spread = L >= LANES
    if mxu_spread:
        dt_src = _split3(dt_all)[:2]
        dec_src = _split3(dec_all)[:2]
        ecum_src = _split3(ecum_all)[:2]
        clast_src = _split3(jnp.broadcast_to(cum_all_t[:, L - 1:L], (LANES, LANES)))
    else:
        dt_src, dec_src, ecum_src = dt_all, dec_all, ecum_all
        lo_row = lax.broadcasted_iota(jnp.int32, (LANES, 1), 0) < S_HEADDIM

    def spread(src, pair):
        if mxu_spread:
            return sum(jnp.dot(p, e_ref[pair], preferred_element_type=F32) for p in src)
        idx = SM_DT + 2 * pair
        return jnp.where(lo, src[:, idx:idx + 1], src[:, idx + 1:idx + 2])

    def state_decay(pair):
        if mxu_spread:
            return jnp.exp(sum(jnp.dot(et_ref[pair], p, preferred_element_type=F32)
                               for p in clast_src))
        idx = SM_DT + 2 * pair
        return jnp.exp(jnp.where(lo_row, cum_last[:, idx:idx + 1], cum_last[:, idx + 1:idx + 2]))

    def group_stages(g):
        _conv_silu(xx_ref, xs_ref, w_ref.at[:, x_cols], b_ref.at[:, x_cols], L, xs_s,
                   slabs=range(g * S_GW // LANES, (g + 1) * S_GW // LANES))
        yield
        bmb = bm_s[g].astype(BF16)
        cmb = cm_s[g].astype(BF16)
        cb = _nt_dot(cmb, bmb)
        ssq = jnp.zeros((L, 1), F32)
        for p in range(S_HPG // 2):
            yield
            pair = g * (S_HPG // 2) + p
            psl = slice(g * S_GW + p * LANES, g * S_GW + (p + 1) * LANES)
            hsl = slice(p * LANES, (p + 1) * LANES)
            lmat = []
            for e in range(2):
                idx = SM_DT + 2 * pair + e
                lmat.append(jnp.exp2(jnp.where(
                    tril, cum2[:, idx:idx + 1] - cum2_t[idx:idx + 1, :], -jnp.inf)))
            yield
            xs = xs_s[pair]
            xdt = xs * spread(dt_src, pair)
            xdd = xs * spread(dec_src, pair)
            h_prev = h_ref[0, g, hsl, :]
            yield
            xdt_b = xdt.astype(BF16)
            y = (jnp.where(lo,
                           jnp.dot((cb * lmat[0]).astype(BF16), xdt_b,
                                   preferred_element_type=F32),
                           jnp.dot((cb * lmat[1]).astype(BF16), xdt_b,
                                   preferred_element_type=F32))
                 + spread(ecum_src, pair) * _nt_dot(cmb, h_prev.astype(BF16)))
            h_ref[0, g, hsl, :] = state_decay(pair) * h_prev + _tn_dot(xdd.astype(BF16), bmb)
            yield
            gated = (y + d_ref[:, psl] * xs) * _silu(z_ref[:, psl])
            ssq = ssq + jnp.sum(gated * gated, axis=1, keepdims=True)
            o_ref[:, psl] = gated
        yield
        gsl = slice(g * S_GW, (g + 1) * S_GW)
        o_ref[:, gsl] = o_ref[:, gsl] * lax.rsqrt(ssq * (1.0 / S_GW) + EPS) * ng_ref[:, gsl]

    return [group_stages(g) for g in range(S_GROUPS)]


def _pair_onehots():
    n_pairs = S_HEADS // 2
    pair = jnp.arange(n_pairs)[:, None, None]
    src = jnp.arange(LANES)[None, :, None]
    dst = jnp.arange(LANES)[None, None, :]
    onehot = src == SM_DT + 2 * pair + dst // S_HEADDIM
    return onehot.astype(BF16), jnp.swapaxes(onehot, 1, 2).astype(BF16)


def _ssd(proj, bsz, T, s_conv_w, s_conv_b, small_bias, alog_row, d_row, s_norm_g, state=None):
    L = min(T, CHUNK)
    nc = T // L
    nseq = _seqs_per_step(bsz, nc, SSD_SHORT_SEQS)
    has_state = state is not None
    n_pairs = S_HEADS // 2
    onehot, onehot_t = _pair_onehots()

    const = lambda b, c: (0, 0)
    const3 = lambda b, c: (0, 0, 0)
    in_specs = [
        pl.BlockSpec((nseq, L, S_INNER), lambda b, c: (b, c, P_Z // S_INNER)),
        pl.BlockSpec((nseq, L, S_INNER), lambda b, c: (b, c, P_XBC // S_INNER)),
        pl.BlockSpec((nseq, L, S_BCW), lambda b, c: (b, c, P_B // S_BCW)),
        pl.BlockSpec((nseq, L, S_BCW), lambda b, c: (b, c, P_C // S_BCW)),
        pl.BlockSpec((nseq, L, LANES), lambda b, c: (b, c, P_SMALL // LANES)),
        pl.BlockSpec((1, LANES), const),
        pl.BlockSpec((1, LANES), const),
        pl.BlockSpec((1, S_INNER), const),
        pl.BlockSpec((1, S_INNER), const),
        pl.BlockSpec((n_pairs, LANES, LANES), const3),
        pl.BlockSpec((n_pairs, LANES, LANES), const3),
        pl.BlockSpec((CONV_W, S_CONV_CH), const),
        pl.BlockSpec((1, S_CONV_CH), const),
    ]
    args = [proj, proj, proj, proj, proj, small_bias, alog_row, d_row, s_norm_g.reshape(1, -1),
            onehot, onehot_t, s_conv_w, s_conv_b.reshape(1, -1)]
    scratch = [pltpu.VMEM((nseq, width // LANES, rows, LANES), F32)
               for rows in (L + SUBLANES, L) for width in (S_INNER, S_BCW, S_BCW)]
    if has_state:
        conv0, h0 = state
        in_specs += [
            pl.BlockSpec((nseq, CONV_W - 1, S_CONV_CH), lambda b, c: (b, 0, 0)),
            pl.BlockSpec((nseq, S_GROUPS, S_GW, S_STATE), lambda b, c: (b, 0, 0, 0)),
        ]
        args += [conv0, h0.reshape(bsz, S_GROUPS, S_GW, S_STATE)]
    out, h1 = pl.pallas_call(
        functools.partial(_ssd_kernel, L=L, nseq=nseq, has_state=has_state),
        grid=(bsz // nseq, nc),
        in_specs=in_specs,
        out_specs=(
            pl.BlockSpec((nseq, L, S_INNER), lambda b, c: (b, c, 0)),
            pl.BlockSpec((nseq, S_GROUPS, S_GW, S_STATE), lambda b, c: (b, 0, 0, 0)),
        ),
        out_shape=(
            jax.ShapeDtypeStruct((bsz, T, S_INNER), F32),
            jax.ShapeDtypeStruct((bsz, S_GROUPS, S_GW, S_STATE), F32),
        ),
        scratch_shapes=scratch,
        compiler_params=pltpu.CompilerParams(
            dimension_semantics=("arbitrary", "arbitrary"), vmem_limit_bytes=VMEM_LIMIT),
        name="ssd_state" if has_state else "ssd",
    )(*args)
    return out, h1.reshape(bsz, S_HEADS, S_HEADDIM, S_STATE)


def _merge_kernel(x_ref, hm_ref, gt_ref, ga_ref, gb_ref, g_ref, wpa_ref, wpb_ref, wo_ref, o_ref,
                  wpa_s, wpb_s, wo_s):
    @pl.when(pl.program_id(0) == 0)
    def _():
        wpa_s[...] = wpa_ref[...].astype(BF16)
        wpb_s[...] = wpb_ref[...].astype(BF16)
        wo_s[...] = wo_ref[...].astype(BF16)

    ya = jnp.dot(hm_ref[...].astype(BF16), wpa_s[...], preferred_element_type=F32)
    yb = jnp.dot(gt_ref[...].astype(BF16), wpb_s[...], preferred_element_type=F32)
    merged = _sigmoid(ga_ref[...]) * ya + _sigmoid(gb_ref[...]) * yb
    o_ref[...] = x_ref[...] + g_ref[0] * jnp.dot(
        merged.astype(BF16), wo_s[...], preferred_element_type=F32)


def _merge(x, hm, gated, proj, mod3, rows_per_group, kmod, w_pa, w_pb, w_out):
    n = x.shape[0]
    tm = MERGE_TM
    const = lambda i: (0, 0)
    resident = functools.partial(pl.BlockSpec, index_map=const, pipeline_mode=pl.Buffered(1))
    return pl.pallas_call(
        _merge_kernel,
        grid=(n // tm,),
        in_specs=[
            pl.BlockSpec((tm, D_MODEL), lambda i: (i, 0)),
            pl.BlockSpec((tm, M_V_W), lambda i: (i, 0)),
            pl.BlockSpec((tm, S_INNER), lambda i: (i, 0)),
            pl.BlockSpec((tm, D_MODEL), lambda i: (i, P_GA // D_MODEL)),
            pl.BlockSpec((tm, D_MODEL), lambda i: (i, P_GB // D_MODEL)),
            _mod_spec(mod3, rows_per_group, tm, kmod),
            resident((M_V_W, D_MODEL)),
            resident((S_INNER, D_MODEL)),
            resident((D_MODEL, D_MODEL)),
        ],
        out_specs=pl.BlockSpec((tm, D_MODEL), lambda i: (i, 0)),
        out_shape=jax.ShapeDtypeStruct((n, D_MODEL), F32),
        scratch_shapes=[pltpu.VMEM((M_V_W, D_MODEL), BF16),
                        pltpu.VMEM((S_INNER, D_MODEL), BF16),
                        pltpu.VMEM((D_MODEL, D_MODEL), BF16)],
        compiler_params=pltpu.CompilerParams(
            dimension_semantics=("arbitrary",), vmem_limit_bytes=VMEM_LIMIT),
        name="merge",
    )(x, hm, gated, proj, proj, mod3, w_pa, w_pb, w_out)


def _layer(x, mod3, rows_per_group, bsz, T, wts, final_norm, state):
    n = bsz * T
    x = x.reshape(n, D_MODEL)
    x = _ffn(x, mod3, rows_per_group, 0, wts["norm_ff1"], wts["ff1_w1"], wts["ff1_w3"],
             wts["ff1_w2"])
    proj = _proj(x, mod3, rows_per_group, 3, wts["norm_mix"], wts["w_packed"])
    proj3 = proj.reshape(bsz, T, P_WIDTH)
    mixer_args = (proj3, bsz, T)
    m_args = (wts["m_conv_w"], wts["m_conv_b"], wts["small_bias"], wts["m_head_g"])
    s_args = (wts["s_conv_w"], wts["s_conv_b"], wts["small_bias"], wts["alog_row"],
              wts["d_row"], wts["s_norm_g"])
    if state is None:
        m_state = s_state = None
    else:
        c0, n0, m0, mconv0, ssm0, sconv0 = state
        m_state = (mconv0, c0, n0, m0)
        s_state = (sconv0, ssm0)
    hm, c1, n1, m1 = _mlstm(*mixer_args, *m_args, state=m_state)
    gated, h1 = _ssd(*mixer_args, *s_args, state=s_state)
    x = _merge(x, hm.reshape(n, M_V_W), gated.reshape(n, S_INNER), proj, mod3, rows_per_group,
               5, wts["w_pa"], wts["w_pb"], wts["w_out"])
    y = _ffn(x, mod3, rows_per_group, 6, wts["norm_ff2"], wts["ff2_w1"], wts["ff2_w3"],
             wts["ff2_w2"], final_norm=final_norm)
    last = proj3[:, T - (CONV_W - 1):]
    mconv1 = last[:, :, P_QK:P_QK + 2 * M_QK_W]
    sconv1 = last[:, :, P_XBC:P_XBC + S_CONV_CH]
    return y.reshape(bsz, T, D_MODEL), (c1, n1, m1, mconv1, h1, sconv1)


def kernel(x_prompt, x_sample, c_prompt, c_sample, state_mlstm_C, state_mlstm_n, state_mlstm_m, state_mlstm_conv, state_ssm, state_ssm_conv, ada_w, ada_b, norm_ff1, ff1_w1, ff1_w3, ff1_w2, norm_mix, w_in, m_conv_w, m_conv_b, m_if_b, m_head_g, w_pa, s_conv_w, s_conv_b, s_dt_bias, s_A_log, s_D, s_norm_g, w_pb, w_out, norm_ff2, ff2_w1, ff2_w3, ff2_w2, final_norm):
    bp, tp, _ = x_prompt.shape
    bs, ts, _ = x_sample.shape
    depth = ada_w.shape[0]
    assert depth == 1

    hp, hs = x_prompt, x_sample
    new_p, new_s = [], []
    for l in range(depth):
        c_all = jnp.concatenate([c_prompt, c_sample], axis=0)
        pad_rows = (-c_all.shape[0]) % (2 * SUBLANES)
        c_all = jnp.pad(c_all, ((0, pad_rows), (0, 0)))
        mod = _ada(c_all, ada_w[l], ada_b[l])
        mod_p = mod[:bp].reshape(bp, 1, N_MOD * D_MODEL)
        mod_s = jnp.repeat(mod[bp:bp + bs], ts, axis=0).reshape(1, bs * ts, N_MOD * D_MODEL)

        lane_pad = LANES - (2 * M_HEADS + S_HEADS)
        small_bias = jnp.concatenate(
            [m_if_b[l], s_dt_bias[l], jnp.zeros((lane_pad,), F32)]).reshape(1, LANES)
        alog_row = jnp.concatenate(
            [jnp.zeros((2 * M_HEADS,), F32), s_A_log[l], jnp.zeros((lane_pad,), F32)]
        ).reshape(1, LANES)
        wts = {
            "norm_ff1": norm_ff1[l], "ff1_w1": ff1_w1[l], "ff1_w3": ff1_w3[l],
            "ff1_w2": ff1_w2[l],
            "norm_mix": norm_mix[l], "w_packed": _pack_w_in(jnp.swapaxes(w_in, 1, 2)[l]),
            "m_conv_w": m_conv_w[l], "m_conv_b": m_conv_b[l], "small_bias": small_bias,
            "m_head_g": m_head_g[l], "w_pa": w_pa[l],
            "s_conv_w": s_conv_w[l], "s_conv_b": s_conv_b[l], "alog_row": alog_row,
            "d_row": jnp.repeat(s_D[l], S_HEADDIM).reshape(1, S_INNER),
            "s_norm_g": s_norm_g[l], "w_pb": w_pb[l], "w_out": w_out[l],
            "norm_ff2": norm_ff2[l], "ff2_w1": ff2_w1[l], "ff2_w3": ff2_w3[l],
            "ff2_w2": ff2_w2[l],
        }
        st_s = (state_mlstm_C[l], state_mlstm_n[l], state_mlstm_m[l], state_mlstm_conv[l],
                state_ssm[l], state_ssm_conv[l])
        hp, sp = _layer(hp, mod_p, tp, bp, tp, wts, final_norm, None)
        hs, ss = _layer(hs, mod_s, bs * ts, bs, ts, wts, final_norm, st_s)
        new_p.append(sp)
        new_s.append(ss)
    outs_p = [jnp.stack([s[i] for s in new_p]) for i in range(6)]
    outs_s = [jnp.stack([s[i] for s in new_s]) for i in range(6)]
    return (hp, hs, *outs_p, *outs_s)
```

```python
import functools

import jax
import jax.numpy as jnp
from jax import lax
from jax.experimental import pallas as pl
from jax.experimental.pallas import tpu as pltpu

F32 = jnp.float32
BF16 = jnp.bfloat16

D_MODEL = 1024
M_HEADS = 4
M_QK = 256
M_V = 512
M_QK_W = M_HEADS * M_QK
M_V_W = M_HEADS * M_V
S_INNER = 2048
S_HEADDIM = 64
S_HEADS = 32
S_GROUPS = 4
S_HPG = S_HEADS // S_GROUPS
S_STATE = 128
S_GW = S_INNER // S_GROUPS
S_BCW = S_GROUPS * S_STATE
S_CONV_CH = S_INNER + 2 * S_BCW
CONV_W = 4
D_FF = 2816
CHUNK = 128
EPS = 1e-6
N_MOD = 9
K_SCALE = M_QK ** -0.5
LOG2_E = 1.4426950408889634

LANES = 128
SUBLANES = 8
VMEM_LIMIT = 56 * 1024 * 1024

P_QK = 0
P_V = 2 * M_QK_W
P_O = P_V + M_V_W
P_HEAD = P_O + M_V_W
P_Z = P_HEAD
P_XBC = P_Z + S_INNER
P_B = P_XBC + S_INNER
P_C = P_B + S_BCW
P_GA = P_XBC + S_CONV_CH
P_GB = P_GA + D_MODEL
P_SMALL = P_GB + D_MODEL
PROJ_TN = 2304
PACK_TN = 512
P_WIDTH = 13824
SM_I = 0
SM_F = M_HEADS
SM_DT = 2 * M_HEADS

FFN_TM = 2048
FFN_TF = 256
PROJ_TM = 1024
MERGE_TM = 256
MLSTM_SHORT_SEQS = 4
SSD_SHORT_SEQS = 8
LONG_SEQS_PER_STEP = 1
CONV_STRIDE = 4
CONV_ROWS = SUBLANES * CONV_STRIDE


def _sigmoid(x):
    return 0.5 * jnp.tanh(0.5 * x) + 0.5


def _silu(x):
    h = 0.5 * x
    return h * jnp.tanh(h) + h


def _rms(x, g):
    return x * lax.rsqrt(jnp.mean(x * x, axis=-1, keepdims=True) + EPS) * g


def _ada_kernel(c_ref, w_ref, b_ref, o_ref):
    s = _silu(c_ref[...]).astype(BF16)
    o_ref[...] = jnp.dot(s, w_ref[...].astype(BF16), preferred_element_type=F32) + b_ref[...]


def _ada(c_all, ada_w, ada_b):
    m = c_all.shape[0]
    n = ada_w.shape[1]
    tn = D_MODEL
    return pl.pallas_call(
        _ada_kernel,
        grid=(n // tn,),
        in_specs=[
            pl.BlockSpec((m, D_MODEL), lambda j: (0, 0)),
            pl.BlockSpec((D_MODEL, tn), lambda j: (0, j)),
            pl.BlockSpec((1, tn), lambda j: (0, j)),
        ],
        out_specs=pl.BlockSpec((m, tn), lambda j: (0, j)),
        out_shape=jax.ShapeDtypeStruct((m, n), F32),
        compiler_params=pltpu.CompilerParams(
            dimension_semantics=("arbitrary",), vmem_limit_bytes=VMEM_LIMIT),
        name="ada_mod",
    )(c_all, ada_w, ada_b.reshape(1, n))


def _ffn_kernel(x_ref, sh_ref, sc_ref, g_ref, nw_ref, w1_ref, w3_ref, w2_ref, *rest,
                nf, final):
    if final:
        fn_ref, o_ref, u_ref = rest
    else:
        o_ref, u_ref = rest
    f = pl.program_id(1)

    @pl.when(f == 0)
    def _():
        xn = _rms(x_ref[...], nw_ref[...])
        tm = x_ref.shape[0]
        u_ref[...] = (xn * (1.0 + _mod_rows(sc_ref, tm)) + _mod_rows(sh_ref, tm)).astype(BF16)
        o_ref[...] = jnp.zeros_like(o_ref)

    u = u_ref[...]
    h1 = jnp.dot(u, w1_ref[...].astype(BF16), preferred_element_type=F32)
    h3 = jnp.dot(u, w3_ref[...].astype(BF16), preferred_element_type=F32)
    a = (_silu(h1) * h3).astype(BF16)
    o_ref[...] += jnp.dot(a, w2_ref[...].astype(BF16), preferred_element_type=F32)

    @pl.when(f == nf - 1)
    def _():
        y = x_ref[...] + (0.5 * _mod_rows(g_ref, x_ref.shape[0])) * o_ref[...]
        if final:
            y = _rms(y, fn_ref[...])
        o_ref[...] = y


def _mod_spec(rows_per_group, tm, k):
    if rows_per_group >= tm:
        tiles_per_group = rows_per_group // tm
        return pl.BlockSpec((1, 1, D_MODEL), lambda i, *_: (i // tiles_per_group, 0, k))
    groups = tm // rows_per_group
    return pl.BlockSpec((groups, 1, D_MODEL), lambda i, *_: (i, 0, k))


def _mod_rows(ref, tm):
    groups = ref.shape[0]
    if groups == 1:
        return ref[0]
    per_row = jnp.broadcast_to(ref[...], (groups, tm // groups, D_MODEL))
    return per_row.reshape(tm, D_MODEL)


def _ffn(x, mod3, rows_per_group, kmod, norm_w, w1, w3, w2, final_norm=None):
    n = x.shape[0]
    tm = min(FFN_TM, n)
    tf = FFN_TF
    nf = D_FF // tf
    final = final_norm is not None
    in_specs = [
        pl.BlockSpec((tm, D_MODEL), lambda i, f: (i, 0)),
        _mod_spec(rows_per_group, tm,kmod),
        _mod_spec(rows_per_group, tm,kmod + 1),
        _mod_spec(rows_per_group, tm,kmod + 2),
        pl.BlockSpec((1, D_MODEL), lambda i, f: (0, 0)),
        pl.BlockSpec((D_MODEL, tf), lambda i, f: (0, f)),
        pl.BlockSpec((D_MODEL, tf), lambda i, f: (0, f)),
        pl.BlockSpec((tf, D_MODEL), lambda i, f: (f, 0)),
    ]
    args = [x, mod3, mod3, mod3, norm_w.reshape(1, D_MODEL), w1, w3, w2]
    if final:
        in_specs.append(pl.BlockSpec((1, D_MODEL), lambda i, f: (0, 0)))
        args.append(final_norm.reshape(1, D_MODEL))
    return pl.pallas_call(
        functools.partial(_ffn_kernel, nf=nf, final=final),
        grid=(n // tm, nf),
        in_specs=in_specs,
        out_specs=pl.BlockSpec((tm, D_MODEL), lambda i, f: (i, 0)),
        out_shape=jax.ShapeDtypeStruct((n, D_MODEL), F32),
        scratch_shapes=[pltpu.VMEM((tm, D_MODEL), BF16)],
        compiler_params=pltpu.CompilerParams(
            dimension_semantics=("parallel", "arbitrary"), vmem_limit_bytes=VMEM_LIMIT),
        name="ffn_final" if final else "ffn",
    )(*args)


PACK_SHIFT_Z = 2 * M_HEADS
PACK_SHIFT_GA = PACK_SHIFT_Z + S_HEADS
SRC_IF = P_HEAD
SRC_DT = P_HEAD + PACK_SHIFT_Z + S_INNER + S_CONV_CH


def _pack_kernel(a_ref, gif_ref, gdt_ref, o_ref):
    j = pl.program_id(0)

    @pl.when(j < P_SMALL // PACK_TN)
    def _():
        o_ref[...] = a_ref[...].astype(BF16)

    @pl.when(j == P_SMALL // PACK_TN)
    def _():
        pad = jnp.zeros((PACK_TN - SM_DT - S_HEADS, o_ref.shape[1]), F32)
        o_ref[...] = jnp.concatenate([gif_ref[...], gdt_ref[...], pad], axis=0).astype(BF16)


def _pack_w_in(w_t):
    d = w_t.shape[1]
    n_main = P_SMALL // PACK_TN
    last_start = w_t.shape[0] - PACK_TN

    def src_row(j):
        shift = jnp.where(j < P_HEAD // PACK_TN, 0,
                          jnp.where(j < P_GA // PACK_TN, PACK_SHIFT_Z, PACK_SHIFT_GA))
        tile = jnp.minimum((j * PACK_TN + shift) // SUBLANES, last_start // SUBLANES)
        return tile * SUBLANES

    return pl.pallas_call(
        _pack_kernel,
        grid=(n_main + 1,),
        in_specs=[
            pl.BlockSpec((pl.Element(PACK_TN), pl.Element(d)), lambda j: (src_row(j), 0)),
            pl.BlockSpec((pl.Element(SM_DT), pl.Element(d)), lambda j: (SRC_IF, 0)),
            pl.BlockSpec((pl.Element(S_HEADS), pl.Element(d)), lambda j: (SRC_DT, 0)),
        ],
        out_specs=pl.BlockSpec((PACK_TN, d), lambda j: (j, 0)),
        out_shape=jax.ShapeDtypeStruct((P_WIDTH, d), BF16),
        compiler_params=pltpu.CompilerParams(
            dimension_semantics=("arbitrary",), vmem_limit_bytes=VMEM_LIMIT),
        name="pack_w_in",
    )(w_t, w_t, w_t)


def _proj_kernel(x_ref, sh_ref, sc_ref, nw_ref, w_ref, o_ref, u_ref):
    @pl.when(pl.program_id(1) == 0)
    def _():
        xn = _rms(x_ref[...], nw_ref[...])
        tm = x_ref.shape[0]
        u_ref[...] = (xn * (1.0 + _mod_rows(sc_ref, tm)) + _mod_rows(sh_ref, tm)).astype(BF16)

    o_ref[...] = _nt_dot(u_ref[...], w_ref[...])


def _proj(x, mod3, rows_per_group, kmod, norm_w, w_packed):
    n = x.shape[0]
    tm = min(PROJ_TM, n)
    tn = PROJ_TN
    return pl.pallas_call(
        _proj_kernel,
        grid=(n // tm, P_WIDTH // tn),
        in_specs=[
            pl.BlockSpec((tm, D_MODEL), lambda i, j: (i, 0)),
            _mod_spec(rows_per_group, tm,kmod),
            _mod_spec(rows_per_group, tm,kmod + 1),
            pl.BlockSpec((1, D_MODEL), lambda i, j: (0, 0)),
            pl.BlockSpec((tn, D_MODEL), lambda i, j: (j, 0)),
        ],
        out_specs=pl.BlockSpec((tm, tn), lambda i, j: (i, j)),
        out_shape=jax.ShapeDtypeStruct((n, P_WIDTH), F32),
        scratch_shapes=[pltpu.VMEM((tm, D_MODEL), BF16)],
        compiler_params=pltpu.CompilerParams(
            dimension_semantics=("parallel", "arbitrary"), vmem_limit_bytes=VMEM_LIMIT),
        name="mix_proj",
    )(x, mod3, mod3, norm_w.reshape(1, D_MODEL), w_packed)


def _conv_silu(xp_ref, x_ref, w_ref, b_ref, L, dst_ref, slabs=None):
    if slabs is None:
        slabs = range(x_ref.shape[1] // LANES)
    for s in slabs:
        cols = slice(s * LANES, (s + 1) * LANES)
        xp, dst = xp_ref.at[s], dst_ref.at[s]
        xp[pl.ds(SUBLANES, L), :] = x_ref[:, cols]
        bias = b_ref[:, cols]
        taps = [w_ref[t:t + 1, cols] for t in range(CONV_W)]
        if L % CONV_ROWS == 0:
            for r in range(0, L, CONV_ROWS):
                for b in range(CONV_STRIDE):
                    acc = bias
                    for back in range(CONV_W):
                        src = pl.ds(SUBLANES + r + b - back, SUBLANES, stride=CONV_STRIDE)
                        acc = acc + xp[src, :] * taps[CONV_W - 1 - back]
                    dst[pl.ds(r + b, SUBLANES, stride=CONV_STRIDE), :] = _silu(acc)
        else:
            x_all = xp[...]
            acc = bias + x_all[SUBLANES:] * taps[CONV_W - 1]
            for back in range(1, CONV_W):
                acc = acc + pltpu.roll(x_all, back, axis=0)[SUBLANES:] * taps[CONV_W - 1 - back]
            dst[...] = _silu(acc)
        xp[pl.ds(0, SUBLANES), :] = xp[pl.ds(L, SUBLANES), :]


def _init_conv(xp_ref, init_ref):
    n_slabs = xp_ref.shape[0]
    xp_ref[:, pl.ds(0, SUBLANES), :] = jnp.zeros((n_slabs, SUBLANES, LANES), F32)
    if init_ref is not None:
        for s in range(n_slabs):
            xp_ref[s, pl.ds(SUBLANES - (CONV_W - 1), CONV_W - 1), :] = (
                init_ref[0, :, s * LANES:(s + 1) * LANES])


def _run_stages(streams):
    live = list(streams)
    while live:
        for stream in list(live):
            try:
                next(stream)
            except StopIteration:
                live.remove(stream)


def _split3(a):
    hi = a.astype(BF16)
    r1 = a - hi.astype(F32)
    mid = r1.astype(BF16)
    lo = (r1 - mid.astype(F32)).astype(BF16)
    return hi, mid, lo


def _nt_dot(a, b):
    return lax.dot_general(a, b, (((1,), (1,)), ((), ())), preferred_element_type=F32)


def _tn_dot(a, b):
    return lax.dot_general(a, b, (((0,), (0,)), ((), ())), preferred_element_type=F32)


def _dot3(mask_b, a):
    return sum(jnp.dot(mask_b, p, preferred_element_type=F32) for p in _split3(a))


def _tn3(a, mask_b):
    return sum(_tn_dot(p, mask_b) for p in _split3(a))


def _log_sigmoid(x):
    return jnp.minimum(x, 0.0) - jnp.log1p(jnp.exp(-jnp.abs(x)))


def _softplus(x):
    return jnp.maximum(x, 0.0) + jnp.log1p(jnp.exp(-jnp.abs(x)))


def _mlstm_kernel(*refs, L, nseq, has_state):
    qk_ref, v_ref, og_ref, sm_ref, sb_ref, hg_ref, w_ref, b_ref = refs[:8]
    refs = refs[8:]
    if has_state:
        cv0_ref, c0_ref, n0_ref, m0_ref = refs[:4]
        refs = refs[4:]
    o_ref, c_ref, n_ref, m_ref, xp_ref, qk_s = refs

    @pl.when(pl.program_id(1) == 0)
    def _():
        if has_state:
            c_ref[...] = c0_ref[...]
            n_ref[...] = n0_ref[...]
            m_ref[...] = m0_ref[...]
        else:
            c_ref[...] = jnp.zeros_like(c_ref)
            n_ref[...] = jnp.zeros_like(n_ref)
            m_ref[...] = jnp.zeros_like(m_ref)
        for s in range(nseq):
            _init_conv(xp_ref.at[s], cv0_ref.at[pl.ds(s, 1)] if has_state else None)

    rr = lax.broadcasted_iota(jnp.int32, (L, L), 0)
    cc = lax.broadcasted_iota(jnp.int32, (L, L), 1)
    tril = cc <= rr
    gates = []
    for s in range(nseq):
        sm = sm_ref[s] + sb_ref[...]
        lf_all = _log_sigmoid(sm)
        b_all = _dot3(tril.astype(BF16), lf_all)
        b_all_t = _tn3(lf_all, (rr <= cc).astype(BF16))
        sm_t = _tn3(sm, (rr == cc).astype(BF16))
        gates.append((sm, sm_t, b_all, b_all_t))

    def head_stages(s, h):
        sm, sm_t, b_all, b_all_t = gates[s]
        qk_seq = qk_s.at[s]
        q_slabs = range(h * M_QK // LANES, (h + 1) * M_QK // LANES)
        k_slabs = range((M_QK_W + h * M_QK) // LANES, (M_QK_W + (h + 1) * M_QK) // LANES)
        _conv_silu(xp_ref.at[s], qk_ref.at[s], w_ref, b_ref, L, qk_seq,
                   slabs=[*q_slabs, *k_slabs])
        yield
        q = jnp.concatenate([qk_seq[i] for i in q_slabs], axis=1)
        k = jnp.concatenate([qk_seq[i] for i in k_slabs], axis=1) * K_SCALE
        v = v_ref[s, :, h * M_V:(h + 1) * M_V]
        i_col = sm[:, SM_I + h:SM_I + h + 1]
        i_row = sm_t[SM_I + h:SM_I + h + 1, :]
        b_col = b_all[:, SM_F + h:SM_F + h + 1]
        b_row = b_all_t[SM_F + h:SM_F + h + 1, :]

        c_prev = c_ref[s, h]
        n_prev = n_ref[s, h]
        m_prev = m_ref[s, h]

        dmat = jnp.where(tril, b_col - b_row + i_row, -jnp.inf)
        inter = b_col + m_prev
        mt = jnp.maximum(inter, jnp.max(dmat, axis=1, keepdims=True))
        w_tok = jnp.exp(dmat - mt)
        w_st = jnp.exp(inter - mt)
        yield

        qb = q.astype(BF16)
        kb = k.astype(BF16)
        vb = v.astype(BF16)
        sc = _nt_dot(qb, kb) * w_tok
        yield
        num = (jnp.dot(sc.astype(BF16), vb, preferred_element_type=F32)
               + w_st * jnp.dot(qb, c_prev.astype(BF16), preferred_element_type=F32))
        den = (jnp.sum(sc, axis=1, keepdims=True)
               + w_st * jnp.sum(q * n_prev, axis=1, keepdims=True))
        yield
        hh = num / jnp.maximum(jnp.abs(den), jnp.exp(-mt))

        m_new = mt[L - 1:L, :]
        b_last = b_col[L - 1:L, :]
        ws = jnp.exp(b_last - b_col + i_col - m_new)
        decay = jnp.exp(b_last + m_prev - m_new)
        kw = k * ws
        c_ref[s, h] = decay * c_prev + _tn_dot(kw.astype(BF16), vb)
        n_ref[s, h] = decay * n_prev + jnp.sum(kw, axis=0, keepdims=True)
        m_ref[s, h] = m_new
        yield

        mu = jnp.mean(hh, axis=-1, keepdims=True)
        hc = hh - mu
        yield
        var = jnp.mean(hc * hc, axis=-1, keepdims=True)
        hsl = slice(h * M_V, (h + 1) * M_V)
        gate = (0.5 * hg_ref[:, hsl]) * (jnp.tanh(0.5 * og_ref[s, :, hsl]) + 1.0)
        o_ref[s, :, hsl] = (hc * lax.rsqrt(var + EPS)) * gate

    _run_stages([head_stages(s, h) for s in range(nseq) for h in range(M_HEADS)])


def _seqs_per_step(bsz, nc, short):
    want = short if nc == 1 else LONG_SEQS_PER_STEP
    return want if bsz % want == 0 else 1


def _mlstm(proj, bsz, T, m_conv_w, m_conv_b, small_bias, m_head_g, state=None):
    L = min(T, CHUNK)
    nc = T // L
    nseq = _seqs_per_step(bsz, nc, MLSTM_SHORT_SEQS)
    has_state = state is not None
    qkw = 2 * M_QK_W

    const = lambda b, c: (0, 0)
    per_b = lambda b, c: (b, 0, 0, 0)
    in_specs = [
        pl.BlockSpec((nseq, L, qkw), lambda b, c: (b, c, P_QK // qkw)),
        pl.BlockSpec((nseq, L, M_V_W), lambda b, c: (b, c, P_V // M_V_W)),
        pl.BlockSpec((nseq, L, M_V_W), lambda b, c: (b, c, P_O // M_V_W)),
        pl.BlockSpec((nseq, L, LANES), lambda b, c: (b, c, P_SMALL // LANES)),
        pl.BlockSpec((1, LANES), const),
        pl.BlockSpec((1, M_V_W), const),
        pl.BlockSpec((CONV_W, qkw), const),
        pl.BlockSpec((1, qkw), const),
    ]
    args = [proj, proj, proj, proj, small_bias, m_head_g.reshape(1, -1),
            m_conv_w, m_conv_b.reshape(1, -1)]
    scratch = [pltpu.VMEM((nseq, qkw // LANES, L + SUBLANES, LANES), F32),
               pltpu.VMEM((nseq, qkw // LANES, L, LANES), F32)]
    if has_state:
        conv0, c0, n0, m0 = state
        in_specs += [
            pl.BlockSpec((nseq, CONV_W - 1, qkw), lambda b, c: (b, 0, 0)),
            pl.BlockSpec((nseq, M_HEADS, M_QK, M_V), per_b),
            pl.BlockSpec((nseq, M_HEADS, 1, M_QK), per_b),
            pl.BlockSpec((nseq, M_HEADS, 1, 1), per_b),
        ]
        args += [conv0, c0, n0.reshape(bsz, M_HEADS, 1, M_QK), m0.reshape(bsz, M_HEADS, 1, 1)]
    out_shape = (
        jax.ShapeDtypeStruct((bsz, T, M_V_W), F32),
        jax.ShapeDtypeStruct((bsz, M_HEADS, M_QK, M_V), F32),
        jax.ShapeDtypeStruct((bsz, M_HEADS, 1, M_QK), F32),
        jax.ShapeDtypeStruct((bsz, M_HEADS, 1, 1), F32),
    )
    out_specs = (
        pl.BlockSpec((nseq, L, M_V_W), lambda b, c: (b, c, 0)),
        pl.BlockSpec((nseq, M_HEADS, M_QK, M_V), per_b),
        pl.BlockSpec((nseq, M_HEADS, 1, M_QK), per_b),
        pl.BlockSpec((nseq, M_HEADS, 1, 1), per_b),
    )
    out, c1, n1, m1 = pl.pallas_call(
        functools.partial(_mlstm_kernel, L=L, nseq=nseq, has_state=has_state),
        grid=(bsz // nseq, nc),
        in_specs=in_specs,
        out_specs=out_specs,
        out_shape=out_shape,
        scratch_shapes=scratch,
        compiler_params=pltpu.CompilerParams(
            dimension_semantics=("arbitrary", "arbitrary"), vmem_limit_bytes=VMEM_LIMIT),
        name="mlstm_state" if has_state else "mlstm",
    )(*args)
    return out, c1, n1.reshape(bsz, M_HEADS, M_QK), m1.reshape(bsz, M_HEADS)


def _ssd_kernel(*refs, L, nseq, has_state):
    n_shared = 13
    streams = []
    for sq in range(nseq):
        one = pl.ds(sq, 1)
        views = [r.at[sq] for r in refs[:5]] + list(refs[5:n_shared])
        rest = refs[n_shared:]
        if has_state:
            views += [r.at[one] for r in rest[:2]]
            rest = rest[2:]
        o_ref, h_ref = rest[:2]
        views += [o_ref.at[sq], h_ref.at[one]] + [r.at[sq] for r in rest[2:]]
        streams += _ssd_seq_streams(*views, L=L, has_state=has_state)
    _run_stages(streams)


def _ssd_seq_streams(*refs, L, has_state):
    (z_ref, xs_ref, bp_ref, cp_ref, sm_ref, sb_ref, al_ref, d_ref, ng_ref, e_ref, et_ref,
     w_ref, b_ref) = refs[:13]
    refs = refs[13:]
    if has_state:
        cv0_ref, h0_ref = refs[:2]
        refs = refs[2:]
    o_ref, h_ref, xx_ref, xb_ref, xc_ref, xs_s, bm_s, cm_s = refs

    @pl.when(pl.program_id(1) == 0)
    def _():
        if has_state:
            h_ref[...] = h0_ref[...]
        else:
            h_ref[...] = jnp.zeros_like(h_ref)

    x_cols = slice(0, S_INNER)
    b_cols = slice(S_INNER, S_INNER + S_BCW)
    c_cols = slice(S_INNER + S_BCW, S_CONV_CH)
    for xp_ref, cols in ((xx_ref, x_cols), (xb_ref, b_cols), (xc_ref, c_cols)):
        @pl.when(pl.program_id(1) == 0)
        def _():
            _init_conv(xp_ref, cv0_ref.at[:, :, cols] if has_state else None)

    _conv_silu(xb_ref, bp_ref, w_ref.at[:, b_cols], b_ref.at[:, b_cols], L, bm_s)
    _conv_silu(xc_ref, cp_ref, w_ref.at[:, c_cols], b_ref.at[:, c_cols], L, cm_s)

    dt_all = _softplus(sm_ref[...] + sb_ref[...])
    da_all = dt_all * (-jnp.exp(al_ref[...]))
    lane = lax.broadcasted_iota(jnp.int32, (L, LANES), 1)
    rr = lax.broadcasted_iota(jnp.int32, (L, L), 0)
    cc = lax.broadcasted_iota(jnp.int32, (L, L), 1)
    tril = cc <= rr
    lo = lane < S_HEADDIM
    cum_all = _dot3(tril.astype(BF16), da_all)
    cum_all_t = _tn3(da_all, (rr <= cc).astype(BF16))
    cum2, cum2_t = cum_all * LOG2_E, cum_all_t * LOG2_E
    cum_last = cum_all[L - 1:L, :]
    dec_all = dt_all * jnp.exp(cum_last - cum_all)
    ecum_all = jnp.exp(cum_all)
    mxu_spread = L >= LANES
    if mxu_spread:
        dt_src = _split3(dt_all)[:2]
        dec_src = _split3(dec_all)[:2]
        ecum_src = _split3(ecum_all)[:2]
        clast_src = _split3(jnp.broadcast_to(cum_all_t[:, L - 1:L], (LANES, LANES)))
    else:
        dt_src, dec_src, ecum_src = dt_all, dec_all, ecum_all
        lo_row = lax.broadcasted_iota(jnp.int32, (LANES, 1), 0) < S_HEADDIM

    def spread(src, pair):
        if mxu_spread:
            return sum(jnp.dot(p, e_ref[pair], preferred_element_type=F32) for p in src)
        idx = SM_DT + 2 * pair
        return jnp.where(lo, src[:, idx:idx + 1], src[:, idx + 1:idx + 2])

    def state_decay(pair):
        if mxu_spread:
            return jnp.exp(sum(jnp.dot(et_ref[pair], p, preferred_element_type=F32)
                               for p in clast_src))
        idx = SM_DT + 2 * pair
        return jnp.exp(jnp.where(lo_row, cum_last[:, idx:idx + 1], cum_last[:, idx + 1:idx + 2]))

    def group_stages(g):
        _conv_silu(xx_ref, xs_ref, w_ref.at[:, x_cols], b_ref.at[:, x_cols], L, xs_s,
                   slabs=range(g * S_GW // LANES, (g + 1) * S_GW // LANES))
        yield
        bmb = bm_s[g].astype(BF16)
        cmb = cm_s[g].astype(BF16)
        cb = _nt_dot(cmb, bmb)
        ssq = jnp.zeros((L, 1), F32)
        for p in range(S_HPG // 2):
            yield
            pair = g * (S_HPG // 2) + p
            psl = slice(g * S_GW + p * LANES, g * S_GW + (p + 1) * LANES)
            hsl = slice(p * LANES, (p + 1) * LANES)
            lmat = []
            for e in range(2):
                idx = SM_DT + 2 * pair + e
                lmat.append(jnp.exp2(jnp.where(
                    tril, cum2[:, idx:idx + 1] - cum2_t[idx:idx + 1, :], -jnp.inf)))
            yield
            xs = xs_s[pair]
            xdt = xs * spread(dt_src, pair)
            xdd = xs * spread(dec_src, pair)
            h_prev = h_ref[0, g, hsl, :]
            yield
            xdt_b = xdt.astype(BF16)
            y = (jnp.where(lo,
                           jnp.dot((cb * lmat[0]).astype(BF16), xdt_b,
                                   preferred_element_type=F32),
                           jnp.dot((cb * lmat[1]).astype(BF16), xdt_b,
                                   preferred_element_type=F32))
                 + spread(ecum_src, pair) * _nt_dot(cmb, h_prev.astype(BF16)))
            h_ref[0, g, hsl, :] = state_decay(pair) * h_prev + _tn_dot(xdd.astype(BF16), bmb)
            yield
            gated = (y + d_ref[:, psl] * xs) * _silu(z_ref[:, psl])
            ssq = ssq + jnp.sum(gated * gated, axis=1, keepdims=True)
            o_ref[:, psl] = gated
        yield
        gsl = slice(g * S_GW, (g + 1) * S_GW)
        o_ref[:, gsl] = o_ref[:, gsl] * lax.rsqrt(ssq * (1.0 / S_GW) + EPS) * ng_ref[:, gsl]

    return [group_stages(g) for g in range(S_GROUPS)]


def _pair_onehots():
    n_pairs = S_HEADS // 2
    pair = jnp.arange(n_pairs)[:, None, None]
    src = jnp.arange(LANES)[None, :, None]
    dst = jnp.arange(LANES)[None, None, :]
    onehot = src == SM_DT + 2 * pair + dst // S_HEADDIM
    return onehot.astype(BF16), jnp.swapaxes(onehot, 1, 2).astype(BF16)


def _ssd(proj, bsz, T, s_conv_w, s_conv_b, small_bias, alog_row, d_row, s_norm_g, state=None):
    L = min(T, CHUNK)
    nc = T // L
    nseq = _seqs_per_step(bsz, nc, SSD_SHORT_SEQS)
    has_state = state is not None
    n_pairs = S_HEADS // 2
    onehot, onehot_t = _pair_onehots()

    const = lambda b, c: (0, 0)
    const3 = lambda b, c: (0, 0, 0)
    in_specs = [
        pl.BlockSpec((nseq, L, S_INNER), lambda b, c: (b, c, P_Z // S_INNER)),
        pl.BlockSpec((nseq, L, S_INNER), lambda b, c: (b, c, P_XBC // S_INNER)),
        pl.BlockSpec((nseq, L, S_BCW), lambda b, c: (b, c, P_B // S_BCW)),
        pl.BlockSpec((nseq, L, S_BCW), lambda b, c: (b, c, P_C // S_BCW)),
        pl.BlockSpec((nseq, L, LANES), lambda b, c: (b, c, P_SMALL // LANES)),
        pl.BlockSpec((1, LANES), const),
        pl.BlockSpec((1, LANES), const),
        pl.BlockSpec((1, S_INNER), const),
        pl.BlockSpec((1, S_INNER), const),
        pl.BlockSpec((n_pairs, LANES, LANES), const3),
        pl.BlockSpec((n_pairs, LANES, LANES), const3),
        pl.BlockSpec((CONV_W, S_CONV_CH), const),
        pl.BlockSpec((1, S_CONV_CH), const),
    ]
    args = [proj, proj, proj, proj, proj, small_bias, alog_row, d_row, s_norm_g.reshape(1, -1),
            onehot, onehot_t, s_conv_w, s_conv_b.reshape(1, -1)]
    scratch = [pltpu.VMEM((nseq, width // LANES, rows, LANES), F32)
               for rows in (L + SUBLANES, L) for width in (S_INNER, S_BCW, S_BCW)]
    if has_state:
        conv0, h0 = state
        in_specs += [
            pl.BlockSpec((nseq, CONV_W - 1, S_CONV_CH), lambda b, c: (b, 0, 0)),
            pl.BlockSpec((nseq, S_GROUPS, S_GW, S_STATE), lambda b, c: (b, 0, 0, 0)),
        ]
        args += [conv0, h0.reshape(bsz, S_GROUPS, S_GW, S_STATE)]
    out, h1 = pl.pallas_call(
        functools.partial(_ssd_kernel, L=L, nseq=nseq, has_state=has_state),
        grid=(bsz // nseq, nc),
        in_specs=in_specs,
        out_specs=(
            pl.BlockSpec((nseq, L, S_INNER), lambda b, c: (b, c, 0)),
            pl.BlockSpec((nseq, S_GROUPS, S_GW, S_STATE), lambda b, c: (b, 0, 0, 0)),
        ),
        out_shape=(
            jax.ShapeDtypeStruct((bsz, T, S_INNER), F32),
            jax.ShapeDtypeStruct((bsz, S_GROUPS, S_GW, S_STATE), F32),
        ),
        scratch_shapes=scratch,
        compiler_params=pltpu.CompilerParams(
            dimension_semantics=("arbitrary", "arbitrary"), vmem_limit_bytes=VMEM_LIMIT),
        name="ssd_state" if has_state else "ssd",
    )(*args)
    return out, h1.reshape(bsz, S_HEADS, S_HEADDIM, S_STATE)


def _merge_kernel(x_ref, hm_ref, gt_ref, ga_ref, gb_ref, g_ref, wpa_ref, wpb_ref, wo_ref, o_ref,
                  wpa_s, wpb_s, wo_s):
    @pl.when(pl.program_id(0) == 0)
    def _():
        wpa_s[...] = wpa_ref[...].astype(BF16)
        wpb_s[...] = wpb_ref[...].astype(BF16)
        wo_s[...] = wo_ref[...].astype(BF16)

    ya = jnp.dot(hm_ref[...].astype(BF16), wpa_s[...], preferred_element_type=F32)
    yb = jnp.dot(gt_ref[...].astype(BF16), wpb_s[...], preferred_element_type=F32)
    merged = _sigmoid(ga_ref[...]) * ya + _sigmoid(gb_ref[...]) * yb
    o_ref[...] = x_ref[...] + _mod_rows(g_ref, x_ref.shape[0]) * jnp.dot(
        merged.astype(BF16), wo_s[...], preferred_element_type=F32)


def _merge(x, hm, gated, proj, mod3, rows_per_group, kmod, w_pa, w_pb, w_out):
    n = x.shape[0]
    tm = MERGE_TM
    const = lambda i: (0, 0)
    resident = functools.partial(pl.BlockSpec, index_map=const, pipeline_mode=pl.Buffered(1))
    return pl.pallas_call(
        _merge_kernel,
        grid=(n // tm,),
        in_specs=[
            pl.BlockSpec((tm, D_MODEL), lambda i: (i, 0)),
            pl.BlockSpec((tm, M_V_W), lambda i: (i, 0)),
            pl.BlockSpec((tm, S_INNER), lambda i: (i, 0)),
            pl.BlockSpec((tm, D_MODEL), lambda i: (i, P_GA // D_MODEL)),
            pl.BlockSpec((tm, D_MODEL), lambda i: (i, P_GB // D_MODEL)),
            _mod_spec(rows_per_group, tm,kmod),
            resident((M_V_W, D_MODEL)),
            resident((S_INNER, D_MODEL)),
            resident((D_MODEL, D_MODEL)),
        ],
        out_specs=pl.BlockSpec((tm, D_MODEL), lambda i: (i, 0)),
        out_shape=jax.ShapeDtypeStruct((n, D_MODEL), F32),
        scratch_shapes=[pltpu.VMEM((M_V_W, D_MODEL), BF16),
                        pltpu.VMEM((S_INNER, D_MODEL), BF16),
                        pltpu.VMEM((D_MODEL, D_MODEL), BF16)],
        compiler_params=pltpu.CompilerParams(
            dimension_semantics=("arbitrary",), vmem_limit_bytes=VMEM_LIMIT),
        name="merge",
    )(x, hm, gated, proj, proj, mod3, w_pa, w_pb, w_out)


def _layer(x, mod3, rows_per_group, bsz, T, wts, final_norm, state):
    n = bsz * T
    x = x.reshape(n, D_MODEL)
    x = _ffn(x, mod3, rows_per_group, 0, wts["norm_ff1"], wts["ff1_w1"], wts["ff1_w3"],
             wts["ff1_w2"])
    proj = _proj(x, mod3, rows_per_group, 3, wts["norm_mix"], wts["w_packed"])
    proj3 = proj.reshape(bsz, T, P_WIDTH)
    mixer_args = (proj3, bsz, T)
    m_args = (wts["m_conv_w"], wts["m_conv_b"], wts["small_bias"], wts["m_head_g"])
    s_args = (wts["s_conv_w"], wts["s_conv_b"], wts["small_bias"], wts["alog_row"],
              wts["d_row"], wts["s_norm_g"])
    if state is None:
        m_state = s_state = None
    else:
        c0, n0, m0, mconv0, ssm0, sconv0 = state
        m_state = (mconv0, c0, n0, m0)
        s_state = (sconv0, ssm0)
    hm, c1, n1, m1 = _mlstm(*mixer_args, *m_args, state=m_state)
    gated, h1 = _ssd(*mixer_args, *s_args, state=s_state)
    x = _merge(x, hm.reshape(n, M_V_W), gated.reshape(n, S_INNER), proj, mod3, rows_per_group,
               5, wts["w_pa"], wts["w_pb"], wts["w_out"])
    y = _ffn(x, mod3, rows_per_group, 6, wts["norm_ff2"], wts["ff2_w1"], wts["ff2_w3"],
             wts["ff2_w2"], final_norm=final_norm)
    last = proj3[:, T - (CONV_W - 1):]
    mconv1 = last[:, :, P_QK:P_QK + 2 * M_QK_W]
    sconv1 = last[:, :, P_XBC:P_XBC + S_CONV_CH]
    return y.reshape(bsz, T, D_MODEL), (c1, n1, m1, mconv1, h1, sconv1)


def kernel(x_prompt, x_sample, c_prompt, c_sample, state_mlstm_C, state_mlstm_n, state_mlstm_m, state_mlstm_conv, state_ssm, state_ssm_conv, ada_w, ada_b, norm_ff1, ff1_w1, ff1_w3, ff1_w2, norm_mix, w_in, m_conv_w, m_conv_b, m_if_b, m_head_g, w_pa, s_conv_w, s_conv_b, s_dt_bias, s_A_log, s_D, s_norm_g, w_pb, w_out, norm_ff2, ff2_w1, ff2_w3, ff2_w2, final_norm):
    bp, tp, _ = x_prompt.shape
    bs, ts, _ = x_sample.shape
    depth = ada_w.shape[0]
    assert depth == 1

    hp, hs = x_prompt, x_sample
    new_p, new_s = [], []
    for l in range(depth):
        c_all = jnp.concatenate([c_prompt, c_sample], axis=0)
        pad_rows = (-c_all.shape[0]) % (2 * SUBLANES)
        c_all = jnp.pad(c_all, ((0, pad_rows), (0, 0)))
        mod = _ada(c_all, ada_w[l], ada_b[l])
        mod_p = mod[:bp].reshape(bp, 1, N_MOD * D_MODEL)
        mod_s = mod[bp:bp + bs].reshape(bs, 1, N_MOD * D_MODEL)

        lane_pad = LANES - (2 * M_HEADS + S_HEADS)
        small_bias = jnp.concatenate(
            [m_if_b[l], s_dt_bias[l], jnp.zeros((lane_pad,), F32)]).reshape(1, LANES)
        alog_row = jnp.concatenate(
            [jnp.zeros((2 * M_HEADS,), F32), s_A_log[l], jnp.zeros((lane_pad,), F32)]
        ).reshape(1, LANES)
        wts = {
            "norm_ff1": norm_ff1[l], "ff1_w1": ff1_w1[l], "ff1_w3": ff1_w3[l],
            "ff1_w2": ff1_w2[l],
            "norm_mix": norm_mix[l], "w_packed": _pack_w_in(jnp.swapaxes(w_in, 1, 2)[l]),
            "m_conv_w": m_conv_w[l], "m_conv_b": m_conv_b[l], "small_bias": small_bias,
            "m_head_g": m_head_g[l], "w_pa": w_pa[l],
            "s_conv_w": s_conv_w[l], "s_conv_b": s_conv_b[l], "alog_row": alog_row,
            "d_row": jnp.repeat(s_D[l], S_HEADDIM).reshape(1, S_INNER),
            "s_norm_g": s_norm_g[l], "w_pb": w_pb[l], "w_out": w_out[l],
            "norm_ff2": norm_ff2[l], "ff2_w1": ff2_w1[l], "ff2_w3": ff2_w3[l],
            "ff2_w2": ff2_w2[l],
        }
        st_s = (state_mlstm_C[l], state_mlstm_n[l], state_mlstm_m[l], state_mlstm_conv[l],
                state_ssm[l], state_ssm_conv[l])
        hp, sp = _layer(hp, mod_p, tp, bp, tp, wts, final_norm, None)
        hs, ss = _layer(hs, mod_s, ts, bs, ts, wts, final_norm, st_s)
        new_p.append(sp)
        new_s.append(ss)
    outs_p = [jnp.stack([s[i] for s in new_p]) for i in range(6)]
    outs_s = [jnp.stack([s[i] for s in new_s]) for i in range(6)]
    return (hp, hs, *outs_p, *outs_s)
```

```python
import functools

import jax
import jax.numpy as jnp
from jax import lax
from jax.experimental import pallas as pl
from jax.experimental.pallas import tpu as pltpu

F32 = jnp.float32
BF16 = jnp.bfloat16

D_MODEL = 1024
M_HEADS = 4
M_QK = 256
M_V = 512
M_QK_W = M_HEADS * M_QK
M_V_W = M_HEADS * M_V
S_INNER = 2048
S_HEADDIM = 64
S_HEADS = 32
S_GROUPS = 4
S_HPG = S_HEADS // S_GROUPS
S_STATE = 128
S_GW = S_INNER // S_GROUPS
S_BCW = S_GROUPS * S_STATE
S_CONV_CH = S_INNER + 2 * S_BCW
CONV_W = 4
D_FF = 2816
CHUNK = 128
EPS = 1e-6
N_MOD = 9
K_SCALE = M_QK ** -0.5
LOG2_E = 1.4426950408889634

LANES = 128
SUBLANES = 8
VMEM_LIMIT = 56 * 1024 * 1024

P_QK = 0
P_V = 2 * M_QK_W
P_O = P_V + M_V_W
P_HEAD = P_O + M_V_W
P_Z = P_HEAD
P_XBC = P_Z + S_INNER
P_B = P_XBC + S_INNER
P_C = P_B + S_BCW
P_GA = P_XBC + S_CONV_CH
P_GB = P_GA + D_MODEL
P_SMALL = P_GB + D_MODEL
PROJ_TN = 2304
PACK_TN = 512
P_WIDTH = 13824
SM_I = 0
SM_F = M_HEADS
SM_DT = 2 * M_HEADS

ADA_TN = 2304
FFN_TM = 2048
FFN_TF = 256
PROJ_TM = 1024
MERGE_TM = 256
MLSTM_SHORT_SEQS = 4
SSD_SHORT_SEQS = 8
LONG_SEQS_PER_STEP = 1
CONV_STRIDE = 4
CONV_ROWS = SUBLANES * CONV_STRIDE


def _sigmoid(x):
    return 0.5 * jnp.tanh(0.5 * x) + 0.5


def _silu(x):
    h = 0.5 * x
    return h * jnp.tanh(h) + h


def _rms(x, g):
    return x * lax.rsqrt(jnp.mean(x * x, axis=-1, keepdims=True) + EPS) * g


def _ada_kernel(c_ref, w_ref, b_ref, o_ref):
    s = _silu(c_ref[...]).astype(BF16)
    o_ref[...] = jnp.dot(s, w_ref[...].astype(BF16), preferred_element_type=F32) + b_ref[...]


def _ada(c_all, ada_w, ada_b):
    m = c_all.shape[0]
    n = ada_w.shape[1]
    tn = ADA_TN
    assert n % tn == 0
    return pl.pallas_call(
        _ada_kernel,
        grid=(n // tn,),
        in_specs=[
            pl.BlockSpec((m, D_MODEL), lambda j: (0, 0)),
            pl.BlockSpec((D_MODEL, tn), lambda j: (0, j)),
            pl.BlockSpec((1, tn), lambda j: (0, j)),
        ],
        out_specs=pl.BlockSpec((m, tn), lambda j: (0, j)),
        out_shape=jax.ShapeDtypeStruct((m, n), F32),
        compiler_params=pltpu.CompilerParams(
            dimension_semantics=("arbitrary",), vmem_limit_bytes=VMEM_LIMIT),
        name="ada_mod",
    )(c_all, ada_w, ada_b.reshape(1, n))


def _ffn_kernel(x_ref, sh_ref, sc_ref, g_ref, nw_ref, w1_ref, w3_ref, w2_ref, *rest,
                nf, final):
    if final:
        fn_ref, o_ref, u_ref = rest
    else:
        o_ref, u_ref = rest
    f = pl.program_id(1)

    @pl.when(f == 0)
    def _():
        tm = x_ref.shape[0]
        gain = nw_ref[...] * (1.0 + _mod_rows(sc_ref, tm))
        u_ref[...] = (_rms(x_ref[...], gain) + _mod_rows(sh_ref, tm)).astype(BF16)
        o_ref[...] = jnp.zeros_like(o_ref)

    u = u_ref[...]
    h1 = jnp.dot(u, w1_ref[...].astype(BF16), preferred_element_type=F32)
    h3 = jnp.dot(u, w3_ref[...].astype(BF16), preferred_element_type=F32)
    a = (_silu(h1) * h3).astype(BF16)
    o_ref[...] += jnp.dot(a, w2_ref[...].astype(BF16), preferred_element_type=F32)

    @pl.when(f == nf - 1)
    def _():
        y = x_ref[...] + (0.5 * _mod_rows(g_ref, x_ref.shape[0])) * o_ref[...]
        if final:
            y = _rms(y, fn_ref[...])
        o_ref[...] = y


def _mod_spec(rows_per_group, tm, k):
    if rows_per_group >= tm:
        tiles_per_group = rows_per_group // tm
        return pl.BlockSpec((1, 1, D_MODEL), lambda i, *_: (i // tiles_per_group, 0, k))
    groups = tm // rows_per_group
    return pl.BlockSpec((groups, 1, D_MODEL), lambda i, *_: (i, 0, k))


def _mod_rows(ref, tm):
    groups = ref.shape[0]
    if groups == 1:
        return ref[0]
    per_row = jnp.broadcast_to(ref[...], (groups, tm // groups, D_MODEL))
    return per_row.reshape(tm, D_MODEL)


def _ffn(x, mod3, rows_per_group, kmod, norm_w, w1, w3, w2, final_norm=None):
    n = x.shape[0]
    tm = min(FFN_TM, n)
    assert n % tm == 0 and (rows_per_group % tm == 0 or tm % rows_per_group == 0)
    tf = FFN_TF
    nf = D_FF // tf
    final = final_norm is not None
    in_specs = [
        pl.BlockSpec((tm, D_MODEL), lambda i, f: (i, 0)),
        _mod_spec(rows_per_group, tm,kmod),
        _mod_spec(rows_per_group, tm,kmod + 1),
        _mod_spec(rows_per_group, tm,kmod + 2),
        pl.BlockSpec((1, D_MODEL), lambda i, f: (0, 0)),
        pl.BlockSpec((D_MODEL, tf), lambda i, f: (0, f)),
        pl.BlockSpec((D_MODEL, tf), lambda i, f: (0, f)),
        pl.BlockSpec((tf, D_MODEL), lambda i, f: (f, 0)),
    ]
    args = [x, mod3, mod3, mod3, norm_w.reshape(1, D_MODEL), w1, w3, w2]
    if final:
        in_specs.append(pl.BlockSpec((1, D_MODEL), lambda i, f: (0, 0)))
        args.append(final_norm.reshape(1, D_MODEL))
    return pl.pallas_call(
        functools.partial(_ffn_kernel, nf=nf, final=final),
        grid=(n // tm, nf),
        in_specs=in_specs,
        out_specs=pl.BlockSpec((tm, D_MODEL), lambda i, f: (i, 0)),
        out_shape=jax.ShapeDtypeStruct((n, D_MODEL), F32),
        scratch_shapes=[pltpu.VMEM((tm, D_MODEL), BF16)],
        compiler_params=pltpu.CompilerParams(
            dimension_semantics=("parallel", "arbitrary"), vmem_limit_bytes=VMEM_LIMIT),
        name="ffn_final" if final else "ffn",
    )(*args)


PACK_SHIFT_Z = 2 * M_HEADS
PACK_SHIFT_GA = PACK_SHIFT_Z + S_HEADS
SRC_IF = P_HEAD
SRC_DT = P_HEAD + PACK_SHIFT_Z + S_INNER + S_CONV_CH


def _pack_kernel(a_ref, gif_ref, gdt_ref, o_ref):
    j = pl.program_id(0)

    @pl.when(j < P_SMALL // PACK_TN)
    def _():
        o_ref[...] = a_ref[...].astype(BF16)

    @pl.when(j == P_SMALL // PACK_TN)
    def _():
        pad = jnp.zeros((PACK_TN - SM_DT - S_HEADS, o_ref.shape[1]), F32)
        o_ref[...] = jnp.concatenate([gif_ref[...], gdt_ref[...], pad], axis=0).astype(BF16)


def _pack_w_in(w_t):
    d = w_t.shape[1]
    n_main = P_SMALL // PACK_TN
    last_start = w_t.shape[0] - PACK_TN

    def src_row(j):
        shift = jnp.where(j < P_HEAD // PACK_TN, 0,
                          jnp.where(j < P_GA // PACK_TN, PACK_SHIFT_Z, PACK_SHIFT_GA))
        tile = jnp.minimum((j * PACK_TN + shift) // SUBLANES, last_start // SUBLANES)
        return tile * SUBLANES

    return pl.pallas_call(
        _pack_kernel,
        grid=(n_main + 1,),
        in_specs=[
            pl.BlockSpec((pl.Element(PACK_TN), pl.Element(d)), lambda j: (src_row(j), 0)),
            pl.BlockSpec((pl.Element(SM_DT), pl.Element(d)), lambda j: (SRC_IF, 0)),
            pl.BlockSpec((pl.Element(S_HEADS), pl.Element(d)), lambda j: (SRC_DT, 0)),
        ],
        out_specs=pl.BlockSpec((PACK_TN, d), lambda j: (j, 0)),
        out_shape=jax.ShapeDtypeStruct((P_WIDTH, d), BF16),
        compiler_params=pltpu.CompilerParams(
            dimension_semantics=("arbitrary",), vmem_limit_bytes=VMEM_LIMIT),
        name="pack_w_in",
    )(w_t, w_t, w_t)


def _proj_kernel(x_ref, sh_ref, sc_ref, nw_ref, w_ref, o_ref, u_ref):
    @pl.when(pl.program_id(1) == 0)
    def _():
        tm = x_ref.shape[0]
        gain = nw_ref[...] * (1.0 + _mod_rows(sc_ref, tm))
        u_ref[...] = (_rms(x_ref[...], gain) + _mod_rows(sh_ref, tm)).astype(BF16)

    o_ref[...] = _nt_dot(u_ref[...], w_ref[...])


def _proj(x, mod3, rows_per_group, kmod, norm_w, w_packed):
    n = x.shape[0]
    tm = min(PROJ_TM, n)
    assert n % tm == 0 and (rows_per_group % tm == 0 or tm % rows_per_group == 0)
    tn = PROJ_TN
    return pl.pallas_call(
        _proj_kernel,
        grid=(n // tm, P_WIDTH // tn),
        in_specs=[
            pl.BlockSpec((tm, D_MODEL), lambda i, j: (i, 0)),
            _mod_spec(rows_per_group, tm,kmod),
            _mod_spec(rows_per_group, tm,kmod + 1),
            pl.BlockSpec((1, D_MODEL), lambda i, j: (0, 0)),
            pl.BlockSpec((tn, D_MODEL), lambda i, j: (j, 0)),
        ],
        out_specs=pl.BlockSpec((tm, tn), lambda i, j: (i, j)),
        out_shape=jax.ShapeDtypeStruct((n, P_WIDTH), F32),
        scratch_shapes=[pltpu.VMEM((tm, D_MODEL), BF16)],
        compiler_params=pltpu.CompilerParams(
            dimension_semantics=("parallel", "arbitrary"), vmem_limit_bytes=VMEM_LIMIT),
        name="mix_proj",
    )(x, mod3, mod3, norm_w.reshape(1, D_MODEL), w_packed)


def _conv_silu(xp_ref, x_ref, w_ref, b_ref, L, dst_ref, slabs=None):
    if slabs is None:
        slabs = range(x_ref.shape[1] // LANES)
    for s in slabs:
        cols = slice(s * LANES, (s + 1) * LANES)
        xp, dst = xp_ref.at[s], dst_ref.at[s]
        xp[pl.ds(SUBLANES, L), :] = x_ref[:, cols]
        bias = b_ref[:, cols]
        taps = [w_ref[t:t + 1, cols] for t in range(CONV_W)]
        if L % CONV_ROWS == 0:
            for r in range(0, L, CONV_ROWS):
                for b in range(CONV_STRIDE):
                    acc = bias
                    for back in range(CONV_W):
                        src = pl.ds(SUBLANES + r + b - back, SUBLANES, stride=CONV_STRIDE)
                        acc = acc + xp[src, :] * taps[CONV_W - 1 - back]
                    dst[pl.ds(r + b, SUBLANES, stride=CONV_STRIDE), :] = _silu(acc)
        else:
            x_all = xp[...]
            acc = bias + x_all[SUBLANES:] * taps[CONV_W - 1]
            for back in range(1, CONV_W):
                acc = acc + pltpu.roll(x_all, back, axis=0)[SUBLANES:] * taps[CONV_W - 1 - back]
            dst[...] = _silu(acc)
        xp[pl.ds(0, SUBLANES), :] = xp[pl.ds(L, SUBLANES), :]


def _init_conv(xp_ref, init_ref):
    n_slabs = xp_ref.shape[0]
    xp_ref[:, pl.ds(0, SUBLANES), :] = jnp.zeros((n_slabs, SUBLANES, LANES), F32)
    if init_ref is not None:
        for s in range(n_slabs):
            xp_ref[s, pl.ds(SUBLANES - (CONV_W - 1), CONV_W - 1), :] = (
                init_ref[0, :, s * LANES:(s + 1) * LANES])


def _run_stages(streams):
    live = list(streams)
    while live:
        for stream in list(live):
            try:
                next(stream)
            except StopIteration:
                live.remove(stream)


def _split3(a):
    hi = a.astype(BF16)
    r1 = a - hi.astype(F32)
    mid = r1.astype(BF16)
    lo = (r1 - mid.astype(F32)).astype(BF16)
    return hi, mid, lo


def _nt_dot(a, b):
    return lax.dot_general(a, b, (((1,), (1,)), ((), ())), preferred_element_type=F32)


def _tn_dot(a, b):
    return lax.dot_general(a, b, (((0,), (0,)), ((), ())), preferred_element_type=F32)


def _dot3(mask_b, a):
    return sum(jnp.dot(mask_b, p, preferred_element_type=F32) for p in _split3(a))


def _tn3(a, mask_b):
    return sum(_tn_dot(p, mask_b) for p in _split3(a))


def _log_sigmoid(x):
    return jnp.minimum(x, 0.0) - jnp.log1p(jnp.exp(-jnp.abs(x)))


def _softplus(x):
    return jnp.maximum(x, 0.0) + jnp.log1p(jnp.exp(-jnp.abs(x)))


def _mlstm_kernel(*refs, L, nseq, has_state):
    qk_ref, v_ref, og_ref, sm_ref, sb_ref, hg_ref, w_ref, b_ref = refs[:8]
    refs = refs[8:]
    if has_state:
        cv0_ref, c0_ref, n0_ref, m0_ref = refs[:4]
        refs = refs[4:]
    o_ref, c_ref, n_ref, m_ref, xp_ref, qk_s = refs

    @pl.when(pl.program_id(1) == 0)
    def _():
        if has_state:
            c_ref[...] = c0_ref[...]
            n_ref[...] = n0_ref[...]
            m_ref[...] = m0_ref[...]
        else:
            c_ref[...] = jnp.zeros_like(c_ref)
            n_ref[...] = jnp.zeros_like(n_ref)
            m_ref[...] = jnp.zeros_like(m_ref)
        for s in range(nseq):
            _init_conv(xp_ref.at[s], cv0_ref.at[pl.ds(s, 1)] if has_state else None)

    rr = lax.broadcasted_iota(jnp.int32, (L, L), 0)
    cc = lax.broadcasted_iota(jnp.int32, (L, L), 1)
    tril = cc <= rr
    gates = []
    for s in range(nseq):
        sm = sm_ref[s] + sb_ref[...]
        lf_all = _log_sigmoid(sm)
        b_all = _dot3(tril.astype(BF16), lf_all)
        b_all_t = _tn3(lf_all, (rr <= cc).astype(BF16))
        sm_t = _tn3(sm, (rr == cc).astype(BF16))
        gates.append((sm, sm_t, b_all, b_all_t))

    def head_stages(s, h):
        sm, sm_t, b_all, b_all_t = gates[s]
        qk_seq = qk_s.at[s]
        q_slabs = range(h * M_QK // LANES, (h + 1) * M_QK // LANES)
        k_slabs = range((M_QK_W + h * M_QK) // LANES, (M_QK_W + (h + 1) * M_QK) // LANES)
        _conv_silu(xp_ref.at[s], qk_ref.at[s], w_ref, b_ref, L, qk_seq,
                   slabs=[*q_slabs, *k_slabs])
        yield
        q = jnp.concatenate([qk_seq[i] for i in q_slabs], axis=1)
        k = jnp.concatenate([qk_seq[i] for i in k_slabs], axis=1) * K_SCALE
        v = v_ref[s, :, h * M_V:(h + 1) * M_V]
        i_col = sm[:, SM_I + h:SM_I + h + 1]
        i_row = sm_t[SM_I + h:SM_I + h + 1, :]
        b_col = b_all[:, SM_F + h:SM_F + h + 1]
        b_row = b_all_t[SM_F + h:SM_F + h + 1, :]

        c_prev = c_ref[s, h]
        n_prev = n_ref[s, h]
        m_prev = m_ref[s, h]

        dmat = jnp.where(tril, b_col - b_row + i_row, -jnp.inf)
        inter = b_col + m_prev
        mt = jnp.maximum(inter, jnp.max(dmat, axis=1, keepdims=True))
        w_tok = jnp.exp(dmat - mt)
        w_st = jnp.exp(inter - mt)
        yield

        qb = q.astype(BF16)
        kb = k.astype(BF16)
        vb = v.astype(BF16)
        sc = _nt_dot(qb, kb) * w_tok
        yield
        num = (jnp.dot(sc.astype(BF16), vb, preferred_element_type=F32)
               + w_st * jnp.dot(qb, c_prev.astype(BF16), preferred_element_type=F32))
        den = (jnp.sum(sc, axis=1, keepdims=True)
               + w_st * jnp.sum(q * n_prev, axis=1, keepdims=True))
        yield
        hh = num / jnp.maximum(jnp.abs(den), jnp.exp(-mt))

        m_new = mt[L - 1:L, :]
        b_last = b_col[L - 1:L, :]
        ws = jnp.exp(b_last - b_col + i_col - m_new)
        decay = jnp.exp(b_last + m_prev - m_new)
        kw = k * ws
        c_ref[s, h] = decay * c_prev + _tn_dot(kw.astype(BF16), vb)
        n_ref[s, h] = decay * n_prev + jnp.sum(kw, axis=0, keepdims=True)
        m_ref[s, h] = m_new
        yield

        mu = jnp.mean(hh, axis=-1, keepdims=True)
        hc = hh - mu
        yield
        var = jnp.mean(hc * hc, axis=-1, keepdims=True)
        hsl = slice(h * M_V, (h + 1) * M_V)
        gate = (0.5 * hg_ref[:, hsl]) * (jnp.tanh(0.5 * og_ref[s, :, hsl]) + 1.0)
        o_ref[s, :, hsl] = (hc * lax.rsqrt(var + EPS)) * gate

    _run_stages([head_stages(s, h) for s in range(nseq) for h in range(M_HEADS)])


def _seqs_per_step(bsz, nc, short):
    want = short if nc == 1 else LONG_SEQS_PER_STEP
    return want if bsz % want == 0 else 1


def _mlstm(proj, bsz, T, m_conv_w, m_conv_b, small_bias, m_head_g, state=None):
    L = min(T, CHUNK)
    nc = T // L
    assert T == nc * L
    nseq = _seqs_per_step(bsz, nc, MLSTM_SHORT_SEQS)
    has_state = state is not None
    qkw = 2 * M_QK_W

    const = lambda b, c: (0, 0)
    per_b = lambda b, c: (b, 0, 0, 0)
    in_specs = [
        pl.BlockSpec((nseq, L, qkw), lambda b, c: (b, c, P_QK // qkw)),
        pl.BlockSpec((nseq, L, M_V_W), lambda b, c: (b, c, P_V // M_V_W)),
        pl.BlockSpec((nseq, L, M_V_W), lambda b, c: (b, c, P_O // M_V_W)),
        pl.BlockSpec((nseq, L, LANES), lambda b, c: (b, c, P_SMALL // LANES)),
        pl.BlockSpec((1, LANES), const),
        pl.BlockSpec((1, M_V_W), const),
        pl.BlockSpec((CONV_W, qkw), const),
        pl.BlockSpec((1, qkw), const),
    ]
    args = [proj, proj, proj, proj, small_bias, m_head_g.reshape(1, -1),
            m_conv_w, m_conv_b.reshape(1, -1)]
    scratch = [pltpu.VMEM((nseq, qkw // LANES, L + SUBLANES, LANES), F32),
               pltpu.VMEM((nseq, qkw // LANES, L, LANES), F32)]
    if has_state:
        conv0, c0, n0, m0 = state
        in_specs += [
            pl.BlockSpec((nseq, CONV_W - 1, qkw), lambda b, c: (b, 0, 0)),
            pl.BlockSpec((nseq, M_HEADS, M_QK, M_V), per_b),
            pl.BlockSpec((nseq, M_HEADS, 1, M_QK), per_b),
            pl.BlockSpec((nseq, M_HEADS, 1, 1), per_b),
        ]
        args += [conv0, c0, n0.reshape(bsz, M_HEADS, 1, M_QK), m0.reshape(bsz, M_HEADS, 1, 1)]
    out_shape = (
        jax.ShapeDtypeStruct((bsz, T, M_V_W), F32),
        jax.ShapeDtypeStruct((bsz, M_HEADS, M_QK, M_V), F32),
        jax.ShapeDtypeStruct((bsz, M_HEADS, 1, M_QK), F32),
        jax.ShapeDtypeStruct((bsz, M_HEADS, 1, 1), F32),
    )
    out_specs = (
        pl.BlockSpec((nseq, L, M_V_W), lambda b, c: (b, c, 0)),
        pl.BlockSpec((nseq, M_HEADS, M_QK, M_V), per_b),
        pl.BlockSpec((nseq, M_HEADS, 1, M_QK), per_b),
        pl.BlockSpec((nseq, M_HEADS, 1, 1), per_b),
    )
    out, c1, n1, m1 = pl.pallas_call(
        functools.partial(_mlstm_kernel, L=L, nseq=nseq, has_state=has_state),
        grid=(bsz // nseq, nc),
        in_specs=in_specs,
        out_specs=out_specs,
        out_shape=out_shape,
        scratch_shapes=scratch,
        compiler_params=pltpu.CompilerParams(
            dimension_semantics=("arbitrary", "arbitrary"), vmem_limit_bytes=VMEM_LIMIT),
        name="mlstm_state" if has_state else "mlstm",
    )(*args)
    return out, c1, n1.reshape(bsz, M_HEADS, M_QK), m1.reshape(bsz, M_HEADS)


def _ssd_kernel(*refs, L, nseq, has_state):
    n_shared = 13
    streams = []
    for sq in range(nseq):
        one = pl.ds(sq, 1)
        views = [r.at[sq] for r in refs[:5]] + list(refs[5:n_shared])
        rest = refs[n_shared:]
        if has_state:
            views += [r.at[one] for r in rest[:2]]
            rest = rest[2:]
        o_ref, h_ref = rest[:2]
        views += [o_ref.at[sq], h_ref.at[one]] + [r.at[sq] for r in rest[2:]]
        streams += _ssd_seq_streams(*views, L=L, has_state=has_state)
    _run_stages(streams)


def _ssd_seq_streams(*refs, L, has_state):
    (z_ref, xs_ref, bp_ref, cp_ref, sm_ref, sb_ref, al_ref, d_ref, ng_ref, e_ref, et_ref,
     w_ref, b_ref) = refs[:13]
    refs = refs[13:]
    if has_state:
        cv0_ref, h0_ref = refs[:2]
        refs = refs[2:]
    o_ref, h_ref, xx_ref, xb_ref, xc_ref, xs_s, bm_s, cm_s = refs

    @pl.when(pl.program_id(1) == 0)
    def _():
        if has_state:
            h_ref[...] = h0_ref[...]
        else:
            h_ref[...] = jnp.zeros_like(h_ref)

    x_cols = slice(0, S_INNER)
    b_cols = slice(S_INNER, S_INNER + S_BCW)
    c_cols = slice(S_INNER + S_BCW, S_CONV_CH)
    for xp_ref, cols in ((xx_ref, x_cols), (xb_ref, b_cols), (xc_ref, c_cols)):
        @pl.when(pl.program_id(1) == 0)
        def _():
            _init_conv(xp_ref, cv0_ref.at[:, :, cols] if has_state else None)

    _conv_silu(xb_ref, bp_ref, w_ref.at[:, b_cols], b_ref.at[:, b_cols], L, bm_s)
    _conv_silu(xc_ref, cp_ref, w_ref.at[:, c_cols], b_ref.at[:, c_cols], L, cm_s)

    dt_all = _softplus(sm_ref[...] + sb_ref[...])
    da_all = dt_all * (-jnp.exp(al_ref[...]))
    lane = lax.broadcasted_iota(jnp.int32, (L, LANES), 1)
    rr = lax.broadcasted_iota(jnp.int32, (L, L), 0)
    cc = lax.broadcasted_iota(jnp.int32, (L, L), 1)
    tril = cc <= rr
    lo = lane < S_HEADDIM
    cum_all = _dot3(tril.astype(BF16), da_all)
    cum_all_t = _tn3(da_all, (rr <= cc).astype(BF16))
    cum2, cum2_t = cum_all * LOG2_E, cum_all_t * LOG2_E
    cum_last = cum_all[L - 1:L, :]
    dec_all = dt_all * jnp.exp(cum_last - cum_all)
    ecum_all = jnp.exp(cum_all)
    mxu_spread = L >= LANES
    if mxu_spread:
        dt_src = _split3(dt_all)[:2]
        dec_src = _split3(dec_all)[:2]
        ecum_src = _split3(ecum_all)[:2]
        clast_src = _split3(jnp.broadcast_to(cum_all_t[:, L - 1:L], (LANES, LANES)))
    else:
        dt_src, dec_src, ecum_src = dt_all, dec_all, ecum_all
        lo_row = lax.broadcasted_iota(jnp.int32, (LANES, 1), 0) < S_HEADDIM

    def spread(src, pair):
        if mxu_spread:
            return sum(jnp.dot(p, e_ref[pair], preferred_element_type=F32) for p in src)
        idx = SM_DT + 2 * pair
        return jnp.where(lo, src[:, idx:idx + 1], src[:, idx + 1:idx + 2])

    def state_decay(pair):
        if mxu_spread:
            return jnp.exp(sum(jnp.dot(et_ref[pair], p, preferred_element_type=F32)
                               for p in clast_src))
        idx = SM_DT + 2 * pair
        return jnp.exp(jnp.where(lo_row, cum_last[:, idx:idx + 1], cum_last[:, idx + 1:idx + 2]))

    def group_stages(g):
        _conv_silu(xx_ref, xs_ref, w_ref.at[:, x_cols], b_ref.at[:, x_cols], L, xs_s,
                   slabs=range(g * S_GW // LANES, (g + 1) * S_GW // LANES))
        yield
        bmb = bm_s[g].astype(BF16)
        cmb = cm_s[g].astype(BF16)
        cb = _nt_dot(cmb, bmb)
        ssq = jnp.zeros((L, 1), F32)
        for p in range(S_HPG // 2):
            yield
            pair = g * (S_HPG // 2) + p
            psl = slice(g * S_GW + p * LANES, g * S_GW + (p + 1) * LANES)
            hsl = slice(p * LANES, (p + 1) * LANES)
            lmat = []
            for e in range(2):
                idx = SM_DT + 2 * pair + e
                lmat.append(jnp.exp2(jnp.where(
                    tril, cum2[:, idx:idx + 1] - cum2_t[idx:idx + 1, :], -jnp.inf)))
            yield
            xs = xs_s[pair]
            xdt = xs * spread(dt_src, pair)
            xdd = xs * spread(dec_src, pair)
            h_prev = h_ref[0, g, hsl, :]
            yield
            xdt_b = xdt.astype(BF16)
            y = (jnp.where(lo,
                           jnp.dot((cb * lmat[0]).astype(BF16), xdt_b,
                                   preferred_element_type=F32),
                           jnp.dot((cb * lmat[1]).astype(BF16), xdt_b,
                                   preferred_element_type=F32))
                 + spread(ecum_src, pair) * _nt_dot(cmb, h_prev.astype(BF16)))
            h_ref[0, g, hsl, :] = state_decay(pair) * h_prev + _tn_dot(xdd.astype(BF16), bmb)
            yield
            gated = (y + d_ref[:, psl] * xs) * _silu(z_ref[:, psl])
            ssq = ssq + jnp.sum(gated * gated, axis=1, keepdims=True)
            o_ref[:, psl] = gated
        yield
        gsl = slice(g * S_GW, (g + 1) * S_GW)
        o_ref[:, gsl] = o_ref[:, gsl] * lax.rsqrt(ssq * (1.0 / S_GW) + EPS) * ng_ref[:, gsl]

    return [group_stages(g) for g in range(S_GROUPS)]


def _pair_onehots():
    n_pairs = S_HEADS // 2
    pair = jnp.arange(n_pairs)[:, None, None]
    src = jnp.arange(LANES)[None, :, None]
    dst = jnp.arange(LANES)[None, None, :]
    onehot = src == SM_DT + 2 * pair + dst // S_HEADDIM
    return onehot.astype(BF16), jnp.swapaxes(onehot, 1, 2).astype(BF16)


def _ssd(proj, bsz, T, s_conv_w, s_conv_b, small_bias, alog_row, d_row, s_norm_g, state=None):
    L = min(T, CHUNK)
    nc = T // L
    assert T == nc * L
    nseq = _seqs_per_step(bsz, nc, SSD_SHORT_SEQS)
    has_state = state is not None
    n_pairs = S_HEADS // 2
    onehot, onehot_t = _pair_onehots()

    const = lambda b, c: (0, 0)
    const3 = lambda b, c: (0, 0, 0)
    in_specs = [
        pl.BlockSpec((nseq, L, S_INNER), lambda b, c: (b, c, P_Z // S_INNER)),
        pl.BlockSpec((nseq, L, S_INNER), lambda b, c: (b, c, P_XBC // S_INNER)),
        pl.BlockSpec((nseq, L, S_BCW), lambda b, c: (b, c, P_B // S_BCW)),
        pl.BlockSpec((nseq, L, S_BCW), lambda b, c: (b, c, P_C // S_BCW)),
        pl.BlockSpec((nseq, L, LANES), lambda b, c: (b, c, P_SMALL // LANES)),
        pl.BlockSpec((1, LANES), const),
        pl.BlockSpec((1, LANES), const),
        pl.BlockSpec((1, S_INNER), const),
        pl.BlockSpec((1, S_INNER), const),
        pl.BlockSpec((n_pairs, LANES, LANES), const3),
        pl.BlockSpec((n_pairs, LANES, LANES), const3),
        pl.BlockSpec((CONV_W, S_CONV_CH), const),
        pl.BlockSpec((1, S_CONV_CH), const),
    ]
    args = [proj, proj, proj, proj, proj, small_bias, alog_row, d_row, s_norm_g.reshape(1, -1),
            onehot, onehot_t, s_conv_w, s_conv_b.reshape(1, -1)]
    scratch = [pltpu.VMEM((nseq, width // LANES, rows, LANES), F32)
               for rows in (L + SUBLANES, L) for width in (S_INNER, S_BCW, S_BCW)]
    if has_state:
        conv0, h0 = state
        in_specs += [
            pl.BlockSpec((nseq, CONV_W - 1, S_CONV_CH), lambda b, c: (b, 0, 0)),
            pl.BlockSpec((nseq, S_GROUPS, S_GW, S_STATE), lambda b, c: (b, 0, 0, 0)),
        ]
        args += [conv0, h0.reshape(bsz, S_GROUPS, S_GW, S_STATE)]
    out, h1 = pl.pallas_call(
        functools.partial(_ssd_kernel, L=L, nseq=nseq, has_state=has_state),
        grid=(bsz // nseq, nc),
        in_specs=in_specs,
        out_specs=(
            pl.BlockSpec((nseq, L, S_INNER), lambda b, c: (b, c, 0)),
            pl.BlockSpec((nseq, S_GROUPS, S_GW, S_STATE), lambda b, c: (b, 0, 0, 0)),
        ),
        out_shape=(
            jax.ShapeDtypeStruct((bsz, T, S_INNER), F32),
            jax.ShapeDtypeStruct((bsz, S_GROUPS, S_GW, S_STATE), F32),
        ),
        scratch_shapes=scratch,
        compiler_params=pltpu.CompilerParams(
            dimension_semantics=("arbitrary", "arbitrary"), vmem_limit_bytes=VMEM_LIMIT),
        name="ssd_state" if has_state else "ssd",
    )(*args)
    return out, h1.reshape(bsz, S_HEADS, S_HEADDIM, S_STATE)


def _merge_kernel(x_ref, hm_ref, gt_ref, ga_ref, gb_ref, g_ref, wpa_ref, wpb_ref, wo_ref, o_ref,
                  wpa_s, wpb_s, wo_s):
    @pl.when(pl.program_id(0) == 0)
    def _():
        wpa_s[...] = wpa_ref[...].astype(BF16)
        wpb_s[...] = wpb_ref[...].astype(BF16)
        wo_s[...] = wo_ref[...].astype(BF16)

    ya = jnp.dot(hm_ref[...].astype(BF16), wpa_s[...], preferred_element_type=F32)
    yb = jnp.dot(gt_ref[...].astype(BF16), wpb_s[...], preferred_element_type=F32)
    merged = _sigmoid(ga_ref[...]) * ya + _sigmoid(gb_ref[...]) * yb
    o_ref[...] = x_ref[...] + _mod_rows(g_ref, x_ref.shape[0]) * jnp.dot(
        merged.astype(BF16), wo_s[...], preferred_element_type=F32)


def _merge(x, hm, gated, proj, mod3, rows_per_group, kmod, w_pa, w_pb, w_out):
    n = x.shape[0]
    tm = MERGE_TM
    assert n % tm == 0 and (rows_per_group % tm == 0 or tm % rows_per_group == 0)
    const = lambda i: (0, 0)
    resident = functools.partial(pl.BlockSpec, index_map=const, pipeline_mode=pl.Buffered(1))
    return pl.pallas_call(
        _merge_kernel,
        grid=(n // tm,),
        in_specs=[
            pl.BlockSpec((tm, D_MODEL), lambda i: (i, 0)),
            pl.BlockSpec((tm, M_V_W), lambda i: (i, 0)),
            pl.BlockSpec((tm, S_INNER), lambda i: (i, 0)),
            pl.BlockSpec((tm, D_MODEL), lambda i: (i, P_GA // D_MODEL)),
            pl.BlockSpec((tm, D_MODEL), lambda i: (i, P_GB // D_MODEL)),
            _mod_spec(rows_per_group, tm,kmod),
            resident((M_V_W, D_MODEL)),
            resident((S_INNER, D_MODEL)),
            resident((D_MODEL, D_MODEL)),
        ],
        out_specs=pl.BlockSpec((tm, D_MODEL), lambda i: (i, 0)),
        out_shape=jax.ShapeDtypeStruct((n, D_MODEL), F32),
        scratch_shapes=[pltpu.VMEM((M_V_W, D_MODEL), BF16),
                        pltpu.VMEM((S_INNER, D_MODEL), BF16),
                        pltpu.VMEM((D_MODEL, D_MODEL), BF16)],
        compiler_params=pltpu.CompilerParams(
            dimension_semantics=("arbitrary",), vmem_limit_bytes=VMEM_LIMIT),
        name="merge",
    )(x, hm, gated, proj, proj, mod3, w_pa, w_pb, w_out)


def _layer(x, mod3, rows_per_group, bsz, T, wts, final_norm, state):
    n = bsz * T
    x = x.reshape(n, D_MODEL)
    x = _ffn(x, mod3, rows_per_group, 0, wts["norm_ff1"], wts["ff1_w1"], wts["ff1_w3"],
             wts["ff1_w2"])
    proj = _proj(x, mod3, rows_per_group, 3, wts["norm_mix"], wts["w_packed"])
    proj3 = proj.reshape(bsz, T, P_WIDTH)
    mixer_args = (proj3, bsz, T)
    m_args = (wts["m_conv_w"], wts["m_conv_b"], wts["small_bias"], wts["m_head_g"])
    s_args = (wts["s_conv_w"], wts["s_conv_b"], wts["small_bias"], wts["alog_row"],
              wts["d_row"], wts["s_norm_g"])
    if state is None:
        m_state = s_state = None
    else:
        c0, n0, m0, mconv0, ssm0, sconv0 = state
        m_state = (mconv0, c0, n0, m0)
        s_state = (sconv0, ssm0)
    hm, c1, n1, m1 = _mlstm(*mixer_args, *m_args, state=m_state)
    gated, h1 = _ssd(*mixer_args, *s_args, state=s_state)
    x = _merge(x, hm.reshape(n, M_V_W), gated.reshape(n, S_INNER), proj, mod3, rows_per_group,
               5, wts["w_pa"], wts["w_pb"], wts["w_out"])
    y = _ffn(x, mod3, rows_per_group, 6, wts["norm_ff2"], wts["ff2_w1"], wts["ff2_w3"],
             wts["ff2_w2"], final_norm=final_norm)
    last = proj3[:, T - (CONV_W - 1):]
    mconv1 = last[:, :, P_QK:P_QK + 2 * M_QK_W]
    sconv1 = last[:, :, P_XBC:P_XBC + S_CONV_CH]
    return y.reshape(bsz, T, D_MODEL), (c1, n1, m1, mconv1, h1, sconv1)


def kernel(x_prompt, x_sample, c_prompt, c_sample, state_mlstm_C, state_mlstm_n, state_mlstm_m, state_mlstm_conv, state_ssm, state_ssm_conv, ada_w, ada_b, norm_ff1, ff1_w1, ff1_w3, ff1_w2, norm_mix, w_in, m_conv_w, m_conv_b, m_if_b, m_head_g, w_pa, s_conv_w, s_conv_b, s_dt_bias, s_A_log, s_D, s_norm_g, w_pb, w_out, norm_ff2, ff2_w1, ff2_w3, ff2_w2, final_norm):
    bp, tp, _ = x_prompt.shape
    bs, ts, _ = x_sample.shape
    depth = ada_w.shape[0]
    assert depth == 1

    hp, hs = x_prompt, x_sample
    new_p, new_s = [], []
    for l in range(depth):
        c_all = jnp.concatenate([c_prompt, c_sample], axis=0)
        pad_rows = (-c_all.shape[0]) % (2 * SUBLANES)
        c_all = jnp.pad(c_all, ((0, pad_rows), (0, 0)))
        mod = _ada(c_all, ada_w[l], ada_b[l])
        mod_p = mod[:bp].reshape(bp, 1, N_MOD * D_MODEL)
        mod_s = mod[bp:bp + bs].reshape(bs, 1, N_MOD * D_MODEL)

        lane_pad = LANES - (2 * M_HEADS + S_HEADS)
        small_bias = jnp.concatenate(
            [m_if_b[l], s_dt_bias[l], jnp.zeros((lane_pad,), F32)]).reshape(1, LANES)
        alog_row = jnp.concatenate(
            [jnp.zeros((2 * M_HEADS,), F32), s_A_log[l], jnp.zeros((lane_pad,), F32)]
        ).reshape(1, LANES)
        wts = {
            "norm_ff1": norm_ff1[l], "ff1_w1": ff1_w1[l], "ff1_w3": ff1_w3[l],
            "ff1_w2": ff1_w2[l],
            "norm_mix": norm_mix[l], "w_packed": _pack_w_in(jnp.swapaxes(w_in, 1, 2)[l]),
            "m_conv_w": m_conv_w[l], "m_conv_b": m_conv_b[l], "small_bias": small_bias,
            "m_head_g": m_head_g[l], "w_pa": w_pa[l],
            "s_conv_w": s_conv_w[l], "s_conv_b": s_conv_b[l], "alog_row": alog_row,
            "d_row": jnp.repeat(s_D[l], S_HEADDIM).reshape(1, S_INNER),
            "s_norm_g": s_norm_g[l], "w_pb": w_pb[l], "w_out": w_out[l],
            "norm_ff2": norm_ff2[l], "ff2_w1": ff2_w1[l], "ff2_w3": ff2_w3[l],
            "ff2_w2": ff2_w2[l],
        }
        st_s = (state_mlstm_C[l], state_mlstm_n[l], state_mlstm_m[l], state_mlstm_conv[l],
                state_ssm[l], state_ssm_conv[l])
        hp, sp = _layer(hp, mod_p, tp, bp, tp, wts, final_norm, None)
        hs, ss = _layer(hs, mod_s, ts, bs, ts, wts, final_norm, st_s)
        new_p.append(sp)
        new_s.append(ss)
    outs_p = [jnp.stack([s[i] for s in new_p]) for i in range(6)]
    outs_s = [jnp.stack([s[i] for s in new_s]) for i in range(6)]
    return (hp, hs, *outs_p, *outs_s)
```

```python
import functools

import jax
import jax.numpy as jnp
from jax import lax
from jax.experimental import pallas as pl
from jax.experimental.pallas import tpu as pltpu

F32 = jnp.float32
BF16 = jnp.bfloat16

D_MODEL = 1024
M_HEADS = 4
M_QK = 256
M_V = 512
M_QK_W = M_HEADS * M_QK
M_V_W = M_HEADS * M_V
S_INNER = 2048
S_HEADDIM = 64
S_HEADS = 32
S_GROUPS = 4
S_HPG = S_HEADS // S_GROUPS
S_STATE = 128
S_GW = S_INNER // S_GROUPS
S_BCW = S_GROUPS * S_STATE
S_CONV_CH = S_INNER + 2 * S_BCW
CONV_W = 4
D_FF = 2816
CHUNK = 128
EPS = 1e-6
N_MOD = 9
K_SCALE = M_QK ** -0.5
LOG2_E = 1.4426950408889634

LANES = 128
SUBLANES = 8
VMEM_LIMIT = 56 * 1024 * 1024

P_QK = 0
P_V = 2 * M_QK_W
P_O = P_V + M_V_W
P_HEAD = P_O + M_V_W
P_Z = P_HEAD
P_XBC = P_Z + S_INNER
P_B = P_XBC + S_INNER
P_C = P_B + S_BCW
P_GA = P_XBC + S_CONV_CH
P_GB = P_GA + D_MODEL
P_SMALL = P_GB + D_MODEL
PROJ_TN = 2304
PACK_TN = 512
P_WIDTH = 13824
SM_I = 0
SM_F = M_HEADS
SM_DT = 2 * M_HEADS

ADA_TN = 2304
FFN_TM = 2048
FFN_TF = 256
PROJ_TM = 1024
MERGE_TM = 256
MLSTM_INTERLEAVE = 8
MLSTM_SHORT_SEQS = 4
SSD_SHORT_SEQS = 8
LONG_SEQS_PER_STEP = 1
CONV_STRIDE = 4
CONV_ROWS = SUBLANES * CONV_STRIDE


def _sigmoid(x):
    return 0.5 * jnp.tanh(0.5 * x) + 0.5


def _silu(x):
    h = 0.5 * x
    return h * jnp.tanh(h) + h


def _rms(x, g):
    return x * lax.rsqrt(jnp.mean(x * x, axis=-1, keepdims=True) + EPS) * g


def _ada_kernel(c_ref, w_ref, b_ref, o_ref):
    s = _silu(c_ref[...]).astype(BF16)
    o_ref[...] = jnp.dot(s, w_ref[...].astype(BF16), preferred_element_type=F32) + b_ref[...]


def _ada(c_all, ada_w, ada_b):
    m = c_all.shape[0]
    n = ada_w.shape[1]
    tn = ADA_TN
    assert n % tn == 0
    return pl.pallas_call(
        _ada_kernel,
        grid=(n // tn,),
        in_specs=[
            pl.BlockSpec((m, D_MODEL), lambda j: (0, 0)),
            pl.BlockSpec((D_MODEL, tn), lambda j: (0, j)),
            pl.BlockSpec((1, tn), lambda j: (0, j)),
        ],
        out_specs=pl.BlockSpec((m, tn), lambda j: (0, j)),
        out_shape=jax.ShapeDtypeStruct((m, n), F32),
        compiler_params=pltpu.CompilerParams(
            dimension_semantics=("arbitrary",), vmem_limit_bytes=VMEM_LIMIT),
        name="ada_mod",
    )(c_all, ada_w, ada_b.reshape(1, n))


def _ffn_kernel(x_ref, sh_ref, sc_ref, g_ref, nw_ref, w1_ref, w3_ref, w2_ref, *rest,
                nf, final):
    if final:
        fn_ref, o_ref, u_ref = rest
    else:
        o_ref, u_ref = rest
    f = pl.program_id(1)

    @pl.when(f == 0)
    def _():
        tm = x_ref.shape[0]
        gain = nw_ref[...] * (1.0 + _mod_rows(sc_ref, tm))
        u_ref[...] = (_rms(x_ref[...], gain) + _mod_rows(sh_ref, tm)).astype(BF16)
        o_ref[...] = jnp.zeros_like(o_ref)

    u = u_ref[...]
    h1 = jnp.dot(u, w1_ref[...].astype(BF16), preferred_element_type=F32)
    h3 = jnp.dot(u, w3_ref[...].astype(BF16), preferred_element_type=F32)
    a = (_silu(h1) * h3).astype(BF16)
    o_ref[...] += jnp.dot(a, w2_ref[...].astype(BF16), preferred_element_type=F32)

    @pl.when(f == nf - 1)
    def _():
        y = x_ref[...] + (0.5 * _mod_rows(g_ref, x_ref.shape[0])) * o_ref[...]
        if final:
            y = _rms(y, fn_ref[...])
        o_ref[...] = y


def _mod_spec(rows_per_group, tm, k):
    if rows_per_group >= tm:
        tiles_per_group = rows_per_group // tm
        return pl.BlockSpec((1, 1, D_MODEL), lambda i, *_: (i // tiles_per_group, 0, k))
    groups = tm // rows_per_group
    return pl.BlockSpec((groups, 1, D_MODEL), lambda i, *_: (i, 0, k))


def _mod_rows(ref, tm):
    groups = ref.shape[0]
    if groups == 1:
        return ref[0]
    per_row = jnp.broadcast_to(ref[...], (groups, tm // groups, D_MODEL))
    return per_row.reshape(tm, D_MODEL)


def _ffn(x, mod3, rows_per_group, kmod, norm_w, w1, w3, w2, final_norm=None):
    n = x.shape[0]
    tm = min(FFN_TM, n)
    assert n % tm == 0 and (rows_per_group % tm == 0 or tm % rows_per_group == 0)
    tf = FFN_TF
    nf = D_FF // tf
    final = final_norm is not None
    in_specs = [
        pl.BlockSpec((tm, D_MODEL), lambda i, f: (i, 0)),
        _mod_spec(rows_per_group, tm,kmod),
        _mod_spec(rows_per_group, tm,kmod + 1),
        _mod_spec(rows_per_group, tm,kmod + 2),
        pl.BlockSpec((1, D_MODEL), lambda i, f: (0, 0)),
        pl.BlockSpec((D_MODEL, tf), lambda i, f: (0, f)),
        pl.BlockSpec((D_MODEL, tf), lambda i, f: (0, f)),
        pl.BlockSpec((tf, D_MODEL), lambda i, f: (f, 0)),
    ]
    args = [x, mod3, mod3, mod3, norm_w.reshape(1, D_MODEL), w1, w3, w2]
    if final:
        in_specs.append(pl.BlockSpec((1, D_MODEL), lambda i, f: (0, 0)))
        args.append(final_norm.reshape(1, D_MODEL))
    return pl.pallas_call(
        functools.partial(_ffn_kernel, nf=nf, final=final),
        grid=(n // tm, nf),
        in_specs=in_specs,
        out_specs=pl.BlockSpec((tm, D_MODEL), lambda i, f: (i, 0)),
        out_shape=jax.ShapeDtypeStruct((n, D_MODEL), F32),
        scratch_shapes=[pltpu.VMEM((tm, D_MODEL), BF16)],
        compiler_params=pltpu.CompilerParams(
            dimension_semantics=("parallel", "arbitrary"), vmem_limit_bytes=VMEM_LIMIT),
        name="ffn_final" if final else "ffn",
    )(*args)


PACK_SHIFT_Z = 2 * M_HEADS
PACK_SHIFT_GA = PACK_SHIFT_Z + S_HEADS
SRC_IF = P_HEAD
SRC_DT = P_HEAD + PACK_SHIFT_Z + S_INNER + S_CONV_CH


def _pack_kernel(a_ref, gif_ref, gdt_ref, o_ref):
    j = pl.program_id(0)

    @pl.when(j < P_SMALL // PACK_TN)
    def _():
        o_ref[...] = a_ref[...].astype(BF16)

    @pl.when(j == P_SMALL // PACK_TN)
    def _():
        pad = jnp.zeros((PACK_TN - SM_DT - S_HEADS, o_ref.shape[1]), F32)
        o_ref[...] = jnp.concatenate([gif_ref[...], gdt_ref[...], pad], axis=0).astype(BF16)


def _pack_w_in(w_t):
    d = w_t.shape[1]
    n_main = P_SMALL // PACK_TN
    last_start = w_t.shape[0] - PACK_TN

    def src_row(j):
        shift = jnp.where(j < P_HEAD // PACK_TN, 0,
                          jnp.where(j < P_GA // PACK_TN, PACK_SHIFT_Z, PACK_SHIFT_GA))
        tile = jnp.minimum((j * PACK_TN + shift) // SUBLANES, last_start // SUBLANES)
        return tile * SUBLANES

    return pl.pallas_call(
        _pack_kernel,
        grid=(n_main + 1,),
        in_specs=[
            pl.BlockSpec((pl.Element(PACK_TN), pl.Element(d)), lambda j: (src_row(j), 0)),
            pl.BlockSpec((pl.Element(SM_DT), pl.Element(d)), lambda j: (SRC_IF, 0)),
            pl.BlockSpec((pl.Element(S_HEADS), pl.Element(d)), lambda j: (SRC_DT, 0)),
        ],
        out_specs=pl.BlockSpec((PACK_TN, d), lambda j: (j, 0)),
        out_shape=jax.ShapeDtypeStruct((P_WIDTH, d), BF16),
        compiler_params=pltpu.CompilerParams(
            dimension_semantics=("arbitrary",), vmem_limit_bytes=VMEM_LIMIT),
        name="pack_w_in",
    )(w_t, w_t, w_t)


def _proj_kernel(x_ref, sh_ref, sc_ref, nw_ref, w_ref, o_ref, u_ref):
    @pl.when(pl.program_id(1) == 0)
    def _():
        tm = x_ref.shape[0]
        gain = nw_ref[...] * (1.0 + _mod_rows(sc_ref, tm))
        u_ref[...] = (_rms(x_ref[...], gain) + _mod_rows(sh_ref, tm)).astype(BF16)

    o_ref[...] = _nt_dot(u_ref[...], w_ref[...])


def _proj(x, mod3, rows_per_group, kmod, norm_w, w_packed):
    n = x.shape[0]
    tm = min(PROJ_TM, n)
    assert n % tm == 0 and (rows_per_group % tm == 0 or tm % rows_per_group == 0)
    tn = PROJ_TN
    return pl.pallas_call(
        _proj_kernel,
        grid=(n // tm, P_WIDTH // tn),
        in_specs=[
            pl.BlockSpec((tm, D_MODEL), lambda i, j: (i, 0)),
            _mod_spec(rows_per_group, tm,kmod),
            _mod_spec(rows_per_group, tm,kmod + 1),
            pl.BlockSpec((1, D_MODEL), lambda i, j: (0, 0)),
            pl.BlockSpec((tn, D_MODEL), lambda i, j: (j, 0)),
        ],
        out_specs=pl.BlockSpec((tm, tn), lambda i, j: (i, j)),
        out_shape=jax.ShapeDtypeStruct((n, P_WIDTH), F32),
        scratch_shapes=[pltpu.VMEM((tm, D_MODEL), BF16)],
        compiler_params=pltpu.CompilerParams(
            dimension_semantics=("parallel", "arbitrary"), vmem_limit_bytes=VMEM_LIMIT),
        name="mix_proj",
    )(x, mod3, mod3, norm_w.reshape(1, D_MODEL), w_packed)


def _conv_silu(xp_ref, x_ref, w_ref, b_ref, L, dst_ref, slabs=None):
    if slabs is None:
        slabs = range(x_ref.shape[1] // LANES)
    for s in slabs:
        cols = slice(s * LANES, (s + 1) * LANES)
        xp, dst = xp_ref.at[s], dst_ref.at[s]
        xp[pl.ds(SUBLANES, L), :] = x_ref[:, cols]
        bias = b_ref[:, cols]
        taps = [w_ref[t:t + 1, cols] for t in range(CONV_W)]
        if L % CONV_ROWS == 0:
            for r in range(0, L, CONV_ROWS):
                for b in range(CONV_STRIDE):
                    acc = bias
                    for back in range(CONV_W):
                        src = pl.ds(SUBLANES + r + b - back, SUBLANES, stride=CONV_STRIDE)
                        acc = acc + xp[src, :] * taps[CONV_W - 1 - back]
                    dst[pl.ds(r + b, SUBLANES, stride=CONV_STRIDE), :] = _silu(acc)
        else:
            x_all = xp[...]
            acc = bias + x_all[SUBLANES:] * taps[CONV_W - 1]
            for back in range(1, CONV_W):
                acc = acc + pltpu.roll(x_all, back, axis=0)[SUBLANES:] * taps[CONV_W - 1 - back]
            dst[...] = _silu(acc)
        xp[pl.ds(0, SUBLANES), :] = xp[pl.ds(L, SUBLANES), :]


def _init_conv(xp_ref, init_ref):
    n_slabs = xp_ref.shape[0]
    xp_ref[:, pl.ds(0, SUBLANES), :] = jnp.zeros((n_slabs, SUBLANES, LANES), F32)
    if init_ref is not None:
        for s in range(n_slabs):
            xp_ref[s, pl.ds(SUBLANES - (CONV_W - 1), CONV_W - 1), :] = (
                init_ref[0, :, s * LANES:(s + 1) * LANES])


def _run_stages(streams):
    live = list(streams)
    while live:
        for stream in list(live):
            try:
                next(stream)
            except StopIteration:
                live.remove(stream)


def _split3(a):
    hi = a.astype(BF16)
    r1 = a - hi.astype(F32)
    mid = r1.astype(BF16)
    lo = (r1 - mid.astype(F32)).astype(BF16)
    return hi, mid, lo


def _nt_dot(a, b):
    return lax.dot_general(a, b, (((1,), (1,)), ((), ())), preferred_element_type=F32)


def _tn_dot(a, b):
    return lax.dot_general(a, b, (((0,), (0,)), ((), ())), preferred_element_type=F32)


def _dot3(mask_b, a):
    return sum(jnp.dot(mask_b, p, preferred_element_type=F32) for p in _split3(a))


def _tn3(a, mask_b):
    return sum(_tn_dot(p, mask_b) for p in _split3(a))


def _log_sigmoid(x):
    return jnp.minimum(x, 0.0) - jnp.log1p(jnp.exp(-jnp.abs(x)))


def _softplus(x):
    return jnp.maximum(x, 0.0) + jnp.log1p(jnp.exp(-jnp.abs(x)))


def _mlstm_kernel(*refs, L, nseq, has_state):
    qk_ref, v_ref, og_ref, sm_ref, sb_ref, hg_ref, w_ref, b_ref = refs[:8]
    refs = refs[8:]
    if has_state:
        cv0_ref, c0_ref, n0_ref, m0_ref = refs[:4]
        refs = refs[4:]
    o_ref, c_ref, n_ref, m_ref, xp_ref, qk_s = refs

    @pl.when(pl.program_id(1) == 0)
    def _():
        if has_state:
            c_ref[...] = c0_ref[...]
            n_ref[...] = n0_ref[...]
            m_ref[...] = m0_ref[...]
        else:
            c_ref[...] = jnp.zeros_like(c_ref)
            n_ref[...] = jnp.zeros_like(n_ref)
            m_ref[...] = jnp.zeros_like(m_ref)
        for s in range(nseq):
            _init_conv(xp_ref.at[s], cv0_ref.at[pl.ds(s, 1)] if has_state else None)

    rr = lax.broadcasted_iota(jnp.int32, (L, L), 0)
    cc = lax.broadcasted_iota(jnp.int32, (L, L), 1)
    tril = cc <= rr
    gates = []
    for s in range(nseq):
        sm = sm_ref[s] + sb_ref[...]
        lf_all = _log_sigmoid(sm)
        b_all = _dot3(tril.astype(BF16), lf_all)
        b_all_t = _tn3(lf_all, (rr <= cc).astype(BF16))
        sm_t = _tn3(sm, (rr == cc).astype(BF16))
        gates.append((sm, sm_t, b_all, b_all_t))

    def head_stages(s, h):
        sm, sm_t, b_all, b_all_t = gates[s]
        qk_seq = qk_s.at[s]
        q_slabs = range(h * M_QK // LANES, (h + 1) * M_QK // LANES)
        k_slabs = range((M_QK_W + h * M_QK) // LANES, (M_QK_W + (h + 1) * M_QK) // LANES)
        _conv_silu(xp_ref.at[s], qk_ref.at[s], w_ref, b_ref, L, qk_seq,
                   slabs=[*q_slabs, *k_slabs])
        yield
        q = jnp.concatenate([qk_seq[i] for i in q_slabs], axis=1)
        k = jnp.concatenate([qk_seq[i] for i in k_slabs], axis=1) * K_SCALE
        v = v_ref[s, :, h * M_V:(h + 1) * M_V]
        i_col = sm[:, SM_I + h:SM_I + h + 1]
        i_row = sm_t[SM_I + h:SM_I + h + 1, :]
        b_col = b_all[:, SM_F + h:SM_F + h + 1]
        b_row = b_all_t[SM_F + h:SM_F + h + 1, :]

        c_prev = c_ref[s, h]
        n_prev = n_ref[s, h]
        m_prev = m_ref[s, h]

        dmat = jnp.where(tril, b_col - b_row + i_row, -jnp.inf)
        inter = b_col + m_prev
        mt = jnp.maximum(inter, jnp.max(dmat, axis=1, keepdims=True))
        w_tok = jnp.exp(dmat - mt)
        w_st = jnp.exp(inter - mt)
        yield

        qb = q.astype(BF16)
        kb = k.astype(BF16)
        vb = v.astype(BF16)
        sc = _nt_dot(qb, kb) * w_tok
        yield
        num = (jnp.dot(sc.astype(BF16), vb, preferred_element_type=F32)
               + w_st * jnp.dot(qb, c_prev.astype(BF16), preferred_element_type=F32))
        den = (jnp.sum(sc, axis=1, keepdims=True)
               + w_st * jnp.sum(q * n_prev, axis=1, keepdims=True))
        yield
        hh = num / jnp.maximum(jnp.abs(den), jnp.exp(-mt))

        m_new = mt[L - 1:L, :]
        b_last = b_col[L - 1:L, :]
        ws = jnp.exp(b_last - b_col + i_col - m_new)
        decay = jnp.exp(b_last + m_prev - m_new)
        kw = k * ws
        c_ref[s, h] = decay * c_prev + _tn_dot(kw.astype(BF16), vb)
        n_ref[s, h] = decay * n_prev + jnp.sum(kw, axis=0, keepdims=True)
        m_ref[s, h] = m_new
        yield

        mu = jnp.mean(hh, axis=-1, keepdims=True)
        hc = hh - mu
        yield
        var = jnp.mean(hc * hc, axis=-1, keepdims=True)
        hsl = slice(h * M_V, (h + 1) * M_V)
        gate = (0.5 * hg_ref[:, hsl]) * (jnp.tanh(0.5 * og_ref[s, :, hsl]) + 1.0)
        o_ref[s, :, hsl] = (hc * lax.rsqrt(var + EPS)) * gate

    streams = [head_stages(s, h) for s in range(nseq) for h in range(M_HEADS)]
    for i in range(0, len(streams), MLSTM_INTERLEAVE):
        _run_stages(streams[i:i + MLSTM_INTERLEAVE])


def _seqs_per_step(bsz, nc, short):
    want = short if nc == 1 else LONG_SEQS_PER_STEP
    return want if bsz % want == 0 else 1


def _mlstm(proj, bsz, T, m_conv_w, m_conv_b, small_bias, m_head_g, state=None):
    L = min(T, CHUNK)
    nc = T // L
    assert T == nc * L
    nseq = _seqs_per_step(bsz, nc, MLSTM_SHORT_SEQS)
    has_state = state is not None
    qkw = 2 * M_QK_W

    const = lambda b, c: (0, 0)
    per_b = lambda b, c: (b, 0, 0, 0)
    in_specs = [
        pl.BlockSpec((nseq, L, qkw), lambda b, c: (b, c, P_QK // qkw)),
        pl.BlockSpec((nseq, L, M_V_W), lambda b, c: (b, c, P_V // M_V_W)),
        pl.BlockSpec((nseq, L, M_V_W), lambda b, c: (b, c, P_O // M_V_W)),
        pl.BlockSpec((nseq, L, LANES), lambda b, c: (b, c, P_SMALL // LANES)),
        pl.BlockSpec((1, LANES), const),
        pl.BlockSpec((1, M_V_W), const),
        pl.BlockSpec((CONV_W, qkw), const),
        pl.BlockSpec((1, qkw), const),
    ]
    args = [proj, proj, proj, proj, small_bias, m_head_g.reshape(1, -1),
            m_conv_w, m_conv_b.reshape(1, -1)]
    scratch = [pltpu.VMEM((nseq, qkw // LANES, L + SUBLANES, LANES), F32),
               pltpu.VMEM((nseq, qkw // LANES, L, LANES), F32)]
    if has_state:
        conv0, c0, n0, m0 = state
        in_specs += [
            pl.BlockSpec((nseq, CONV_W - 1, qkw), lambda b, c: (b, 0, 0)),
            pl.BlockSpec((nseq, M_HEADS, M_QK, M_V), per_b),
            pl.BlockSpec((nseq, M_HEADS, 1, M_QK), per_b),
            pl.BlockSpec((nseq, M_HEADS, 1, 1), per_b),
        ]
        args += [conv0, c0, n0.reshape(bsz, M_HEADS, 1, M_QK), m0.reshape(bsz, M_HEADS, 1, 1)]
    out_shape = (
        jax.ShapeDtypeStruct((bsz, T, M_V_W), F32),
        jax.ShapeDtypeStruct((bsz, M_HEADS, M_QK, M_V), F32),
        jax.ShapeDtypeStruct((bsz, M_HEADS, 1, M_QK), F32),
        jax.ShapeDtypeStruct((bsz, M_HEADS, 1, 1), F32),
    )
    out_specs = (
        pl.BlockSpec((nseq, L, M_V_W), lambda b, c: (b, c, 0)),
        pl.BlockSpec((nseq, M_HEADS, M_QK, M_V), per_b),
        pl.BlockSpec((nseq, M_HEADS, 1, M_QK), per_b),
        pl.BlockSpec((nseq, M_HEADS, 1, 1), per_b),
    )
    out, c1, n1, m1 = pl.pallas_call(
        functools.partial(_mlstm_kernel, L=L, nseq=nseq, has_state=has_state),
        grid=(bsz // nseq, nc),
        in_specs=in_specs,
        out_specs=out_specs,
        out_shape=out_shape,
        scratch_shapes=scratch,
        compiler_params=pltpu.CompilerParams(
            dimension_semantics=("arbitrary", "arbitrary"), vmem_limit_bytes=VMEM_LIMIT),
        name="mlstm_state" if has_state else "mlstm",
    )(*args)
    return out, c1, n1.reshape(bsz, M_HEADS, M_QK), m1.reshape(bsz, M_HEADS)


def _ssd_kernel(*refs, L, nseq, has_state):
    n_shared = 13
    e_ref = refs[9]
    streams, shared = [], []
    for sq in range(nseq):
        one = pl.ds(sq, 1)
        views = [r.at[sq] for r in refs[:5]] + list(refs[5:n_shared])
        rest = refs[n_shared:]
        if has_state:
            views += [r.at[one] for r in rest[:2]]
            rest = rest[2:]
        o_ref, h_ref = rest[:2]
        views += [o_ref.at[sq], h_ref.at[one]] + [r.at[sq] for r in rest[2:]]
        seq_streams, seq_shared = _ssd_seq_streams(*views, L=L, has_state=has_state)
        streams += seq_streams
        shared.append(seq_shared)
    if L < LANES:
        onehot = jnp.concatenate([e_ref[p] for p in range(e_ref.shape[0])], axis=1)
        stacked = jnp.concatenate([a for sh in shared for a in sh["src"]], axis=0)
        wide = sum(jnp.dot(p, onehot, preferred_element_type=F32)
                   for p in _split3(stacked)[:2])
        for i, sh in enumerate(shared):
            sh["wide"] = [wide[(3 * i + q) * L:(3 * i + q + 1) * L] for q in range(3)]
    _run_stages(streams)


def _ssd_seq_streams(*refs, L, has_state):
    (z_ref, xs_ref, bp_ref, cp_ref, sm_ref, sb_ref, al_ref, d_ref, ng_ref, e_ref, et_ref,
     w_ref, b_ref) = refs[:13]
    refs = refs[13:]
    if has_state:
        cv0_ref, h0_ref = refs[:2]
        refs = refs[2:]
    o_ref, h_ref, xx_ref, xb_ref, xc_ref, xs_s, bm_s, cm_s = refs

    @pl.when(pl.program_id(1) == 0)
    def _():
        if has_state:
            h_ref[...] = h0_ref[...]
        else:
            h_ref[...] = jnp.zeros_like(h_ref)

    x_cols = slice(0, S_INNER)
    b_cols = slice(S_INNER, S_INNER + S_BCW)
    c_cols = slice(S_INNER + S_BCW, S_CONV_CH)
    for xp_ref, cols in ((xx_ref, x_cols), (xb_ref, b_cols), (xc_ref, c_cols)):
        @pl.when(pl.program_id(1) == 0)
        def _():
            _init_conv(xp_ref, cv0_ref.at[:, :, cols] if has_state else None)

    _conv_silu(xb_ref, bp_ref, w_ref.at[:, b_cols], b_ref.at[:, b_cols], L, bm_s)
    _conv_silu(xc_ref, cp_ref, w_ref.at[:, c_cols], b_ref.at[:, c_cols], L, cm_s)

    dt_all = _softplus(sm_ref[...] + sb_ref[...])
    da_all = dt_all * (-jnp.exp(al_ref[...]))
    lane = lax.broadcasted_iota(jnp.int32, (L, LANES), 1)
    rr = lax.broadcasted_iota(jnp.int32, (L, L), 0)
    cc = lax.broadcasted_iota(jnp.int32, (L, L), 1)
    tril = cc <= rr
    lo = lane < S_HEADDIM
    cum_all = _dot3(tril.astype(BF16), da_all)
    cum_all_t = _tn3(da_all, (rr <= cc).astype(BF16))
    cum2, cum2_t = cum_all * LOG2_E, cum_all_t * LOG2_E
    cum_last = cum_all[L - 1:L, :]
    dec_all = dt_all * jnp.exp(cum_last - cum_all)
    ecum_all = jnp.exp(cum_all)
    mxu_spread = L >= LANES
    shared = {"src": (dt_all, dec_all, ecum_all), "wide": None}
    if mxu_spread:
        dt_src = _split3(dt_all)[:2]
        dec_src = _split3(dec_all)[:2]
        ecum_src = _split3(ecum_all)[:2]
        clast_src = _split3(jnp.broadcast_to(cum_all_t[:, L - 1:L], (LANES, LANES)))
    else:
        dt_src, dec_src, ecum_src = 0, 1, 2
        lo_row = lax.broadcasted_iota(jnp.int32, (LANES, 1), 0) < S_HEADDIM

    def spread(src, pair):
        if mxu_spread:
            return sum(jnp.dot(p, e_ref[pair], preferred_element_type=F32) for p in src)
        return shared["wide"][src][:, pair * LANES:(pair + 1) * LANES]

    def state_decay(pair):
        if mxu_spread:
            return jnp.exp(sum(jnp.dot(et_ref[pair], p, preferred_element_type=F32)
                               for p in clast_src))
        idx = SM_DT + 2 * pair
        return jnp.exp(jnp.where(lo_row, cum_last[:, idx:idx + 1], cum_last[:, idx + 1:idx + 2]))

    def group_stages(g):
        _conv_silu(xx_ref, xs_ref, w_ref.at[:, x_cols], b_ref.at[:, x_cols], L, xs_s,
                   slabs=range(g * S_GW // LANES, (g + 1) * S_GW // LANES))
        yield
        bmb = bm_s[g].astype(BF16)
        cmb = cm_s[g].astype(BF16)
        cb = _nt_dot(cmb, bmb)
        ssq = jnp.zeros((L, 1), F32)
        for p in range(S_HPG // 2):
            yield
            pair = g * (S_HPG // 2) + p
            psl = slice(g * S_GW + p * LANES, g * S_GW + (p + 1) * LANES)
            hsl = slice(p * LANES, (p + 1) * LANES)
            lmat = []
            for e in range(2):
                idx = SM_DT + 2 * pair + e
                lmat.append(jnp.exp2(jnp.where(
                    tril, cum2[:, idx:idx + 1] - cum2_t[idx:idx + 1, :], -jnp.inf)))
            yield
            xs = xs_s[pair]
            xdt = xs * spread(dt_src, pair)
            xdd = xs * spread(dec_src, pair)
            h_prev = h_ref[0, g, hsl, :]
            yield
            xdt_b = xdt.astype(BF16)
            y = (jnp.where(lo,
                           jnp.dot((cb * lmat[0]).astype(BF16), xdt_b,
                                   preferred_element_type=F32),
                           jnp.dot((cb * lmat[1]).astype(BF16), xdt_b,
                                   preferred_element_type=F32))
                 + spread(ecum_src, pair) * _nt_dot(cmb, h_prev.astype(BF16)))
            h_ref[0, g, hsl, :] = state_decay(pair) * h_prev + _tn_dot(xdd.astype(BF16), bmb)
            yield
            gated = (y + d_ref[:, psl] * xs) * _silu(z_ref[:, psl])
            ssq = ssq + jnp.sum(gated * gated, axis=1, keepdims=True)
            o_ref[:, psl] = gated
        yield
        gsl = slice(g * S_GW, (g + 1) * S_GW)
        o_ref[:, gsl] = o_ref[:, gsl] * lax.rsqrt(ssq * (1.0 / S_GW) + EPS) * ng_ref[:, gsl]

    return [group_stages(g) for g in range(S_GROUPS)], shared


def _pair_onehots():
    n_pairs = S_HEADS // 2
    pair = jnp.arange(n_pairs)[:, None, None]
    src = jnp.arange(LANES)[None, :, None]
    dst = jnp.arange(LANES)[None, None, :]
    onehot = src == SM_DT + 2 * pair + dst // S_HEADDIM
    return onehot.astype(BF16), jnp.swapaxes(onehot, 1, 2).astype(BF16)


def _ssd(proj, bsz, T, s_conv_w, s_conv_b, small_bias, alog_row, d_row, s_norm_g, state=None):
    L = min(T, CHUNK)
    nc = T // L
    assert T == nc * L
    nseq = _seqs_per_step(bsz, nc, SSD_SHORT_SEQS)
    has_state = state is not None
    n_pairs = S_HEADS // 2
    onehot, onehot_t = _pair_onehots()

    const = lambda b, c: (0, 0)
    const3 = lambda b, c: (0, 0, 0)
    in_specs = [
        pl.BlockSpec((nseq, L, S_INNER), lambda b, c: (b, c, P_Z // S_INNER)),
        pl.BlockSpec((nseq, L, S_INNER), lambda b, c: (b, c, P_XBC // S_INNER)),
        pl.BlockSpec((nseq, L, S_BCW), lambda b, c: (b, c, P_B // S_BCW)),
        pl.BlockSpec((nseq, L, S_BCW), lambda b, c: (b, c, P_C // S_BCW)),
        pl.BlockSpec((nseq, L, LANES), lambda b, c: (b, c, P_SMALL // LANES)),
        pl.BlockSpec((1, LANES), const),
        pl.BlockSpec((1, LANES), const),
        pl.BlockSpec((1, S_INNER), const),
        pl.BlockSpec((1, S_INNER), const),
        pl.BlockSpec((n_pairs, LANES, LANES), const3),
        pl.BlockSpec((n_pairs, LANES, LANES), const3),
        pl.BlockSpec((CONV_W, S_CONV_CH), const),
        pl.BlockSpec((1, S_CONV_CH), const),
    ]
    args = [proj, proj, proj, proj, proj, small_bias, alog_row, d_row, s_norm_g.reshape(1, -1),
            onehot, onehot_t, s_conv_w, s_conv_b.reshape(1, -1)]
    scratch = [pltpu.VMEM((nseq, width // LANES, rows, LANES), F32)
               for rows in (L + SUBLANES, L) for width in (S_INNER, S_BCW, S_BCW)]
    if has_state:
        conv0, h0 = state
        in_specs += [
            pl.BlockSpec((nseq, CONV_W - 1, S_CONV_CH), lambda b, c: (b, 0, 0)),
            pl.BlockSpec((nseq, S_GROUPS, S_GW, S_STATE), lambda b, c: (b, 0, 0, 0)),
        ]
        args += [conv0, h0.reshape(bsz, S_GROUPS, S_GW, S_STATE)]
    out, h1 = pl.pallas_call(
        functools.partial(_ssd_kernel, L=L, nseq=nseq, has_state=has_state),
        grid=(bsz // nseq, nc),
        in_specs=in_specs,
        out_specs=(
            pl.BlockSpec((nseq, L, S_INNER), lambda b, c: (b, c, 0)),
            pl.BlockSpec((nseq, S_GROUPS, S_GW, S_STATE), lambda b, c: (b, 0, 0, 0)),
        ),
        out_shape=(
            jax.ShapeDtypeStruct((bsz, T, S_INNER), F32),
            jax.ShapeDtypeStruct((bsz, S_GROUPS, S_GW, S_STATE), F32),
        ),
        scratch_shapes=scratch,
        compiler_params=pltpu.CompilerParams(
            dimension_semantics=("arbitrary", "arbitrary"), vmem_limit_bytes=VMEM_LIMIT),
        name="ssd_state" if has_state else "ssd",
    )(*args)
    return out, h1.reshape(bsz, S_HEADS, S_HEADDIM, S_STATE)


def _merge_kernel(x_ref, hm_ref, gt_ref, ga_ref, gb_ref, g_ref, wpa_ref, wpb_ref, wo_ref, o_ref,
                  wpa_s, wpb_s, wo_s):
    @pl.when(pl.program_id(0) == 0)
    def _():
        wpa_s[...] = wpa_ref[...].astype(BF16)
        wpb_s[...] = wpb_ref[...].astype(BF16)
        wo_s[...] = wo_ref[...].astype(BF16)

    ya = jnp.dot(hm_ref[...].astype(BF16), wpa_s[...], preferred_element_type=F32)
    yb = jnp.dot(gt_ref[...].astype(BF16), wpb_s[...], preferred_element_type=F32)
    merged = _sigmoid(ga_ref[...]) * ya + _sigmoid(gb_ref[...]) * yb
    o_ref[...] = x_ref[...] + _mod_rows(g_ref, x_ref.shape[0]) * jnp.dot(
        merged.astype(BF16), wo_s[...], preferred_element_type=F32)


def _merge(x, hm, gated, proj, mod3, rows_per_group, kmod, w_pa, w_pb, w_out):
    n = x.shape[0]
    tm = MERGE_TM
    assert n % tm == 0 and (rows_per_group % tm == 0 or tm % rows_per_group == 0)
    const = lambda i: (0, 0)
    resident = functools.partial(pl.BlockSpec, index_map=const, pipeline_mode=pl.Buffered(1))
    return pl.pallas_call(
        _merge_kernel,
        grid=(n // tm,),
        in_specs=[
            pl.BlockSpec((tm, D_MODEL), lambda i: (i, 0)),
            pl.BlockSpec((tm, M_V_W), lambda i: (i, 0)),
            pl.BlockSpec((tm, S_INNER), lambda i: (i, 0)),
            pl.BlockSpec((tm, D_MODEL), lambda i: (i, P_GA // D_MODEL)),
            pl.BlockSpec((tm, D_MODEL), lambda i: (i, P_GB // D_MODEL)),
            _mod_spec(rows_per_group, tm,kmod),
            resident((M_V_W, D_MODEL)),
            resident((S_INNER, D_MODEL)),
            resident((D_MODEL, D_MODEL)),
        ],
        out_specs=pl.BlockSpec((tm, D_MODEL), lambda i: (i, 0)),
        out_shape=jax.ShapeDtypeStruct((n, D_MODEL), F32),
        scratch_shapes=[pltpu.VMEM((M_V_W, D_MODEL), BF16),
                        pltpu.VMEM((S_INNER, D_MODEL), BF16),
                        pltpu.VMEM((D_MODEL, D_MODEL), BF16)],
        compiler_params=pltpu.CompilerParams(
            dimension_semantics=("arbitrary",), vmem_limit_bytes=VMEM_LIMIT),
        name="merge",
    )(x, hm, gated, proj, proj, mod3, w_pa, w_pb, w_out)


def _layer(x, mod3, rows_per_group, bsz, T, wts, final_norm, state):
    n = bsz * T
    x = x.reshape(n, D_MODEL)
    x = _ffn(x, mod3, rows_per_group, 0, wts["norm_ff1"], wts["ff1_w1"], wts["ff1_w3"],
             wts["ff1_w2"])
    proj = _proj(x, mod3, rows_per_group, 3, wts["norm_mix"], wts["w_packed"])
    proj3 = proj.reshape(bsz, T, P_WIDTH)
    mixer_args = (proj3, bsz, T)
    m_args = (wts["m_conv_w"], wts["m_conv_b"], wts["small_bias"], wts["m_head_g"])
    s_args = (wts["s_conv_w"], wts["s_conv_b"], wts["small_bias"], wts["alog_row"],
              wts["d_row"], wts["s_norm_g"])
    if state is None:
        m_state = s_state = None
    else:
        c0, n0, m0, mconv0, ssm0, sconv0 = state
        m_state = (mconv0, c0, n0, m0)
        s_state = (sconv0, ssm0)
    hm, c1, n1, m1 = _mlstm(*mixer_args, *m_args, state=m_state)
    gated, h1 = _ssd(*mixer_args, *s_args, state=s_state)
    x = _merge(x, hm.reshape(n, M_V_W), gated.reshape(n, S_INNER), proj, mod3, rows_per_group,
               5, wts["w_pa"], wts["w_pb"], wts["w_out"])
    y = _ffn(x, mod3, rows_per_group, 6, wts["norm_ff2"], wts["ff2_w1"], wts["ff2_w3"],
             wts["ff2_w2"], final_norm=final_norm)
    last = proj3[:, T - (CONV_W - 1):]
    mconv1 = last[:, :, P_QK:P_QK + 2 * M_QK_W]
    sconv1 = last[:, :, P_XBC:P_XBC + S_CONV_CH]
    return y.reshape(bsz, T, D_MODEL), (c1, n1, m1, mconv1, h1, sconv1)


def kernel(x_prompt, x_sample, c_prompt, c_sample, state_mlstm_C, state_mlstm_n, state_mlstm_m, state_mlstm_conv, state_ssm, state_ssm_conv, ada_w, ada_b, norm_ff1, ff1_w1, ff1_w3, ff1_w2, norm_mix, w_in, m_conv_w, m_conv_b, m_if_b, m_head_g, w_pa, s_conv_w, s_conv_b, s_dt_bias, s_A_log, s_D, s_norm_g, w_pb, w_out, norm_ff2, ff2_w1, ff2_w3, ff2_w2, final_norm):
    bp, tp, _ = x_prompt.shape
    bs, ts, _ = x_sample.shape
    depth = ada_w.shape[0]
    assert depth == 1

    hp, hs = x_prompt, x_sample
    new_p, new_s = [], []
    for l in range(depth):
        c_all = jnp.concatenate([c_prompt, c_sample], axis=0)
        pad_rows = (-c_all.shape[0]) % (2 * SUBLANES)
        c_all = jnp.pad(c_all, ((0, pad_rows), (0, 0)))
        mod = _ada(c_all, ada_w[l], ada_b[l])
        mod_p = mod[:bp].reshape(bp, 1, N_MOD * D_MODEL)
        mod_s = mod[bp:bp + bs].reshape(bs, 1, N_MOD * D_MODEL)

        lane_pad = LANES - (2 * M_HEADS + S_HEADS)
        small_bias = jnp.concatenate(
            [m_if_b[l], s_dt_bias[l], jnp.zeros((lane_pad,), F32)]).reshape(1, LANES)
        alog_row = jnp.concatenate(
            [jnp.zeros((2 * M_HEADS,), F32), s_A_log[l], jnp.zeros((lane_pad,), F32)]
        ).reshape(1, LANES)
        wts = {
            "norm_ff1": norm_ff1[l], "ff1_w1": ff1_w1[l], "ff1_w3": ff1_w3[l],
            "ff1_w2": ff1_w2[l],
            "norm_mix": norm_mix[l], "w_packed": _pack_w_in(jnp.swapaxes(w_in, 1, 2)[l]),
            "m_conv_w": m_conv_w[l], "m_conv_b": m_conv_b[l], "small_bias": small_bias,
            "m_head_g": m_head_g[l], "w_pa": w_pa[l],
            "s_conv_w": s_conv_w[l], "s_conv_b": s_conv_b[l], "alog_row": alog_row,
            "d_row": jnp.repeat(s_D[l], S_HEADDIM).reshape(1, S_INNER),
            "s_norm_g": s_norm_g[l], "w_pb": w_pb[l], "w_out": w_out[l],
            "norm_ff2": norm_ff2[l], "ff2_w1": ff2_w1[l], "ff2_w3": ff2_w3[l],
            "ff2_w2": ff2_w2[l],
        }
        st_s = (state_mlstm_C[l], state_mlstm_n[l], state_mlstm_m[l], state_mlstm_conv[l],
                state_ssm[l], state_ssm_conv[l])
        hp, sp = _layer(hp, mod_p, tp, bp, tp, wts, final_norm, None)
        hs, ss = _layer(hs, mod_s, ts, bs, ts, wts, final_norm, st_s)
        new_p.append(sp)
        new_s.append(ss)
    outs_p = [jnp.stack([s[i] for s in new_p]) for i in range(6)]
    outs_s = [jnp.stack([s[i] for s in new_s]) for i in range(6)]
    return (hp, hs, *outs_p, *outs_s)
```

```python
import functools

import jax
import jax.numpy as jnp
from jax import lax
from jax.experimental import pallas as pl
from jax.experimental.pallas import tpu as pltpu

F32 = jnp.float32
BF16 = jnp.bfloat16

D_MODEL = 1024
M_HEADS = 4
M_QK = 256
M_V = 512
M_QK_W = M_HEADS * M_QK
M_V_W = M_HEADS * M_V
S_INNER = 2048
S_HEADDIM = 64
S_HEADS = 32
S_GROUPS = 4
S_HPG = S_HEADS // S_GROUPS
S_STATE = 128
S_GW = S_INNER // S_GROUPS
S_BCW = S_GROUPS * S_STATE
S_CONV_CH = S_INNER + 2 * S_BCW
CONV_W = 4
D_FF = 2816
CHUNK = 128
EPS = 1e-6
N_MOD = 9
K_SCALE = M_QK ** -0.5
LOG2_E = 1.4426950408889634

LANES = 128
SUBLANES = 8
VMEM_LIMIT = 56 * 1024 * 1024

P_QK = 0
P_V = 2 * M_QK_W
P_O = P_V + M_V_W
P_HEAD = P_O + M_V_W
P_Z = P_HEAD
P_XBC = P_Z + S_INNER
P_B = P_XBC + S_INNER
P_C = P_B + S_BCW
P_GA = P_XBC + S_CONV_CH
P_GB = P_GA + D_MODEL
P_SMALL = P_GB + D_MODEL
PROJ_TN = 2304
PACK_TN = 512
P_WIDTH = 13824
SM_I = 0
SM_F = M_HEADS
SM_DT = 2 * M_HEADS

ADA_TN = 2304
FFN_TM = 2048
FFN_TF = 256
PROJ_TM = 1024
MERGE_TM = 256
C_RING_SLOTS = 3
MLSTM_INTERLEAVE = 8
MLSTM_SHORT_SEQS = 4
SSD_SHORT_SEQS = 8
LONG_SEQS_PER_STEP = 1
CONV_STRIDE = 4
CONV_ROWS = SUBLANES * CONV_STRIDE


def _sigmoid(x):
    return 0.5 * jnp.tanh(0.5 * x) + 0.5


def _silu(x):
    h = 0.5 * x
    return h * jnp.tanh(h) + h


def _rms(x, g):
    return x * lax.rsqrt(jnp.mean(x * x, axis=-1, keepdims=True) + EPS) * g


def _ada_kernel(c_ref, w_ref, b_ref, o_ref):
    s = _silu(c_ref[...]).astype(BF16)
    o_ref[...] = jnp.dot(s, w_ref[...].astype(BF16), preferred_element_type=F32) + b_ref[...]


def _ada(c_all, ada_w, ada_b):
    m = c_all.shape[0]
    n = ada_w.shape[1]
    tn = ADA_TN
    assert n % tn == 0
    return pl.pallas_call(
        _ada_kernel,
        grid=(n // tn,),
        in_specs=[
            pl.BlockSpec((m, D_MODEL), lambda j: (0, 0)),
            pl.BlockSpec((D_MODEL, tn), lambda j: (0, j)),
            pl.BlockSpec((1, tn), lambda j: (0, j)),
        ],
        out_specs=pl.BlockSpec((m, tn), lambda j: (0, j)),
        out_shape=jax.ShapeDtypeStruct((m, n), F32),
        compiler_params=pltpu.CompilerParams(
            dimension_semantics=("arbitrary",), vmem_limit_bytes=VMEM_LIMIT),
        name="ada_mod",
    )(c_all, ada_w, ada_b.reshape(1, n))


def _ffn_kernel(x_ref, sh_ref, sc_ref, g_ref, nw_ref, w1_ref, w3_ref, w2_ref, *rest,
                nf, final):
    if final:
        fn_ref, o_ref, u_ref = rest
    else:
        o_ref, u_ref = rest
    f = pl.program_id(1)

    @pl.when(f == 0)
    def _():
        tm = x_ref.shape[0]
        gain = nw_ref[...] * (1.0 + _mod_rows(sc_ref, tm))
        u_ref[...] = (_rms(x_ref[...], gain) + _mod_rows(sh_ref, tm)).astype(BF16)
        o_ref[...] = jnp.zeros_like(o_ref)

    u = u_ref[...]
    h1 = jnp.dot(u, w1_ref[...].astype(BF16), preferred_element_type=F32)
    h3 = jnp.dot(u, w3_ref[...].astype(BF16), preferred_element_type=F32)
    a = (_silu(h1) * h3).astype(BF16)
    o_ref[...] += jnp.dot(a, w2_ref[...].astype(BF16), preferred_element_type=F32)

    @pl.when(f == nf - 1)
    def _():
        y = x_ref[...] + (0.5 * _mod_rows(g_ref, x_ref.shape[0])) * o_ref[...]
        if final:
            y = _rms(y, fn_ref[...])
        o_ref[...] = y


def _mod_spec(rows_per_group, tm, k):
    if rows_per_group >= tm:
        tiles_per_group = rows_per_group // tm
        return pl.BlockSpec((1, 1, D_MODEL), lambda i, *_: (i // tiles_per_group, 0, k))
    groups = tm // rows_per_group
    return pl.BlockSpec((groups, 1, D_MODEL), lambda i, *_: (i, 0, k))


def _mod_rows(ref, tm):
    groups = ref.shape[0]
    if groups == 1:
        return ref[0]
    per_row = jnp.broadcast_to(ref[...], (groups, tm // groups, D_MODEL))
    return per_row.reshape(tm, D_MODEL)


def _ffn(x, mod3, rows_per_group, kmod, norm_w, w1, w3, w2, final_norm=None):
    n = x.shape[0]
    tm = min(FFN_TM, n)
    assert n % tm == 0 and (rows_per_group % tm == 0 or tm % rows_per_group == 0)
    tf = FFN_TF
    nf = D_FF // tf
    final = final_norm is not None
    in_specs = [
        pl.BlockSpec((tm, D_MODEL), lambda i, f: (i, 0)),
        _mod_spec(rows_per_group, tm,kmod),
        _mod_spec(rows_per_group, tm,kmod + 1),
        _mod_spec(rows_per_group, tm,kmod + 2),
        pl.BlockSpec((1, D_MODEL), lambda i, f: (0, 0)),
        pl.BlockSpec((D_MODEL, tf), lambda i, f: (0, f)),
        pl.BlockSpec((D_MODEL, tf), lambda i, f: (0, f)),
        pl.BlockSpec((tf, D_MODEL), lambda i, f: (f, 0)),
    ]
    args = [x, mod3, mod3, mod3, norm_w.reshape(1, D_MODEL), w1, w3, w2]
    if final:
        in_specs.append(pl.BlockSpec((1, D_MODEL), lambda i, f: (0, 0)))
        args.append(final_norm.reshape(1, D_MODEL))
    return pl.pallas_call(
        functools.partial(_ffn_kernel, nf=nf, final=final),
        grid=(n // tm, nf),
        in_specs=in_specs,
        out_specs=pl.BlockSpec((tm, D_MODEL), lambda i, f: (i, 0)),
        out_shape=jax.ShapeDtypeStruct((n, D_MODEL), F32),
        scratch_shapes=[pltpu.VMEM((tm, D_MODEL), BF16)],
        compiler_params=pltpu.CompilerParams(
            dimension_semantics=("parallel", "arbitrary"), vmem_limit_bytes=VMEM_LIMIT),
        name="ffn_final" if final else "ffn",
    )(*args)


PACK_SHIFT_Z = 2 * M_HEADS
PACK_SHIFT_GA = PACK_SHIFT_Z + S_HEADS
SRC_IF = P_HEAD
SRC_DT = P_HEAD + PACK_SHIFT_Z + S_INNER + S_CONV_CH


def _pack_kernel(a_ref, gif_ref, gdt_ref, o_ref):
    j = pl.program_id(0)

    @pl.when(j < P_SMALL // PACK_TN)
    def _():
        o_ref[...] = a_ref[...].astype(BF16)

    @pl.when(j == P_SMALL // PACK_TN)
    def _():
        pad = jnp.zeros((PACK_TN - SM_DT - S_HEADS, o_ref.shape[1]), F32)
        o_ref[...] = jnp.concatenate([gif_ref[...], gdt_ref[...], pad], axis=0).astype(BF16)


def _pack_w_in(w_t):
    d = w_t.shape[1]
    n_main = P_SMALL // PACK_TN
    last_start = w_t.shape[0] - PACK_TN

    def src_row(j):
        shift = jnp.where(j < P_HEAD // PACK_TN, 0,
                          jnp.where(j < P_GA // PACK_TN, PACK_SHIFT_Z, PACK_SHIFT_GA))
        tile = jnp.minimum((j * PACK_TN + shift) // SUBLANES, last_start // SUBLANES)
        return tile * SUBLANES

    return pl.pallas_call(
        _pack_kernel,
        grid=(n_main + 1,),
        in_specs=[
            pl.BlockSpec((pl.Element(PACK_TN), pl.Element(d)), lambda j: (src_row(j), 0)),
            pl.BlockSpec((pl.Element(SM_DT), pl.Element(d)), lambda j: (SRC_IF, 0)),
            pl.BlockSpec((pl.Element(S_HEADS), pl.Element(d)), lambda j: (SRC_DT, 0)),
        ],
        out_specs=pl.BlockSpec((PACK_TN, d), lambda j: (j, 0)),
        out_shape=jax.ShapeDtypeStruct((P_WIDTH, d), BF16),
        compiler_params=pltpu.CompilerParams(
            dimension_semantics=("arbitrary",), vmem_limit_bytes=VMEM_LIMIT),
        name="pack_w_in",
    )(w_t, w_t, w_t)


def _proj_kernel(x_ref, sh_ref, sc_ref, nw_ref, w_ref, o_ref, u_ref):
    @pl.when(pl.program_id(1) == 0)
    def _():
        tm = x_ref.shape[0]
        gain = nw_ref[...] * (1.0 + _mod_rows(sc_ref, tm))
        u_ref[...] = (_rms(x_ref[...], gain) + _mod_rows(sh_ref, tm)).astype(BF16)

    o_ref[...] = _nt_dot(u_ref[...], w_ref[...])


def _proj(x, mod3, rows_per_group, kmod, norm_w, w_packed):
    n = x.shape[0]
    tm = min(PROJ_TM, n)
    assert n % tm == 0 and (rows_per_group % tm == 0 or tm % rows_per_group == 0)
    tn = PROJ_TN
    return pl.pallas_call(
        _proj_kernel,
        grid=(n // tm, P_WIDTH // tn),
        in_specs=[
            pl.BlockSpec((tm, D_MODEL), lambda i, j: (i, 0)),
            _mod_spec(rows_per_group, tm,kmod),
            _mod_spec(rows_per_group, tm,kmod + 1),
            pl.BlockSpec((1, D_MODEL), lambda i, j: (0, 0)),
            pl.BlockSpec((tn, D_MODEL), lambda i, j: (j, 0)),
        ],
        out_specs=pl.BlockSpec((tm, tn), lambda i, j: (i, j)),
        out_shape=jax.ShapeDtypeStruct((n, P_WIDTH), F32),
        scratch_shapes=[pltpu.VMEM((tm, D_MODEL), BF16)],
        compiler_params=pltpu.CompilerParams(
            dimension_semantics=("parallel", "arbitrary"), vmem_limit_bytes=VMEM_LIMIT),
        name="mix_proj",
    )(x, mod3, mod3, norm_w.reshape(1, D_MODEL), w_packed)


def _conv_silu(xp_ref, x_ref, w_ref, b_ref, L, dst_ref, slabs=None):
    if slabs is None:
        slabs = range(x_ref.shape[1] // LANES)
    for s in slabs:
        cols = slice(s * LANES, (s + 1) * LANES)
        xp, dst = xp_ref.at[s], dst_ref.at[s]
        xp[pl.ds(SUBLANES, L), :] = x_ref[:, cols]
        bias = b_ref[:, cols]
        taps = [w_ref[t:t + 1, cols] for t in range(CONV_W)]
        if L % CONV_ROWS == 0:
            for r in range(0, L, CONV_ROWS):
                for b in range(CONV_STRIDE):
                    acc = bias
                    for back in range(CONV_W):
                        src = pl.ds(SUBLANES + r + b - back, SUBLANES, stride=CONV_STRIDE)
                        acc = acc + xp[src, :] * taps[CONV_W - 1 - back]
                    dst[pl.ds(r + b, SUBLANES, stride=CONV_STRIDE), :] = _silu(acc)
        else:
            x_all = xp[...]
            acc = bias + x_all[SUBLANES:] * taps[CONV_W - 1]
            for back in range(1, CONV_W):
                acc = acc + pltpu.roll(x_all, back, axis=0)[SUBLANES:] * taps[CONV_W - 1 - back]
            dst[...] = _silu(acc)
        xp[pl.ds(0, SUBLANES), :] = xp[pl.ds(L, SUBLANES), :]


def _init_conv(xp_ref, init_ref):
    n_slabs = xp_ref.shape[0]
    xp_ref[:, pl.ds(0, SUBLANES), :] = jnp.zeros((n_slabs, SUBLANES, LANES), F32)
    if init_ref is not None:
        for s in range(n_slabs):
            xp_ref[s, pl.ds(SUBLANES - (CONV_W - 1), CONV_W - 1), :] = (
                init_ref[0, :, s * LANES:(s + 1) * LANES])


def _run_stages(streams):
    live = list(streams)
    while live:
        for stream in list(live):
            try:
                next(stream)
            except StopIteration:
                live.remove(stream)


def _split3(a):
    hi = a.astype(BF16)
    r1 = a - hi.astype(F32)
    mid = r1.astype(BF16)
    lo = (r1 - mid.astype(F32)).astype(BF16)
    return hi, mid, lo


def _nt_dot(a, b):
    return lax.dot_general(a, b, (((1,), (1,)), ((), ())), preferred_element_type=F32)


def _tn_dot(a, b):
    return lax.dot_general(a, b, (((0,), (0,)), ((), ())), preferred_element_type=F32)


def _dot3(mask_b, a):
    return sum(jnp.dot(mask_b, p, preferred_element_type=F32) for p in _split3(a))


def _tn3(a, mask_b):
    return sum(_tn_dot(p, mask_b) for p in _split3(a))


def _log_sigmoid(x):
    return jnp.minimum(x, 0.0) - jnp.log1p(jnp.exp(-jnp.abs(x)))


def _softplus(x):
    return jnp.maximum(x, 0.0) + jnp.log1p(jnp.exp(-jnp.abs(x)))


def _mlstm_kernel(*refs, L, nseq, has_state, ring_steps):
    qk_ref, v_ref, og_ref, sm_ref, sb_ref, hg_ref, w_ref, b_ref = refs[:8]
    refs = refs[8:]
    if has_state:
        cv0_ref, c0_ref, n0_ref, m0_ref = refs[:4]
        refs = refs[4:]
    o_ref, c_ref, n_ref, m_ref, xp_ref, qk_s = refs[:6]
    c_src = c_ref
    if ring_steps:
        c_ring, c_sem = refs[6:]
        step = pl.program_id(0)
        slot = step % C_RING_SLOTS

        def c_fetch(for_step, into_slot):
            return pltpu.make_async_copy(c0_ref.at[pl.ds(for_step * nseq, nseq)],
                                         c_ring.at[into_slot], c_sem.at[into_slot])

        @pl.when(step == 0)
        def _():
            for first in range(min(C_RING_SLOTS - 1, ring_steps)):
                c_fetch(first, first).start()

        ahead = step + (C_RING_SLOTS - 1)

        @pl.when(ahead < ring_steps)
        def _():
            c_fetch(ahead, ahead % C_RING_SLOTS).start()

        c_fetch(step, slot).wait()
        c_src = c_ring.at[slot]

    @pl.when(pl.program_id(1) == 0)
    def _():
        if has_state:
            if not ring_steps:
                c_ref[...] = c0_ref[...]
            n_ref[...] = n0_ref[...]
            m_ref[...] = m0_ref[...]
        else:
            c_ref[...] = jnp.zeros_like(c_ref)
            n_ref[...] = jnp.zeros_like(n_ref)
            m_ref[...] = jnp.zeros_like(m_ref)
        for s in range(nseq):
            _init_conv(xp_ref.at[s], cv0_ref.at[pl.ds(s, 1)] if has_state else None)

    rr = lax.broadcasted_iota(jnp.int32, (L, L), 0)
    cc = lax.broadcasted_iota(jnp.int32, (L, L), 1)
    tril = cc <= rr
    gates = []
    for s in range(nseq):
        sm = sm_ref[s] + sb_ref[...]
        lf_all = _log_sigmoid(sm)
        b_all = _dot3(tril.astype(BF16), lf_all)
        b_all_t = _tn3(lf_all, (rr <= cc).astype(BF16))
        sm_t = _tn3(sm, (rr == cc).astype(BF16))
        gates.append((sm, sm_t, b_all, b_all_t))

    def head_stages(s, h):
        sm, sm_t, b_all, b_all_t = gates[s]
        qk_seq = qk_s.at[s]
        q_slabs = range(h * M_QK // LANES, (h + 1) * M_QK // LANES)
        k_slabs = range((M_QK_W + h * M_QK) // LANES, (M_QK_W + (h + 1) * M_QK) // LANES)
        _conv_silu(xp_ref.at[s], qk_ref.at[s], w_ref, b_ref, L, qk_seq,
                   slabs=[*q_slabs, *k_slabs])
        yield
        q = jnp.concatenate([qk_seq[i] for i in q_slabs], axis=1)
        k = jnp.concatenate([qk_seq[i] for i in k_slabs], axis=1) * K_SCALE
        v = v_ref[s, :, h * M_V:(h + 1) * M_V]
        i_col = sm[:, SM_I + h:SM_I + h + 1]
        i_row = sm_t[SM_I + h:SM_I + h + 1, :]
        b_col = b_all[:, SM_F + h:SM_F + h + 1]
        b_row = b_all_t[SM_F + h:SM_F + h + 1, :]

        c_prev = c_src[s, h]
        n_prev = n_ref[s, h]
        m_prev = m_ref[s, h]

        dmat = jnp.where(tril, b_col - b_row + i_row, -jnp.inf)
        inter = b_col + m_prev
        mt = jnp.maximum(inter, jnp.max(dmat, axis=1, keepdims=True))
        w_tok = jnp.exp(dmat - mt)
        w_st = jnp.exp(inter - mt)
        yield

        qb = q.astype(BF16)
        kb = k.astype(BF16)
        vb = v.astype(BF16)
        sc = _nt_dot(qb, kb) * w_tok
        yield
        num = (jnp.dot(sc.astype(BF16), vb, preferred_element_type=F32)
               + w_st * jnp.dot(qb, c_prev.astype(BF16), preferred_element_type=F32))
        den = (jnp.sum(sc, axis=1, keepdims=True)
               + w_st * jnp.sum(q * n_prev, axis=1, keepdims=True))
        yield
        hh = num / jnp.maximum(jnp.abs(den), jnp.exp(-mt))

        m_new = mt[L - 1:L, :]
        b_last = b_col[L - 1:L, :]
        ws = jnp.exp(b_last - b_col + i_col - m_new)
        decay = jnp.exp(b_last + m_prev - m_new)
        kw = k * ws
        c_ref[s, h] = decay * c_prev + _tn_dot(kw.astype(BF16), vb)
        n_ref[s, h] = decay * n_prev + jnp.sum(kw, axis=0, keepdims=True)
        m_ref[s, h] = m_new
        yield

        mu = jnp.mean(hh, axis=-1, keepdims=True)
        hc = hh - mu
        yield
        var = jnp.mean(hc * hc, axis=-1, keepdims=True)
        hsl = slice(h * M_V, (h + 1) * M_V)
        gate = (0.5 * hg_ref[:, hsl]) * (jnp.tanh(0.5 * og_ref[s, :, hsl]) + 1.0)
        o_ref[s, :, hsl] = (hc * lax.rsqrt(var + EPS)) * gate

    streams = [head_stages(s, h) for s in range(nseq) for h in range(M_HEADS)]
    for i in range(0, len(streams), MLSTM_INTERLEAVE):
        _run_stages(streams[i:i + MLSTM_INTERLEAVE])


def _seqs_per_step(bsz, nc, short):
    want = short if nc == 1 else LONG_SEQS_PER_STEP
    return want if bsz % want == 0 else 1


def _mlstm(proj, bsz, T, m_conv_w, m_conv_b, small_bias, m_head_g, state=None):
    L = min(T, CHUNK)
    nc = T // L
    assert T == nc * L
    nseq = _seqs_per_step(bsz, nc, MLSTM_SHORT_SEQS)
    has_state = state is not None
    qkw = 2 * M_QK_W

    const = lambda b, c: (0, 0)
    per_b = lambda b, c: (b, 0, 0, 0)
    in_specs = [
        pl.BlockSpec((nseq, L, qkw), lambda b, c: (b, c, P_QK // qkw)),
        pl.BlockSpec((nseq, L, M_V_W), lambda b, c: (b, c, P_V // M_V_W)),
        pl.BlockSpec((nseq, L, M_V_W), lambda b, c: (b, c, P_O // M_V_W)),
        pl.BlockSpec((nseq, L, LANES), lambda b, c: (b, c, P_SMALL // LANES)),
        pl.BlockSpec((1, LANES), const),
        pl.BlockSpec((1, M_V_W), const),
        pl.BlockSpec((CONV_W, qkw), const),
        pl.BlockSpec((1, qkw), const),
    ]
    args = [proj, proj, proj, proj, small_bias, m_head_g.reshape(1, -1),
            m_conv_w, m_conv_b.reshape(1, -1)]
    scratch = [pltpu.VMEM((nseq, qkw // LANES, L + SUBLANES, LANES), F32),
               pltpu.VMEM((nseq, qkw // LANES, L, LANES), F32)]
    ring_steps = bsz // nseq if has_state and nc == 1 else 0
    if has_state:
        conv0, c0, n0, m0 = state
        if ring_steps:
            c_spec = pl.BlockSpec(memory_space=pl.ANY)
            scratch += [pltpu.VMEM((C_RING_SLOTS, nseq, M_HEADS, M_QK, M_V), F32),
                        pltpu.SemaphoreType.DMA((C_RING_SLOTS,))]
        else:
            c_spec = pl.BlockSpec((nseq, M_HEADS, M_QK, M_V), per_b)
        in_specs += [
            pl.BlockSpec((nseq, CONV_W - 1, qkw), lambda b, c: (b, 0, 0)),
            c_spec,
            pl.BlockSpec((nseq, M_HEADS, 1, M_QK), per_b),
            pl.BlockSpec((nseq, M_HEADS, 1, 1), per_b),
        ]
        args += [conv0, c0, n0.reshape(bsz, M_HEADS, 1, M_QK), m0.reshape(bsz, M_HEADS, 1, 1)]
    out_shape = (
        jax.ShapeDtypeStruct((bsz, T, M_V_W), F32),
        jax.ShapeDtypeStruct((bsz, M_HEADS, M_QK, M_V), F32),
        jax.ShapeDtypeStruct((bsz, M_HEADS, 1, M_QK), F32),
        jax.ShapeDtypeStruct((bsz, M_HEADS, 1, 1), F32),
    )
    out_specs = (
        pl.BlockSpec((nseq, L, M_V_W), lambda b, c: (b, c, 0)),
        pl.BlockSpec((nseq, M_HEADS, M_QK, M_V), per_b),
        pl.BlockSpec((nseq, M_HEADS, 1, M_QK), per_b),
        pl.BlockSpec((nseq, M_HEADS, 1, 1), per_b),
    )
    out, c1, n1, m1 = pl.pallas_call(
        functools.partial(_mlstm_kernel, L=L, nseq=nseq, has_state=has_state,
                          ring_steps=ring_steps),
        grid=(bsz // nseq, nc),
        in_specs=in_specs,
        out_specs=out_specs,
        out_shape=out_shape,
        scratch_shapes=scratch,
        compiler_params=pltpu.CompilerParams(
            dimension_semantics=("arbitrary", "arbitrary"), vmem_limit_bytes=VMEM_LIMIT),
        name="mlstm_state" if has_state else "mlstm",
    )(*args)
    return out, c1, n1.reshape(bsz, M_HEADS, M_QK), m1.reshape(bsz, M_HEADS)


def _ssd_kernel(*refs, L, nseq, has_state):
    n_shared = 13
    e_ref = refs[9]
    streams, shared = [], []
    for sq in range(nseq):
        one = pl.ds(sq, 1)
        views = [r.at[sq] for r in refs[:5]] + list(refs[5:n_shared])
        rest = refs[n_shared:]
        if has_state:
            views += [r.at[one] for r in rest[:2]]
            rest = rest[2:]
        o_ref, h_ref = rest[:2]
        views += [o_ref.at[sq], h_ref.at[one]] + [r.at[sq] for r in rest[2:]]
        seq_streams, seq_shared = _ssd_seq_streams(*views, L=L, has_state=has_state)
        streams += seq_streams
        shared.append(seq_shared)
    if L < LANES:
        onehot = jnp.concatenate([e_ref[p] for p in range(e_ref.shape[0])], axis=1)
        stacked = jnp.concatenate([a for sh in shared for a in sh["src"]], axis=0)
        wide = sum(jnp.dot(p, onehot, preferred_element_type=F32)
                   for p in _split3(stacked)[:2])
        for i, sh in enumerate(shared):
            sh["wide"] = [wide[(3 * i + q) * L:(3 * i + q + 1) * L] for q in range(3)]
    _run_stages(streams)


def _ssd_seq_streams(*refs, L, has_state):
    (z_ref, xs_ref, bp_ref, cp_ref, sm_ref, sb_ref, al_ref, d_ref, ng_ref, e_ref, et_ref,
     w_ref, b_ref) = refs[:13]
    refs = refs[13:]
    if has_state:
        cv0_ref, h0_ref = refs[:2]
        refs = refs[2:]
    o_ref, h_ref, xx_ref, xb_ref, xc_ref, xs_s, bm_s, cm_s = refs

    @pl.when(pl.program_id(1) == 0)
    def _():
        if has_state:
            h_ref[...] = h0_ref[...]
        else:
            h_ref[...] = jnp.zeros_like(h_ref)

    x_cols = slice(0, S_INNER)
    b_cols = slice(S_INNER, S_INNER + S_BCW)
    c_cols = slice(S_INNER + S_BCW, S_CONV_CH)
    for xp_ref, cols in ((xx_ref, x_cols), (xb_ref, b_cols), (xc_ref, c_cols)):
        @pl.when(pl.program_id(1) == 0)
        def _():
            _init_conv(xp_ref, cv0_ref.at[:, :, cols] if has_state else None)

    _conv_silu(xb_ref, bp_ref, w_ref.at[:, b_cols], b_ref.at[:, b_cols], L, bm_s)
    _conv_silu(xc_ref, cp_ref, w_ref.at[:, c_cols], b_ref.at[:, c_cols], L, cm_s)

    dt_all = _softplus(sm_ref[...] + sb_ref[...])
    da_all = dt_all * (-jnp.exp(al_ref[...]))
    lane = lax.broadcasted_iota(jnp.int32, (L, LANES), 1)
    rr = lax.broadcasted_iota(jnp.int32, (L, L), 0)
    cc = lax.broadcasted_iota(jnp.int32, (L, L), 1)
    tril = cc <= rr
    lo = lane < S_HEADDIM
    cum_all = _dot3(tril.astype(BF16), da_all)
    cum_all_t = _tn3(da_all, (rr <= cc).astype(BF16))
    cum2, cum2_t = cum_all * LOG2_E, cum_all_t * LOG2_E
    cum_last = cum_all[L - 1:L, :]
    dec_all = dt_all * jnp.exp(cum_last - cum_all)
    ecum_all = jnp.exp(cum_all)
    mxu_spread = L >= LANES
    shared = {"src": (dt_all, dec_all, ecum_all), "wide": None}
    if mxu_spread:
        dt_src = _split3(dt_all)[:2]
        dec_src = _split3(dec_all)[:2]
        ecum_src = _split3(ecum_all)[:2]
        clast_src = _split3(jnp.broadcast_to(cum_all_t[:, L - 1:L], (LANES, LANES)))
    else:
        dt_src, dec_src, ecum_src = 0, 1, 2
        lo_row = lax.broadcasted_iota(jnp.int32, (LANES, 1), 0) < S_HEADDIM

    def spread(src, pair):
        if mxu_spread:
            return sum(jnp.dot(p, e_ref[pair], preferred_element_type=F32) for p in src)
        return shared["wide"][src][:, pair * LANES:(pair + 1) * LANES]

    def state_decay(pair):
        if mxu_spread:
            return jnp.exp(sum(jnp.dot(et_ref[pair], p, preferred_element_type=F32)
                               for p in clast_src))
        idx = SM_DT + 2 * pair
        return jnp.exp(jnp.where(lo_row, cum_last[:, idx:idx + 1], cum_last[:, idx + 1:idx + 2]))

    def group_stages(g):
        _conv_silu(xx_ref, xs_ref, w_ref.at[:, x_cols], b_ref.at[:, x_cols], L, xs_s,
                   slabs=range(g * S_GW // LANES, (g + 1) * S_GW // LANES))
        yield
        bmb = bm_s[g].astype(BF16)
        cmb = cm_s[g].astype(BF16)
        cb = _nt_dot(cmb, bmb)
        ssq = jnp.zeros((L, 1), F32)
        for p in range(S_HPG // 2):
            yield
            pair = g * (S_HPG // 2) + p
            psl = slice(g * S_GW + p * LANES, g * S_GW + (p + 1) * LANES)
            hsl = slice(p * LANES, (p + 1) * LANES)
            lmat = []
            for e in range(2):
                idx = SM_DT + 2 * pair + e
                lmat.append(jnp.exp2(jnp.where(
                    tril, cum2[:, idx:idx + 1] - cum2_t[idx:idx + 1, :], -jnp.inf)))
            yield
            xs = xs_s[pair]
            xdt = xs * spread(dt_src, pair)
            xdd = xs * spread(dec_src, pair)
            h_prev = h_ref[0, g, hsl, :]
            yield
            xdt_b = xdt.astype(BF16)
            y = (jnp.where(lo,
                           jnp.dot((cb * lmat[0]).astype(BF16), xdt_b,
                                   preferred_element_type=F32),
                           jnp.dot((cb * lmat[1]).astype(BF16), xdt_b,
                                   preferred_element_type=F32))
                 + spread(ecum_src, pair) * _nt_dot(cmb, h_prev.astype(BF16)))
            h_ref[0, g, hsl, :] = state_decay(pair) * h_prev + _tn_dot(xdd.astype(BF16), bmb)
            yield
            gated = (y + d_ref[:, psl] * xs) * _silu(z_ref[:, psl])
            ssq = ssq + jnp.sum(gated * gated, axis=1, keepdims=True)
            o_ref[:, psl] = gated
        yield
        gsl = slice(g * S_GW, (g + 1) * S_GW)
        o_ref[:, gsl] = o_ref[:, gsl] * lax.rsqrt(ssq * (1.0 / S_GW) + EPS) * ng_ref[:, gsl]

    return [group_stages(g) for g in range(S_GROUPS)], shared


def _pair_onehots():
    n_pairs = S_HEADS // 2
    pair = jnp.arange(n_pairs)[:, None, None]
    src = jnp.arange(LANES)[None, :, None]
    dst = jnp.arange(LANES)[None, None, :]
    onehot = src == SM_DT + 2 * pair + dst // S_HEADDIM
    return onehot.astype(BF16), jnp.swapaxes(onehot, 1, 2).astype(BF16)


def _ssd(proj, bsz, T, s_conv_w, s_conv_b, small_bias, alog_row, d_row, s_norm_g, state=None):
    L = min(T, CHUNK)
    nc = T // L
    assert T == nc * L
    nseq = _seqs_per_step(bsz, nc, SSD_SHORT_SEQS)
    has_state = state is not None
    n_pairs = S_HEADS // 2
    onehot, onehot_t = _pair_onehots()

    const = lambda b, c: (0, 0)
    const3 = lambda b, c: (0, 0, 0)
    in_specs = [
        pl.BlockSpec((nseq, L, S_INNER), lambda b, c: (b, c, P_Z // S_INNER)),
        pl.BlockSpec((nseq, L, S_INNER), lambda b, c: (b, c, P_XBC // S_INNER)),
        pl.BlockSpec((nseq, L, S_BCW), lambda b, c: (b, c, P_B // S_BCW)),
        pl.BlockSpec((nseq, L, S_BCW), lambda b, c: (b, c, P_C // S_BCW)),
        pl.BlockSpec((nseq, L, LANES), lambda b, c: (b, c, P_SMALL // LANES)),
        pl.BlockSpec((1, LANES), const),
        pl.BlockSpec((1, LANES), const),
        pl.BlockSpec((1, S_INNER), const),
        pl.BlockSpec((1, S_INNER), const),
        pl.BlockSpec((n_pairs, LANES, LANES), const3),
        pl.BlockSpec((n_pairs, LANES, LANES), const3),
        pl.BlockSpec((CONV_W, S_CONV_CH), const),
        pl.BlockSpec((1, S_CONV_CH), const),
    ]
    args = [proj, proj, proj, proj, proj, small_bias, alog_row, d_row, s_norm_g.reshape(1, -1),
            onehot, onehot_t, s_conv_w, s_conv_b.reshape(1, -1)]
    scratch = [pltpu.VMEM((nseq, width // LANES, rows, LANES), F32)
               for rows in (L + SUBLANES, L) for width in (S_INNER, S_BCW, S_BCW)]
    if has_state:
        conv0, h0 = state
        in_specs += [
            pl.BlockSpec((nseq, CONV_W - 1, S_CONV_CH), lambda b, c: (b, 0, 0)),
            pl.BlockSpec((nseq, S_GROUPS, S_GW, S_STATE), lambda b, c: (b, 0, 0, 0)),
        ]
        args += [conv0, h0.reshape(bsz, S_GROUPS, S_GW, S_STATE)]
    out, h1 = pl.pallas_call(
        functools.partial(_ssd_kernel, L=L, nseq=nseq, has_state=has_state),
        grid=(bsz // nseq, nc),
        in_specs=in_specs,
        out_specs=(
            pl.BlockSpec((nseq, L, S_INNER), lambda b, c: (b, c, 0)),
            pl.BlockSpec((nseq, S_GROUPS, S_GW, S_STATE), lambda b, c: (b, 0, 0, 0)),
        ),
        out_shape=(
            jax.ShapeDtypeStruct((bsz, T, S_INNER), F32),
            jax.ShapeDtypeStruct((bsz, S_GROUPS, S_GW, S_STATE), F32),
        ),
        scratch_shapes=scratch,
        compiler_params=pltpu.CompilerParams(
            dimension_semantics=("arbitrary", "arbitrary"), vmem_limit_bytes=VMEM_LIMIT),
        name="ssd_state" if has_state else "ssd",
    )(*args)
    return out, h1.reshape(bsz, S_HEADS, S_HEADDIM, S_STATE)


def _merge_kernel(x_ref, hm_ref, gt_ref, ga_ref, gb_ref, g_ref, wpa_ref, wpb_ref, wo_ref, o_ref,
                  wpa_s, wpb_s, wo_s):
    @pl.when(pl.program_id(0) == 0)
    def _():
        wpa_s[...] = wpa_ref[...].astype(BF16)
        wpb_s[...] = wpb_ref[...].astype(BF16)
        wo_s[...] = wo_ref[...].astype(BF16)

    ya = jnp.dot(hm_ref[...].astype(BF16), wpa_s[...], preferred_element_type=F32)
    yb = jnp.dot(gt_ref[...].astype(BF16), wpb_s[...], preferred_element_type=F32)
    merged = _sigmoid(ga_ref[...]) * ya + _sigmoid(gb_ref[...]) * yb
    o_ref[...] = x_ref[...] + _mod_rows(g_ref, x_ref.shape[0]) * jnp.dot(
        merged.astype(BF16), wo_s[...], preferred_element_type=F32)


def _merge(x, hm, gated, proj, mod3, rows_per_group, kmod, w_pa, w_pb, w_out):
    n = x.shape[0]
    tm = MERGE_TM
    assert n % tm == 0 and (rows_per_group % tm == 0 or tm % rows_per_group == 0)
    const = lambda i: (0, 0)
    resident = functools.partial(pl.BlockSpec, index_map=const, pipeline_mode=pl.Buffered(1))
    return pl.pallas_call(
        _merge_kernel,
        grid=(n // tm,),
        in_specs=[
            pl.BlockSpec((tm, D_MODEL), lambda i: (i, 0)),
            pl.BlockSpec((tm, M_V_W), lambda i: (i, 0)),
            pl.BlockSpec((tm, S_INNER), lambda i: (i, 0)),
            pl.BlockSpec((tm, D_MODEL), lambda i: (i, P_GA // D_MODEL)),
            pl.BlockSpec((tm, D_MODEL), lambda i: (i, P_GB // D_MODEL)),
            _mod_spec(rows_per_group, tm,kmod),
            resident((M_V_W, D_MODEL)),
            resident((S_INNER, D_MODEL)),
            resident((D_MODEL, D_MODEL)),
        ],
        out_specs=pl.BlockSpec((tm, D_MODEL), lambda i: (i, 0)),
        out_shape=jax.ShapeDtypeStruct((n, D_MODEL), F32),
        scratch_shapes=[pltpu.VMEM((M_V_W, D_MODEL), BF16),
                        pltpu.VMEM((S_INNER, D_MODEL), BF16),
                        pltpu.VMEM((D_MODEL, D_MODEL), BF16)],
        compiler_params=pltpu.CompilerParams(
            dimension_semantics=("arbitrary",), vmem_limit_bytes=VMEM_LIMIT),
        name="merge",
    )(x, hm, gated, proj, proj, mod3, w_pa, w_pb, w_out)


def _layer(x, mod3, rows_per_group, bsz, T, wts, final_norm, state):
    n = bsz * T
    x = x.reshape(n, D_MODEL)
    x = _ffn(x, mod3, rows_per_group, 0, wts["norm_ff1"], wts["ff1_w1"], wts["ff1_w3"],
             wts["ff1_w2"])
    proj = _proj(x, mod3, rows_per_group, 3, wts["norm_mix"], wts["w_packed"])
    proj3 = proj.reshape(bsz, T, P_WIDTH)
    mixer_args = (proj3, bsz, T)
    m_args = (wts["m_conv_w"], wts["m_conv_b"], wts["small_bias"], wts["m_head_g"])
    s_args = (wts["s_conv_w"], wts["s_conv_b"], wts["small_bias"], wts["alog_row"],
              wts["d_row"], wts["s_norm_g"])
    if state is None:
        m_state = s_state = None
    else:
        c0, n0, m0, mconv0, ssm0, sconv0 = state
        m_state = (mconv0, c0, n0, m0)
        s_state = (sconv0, ssm0)
    hm, c1, n1, m1 = _mlstm(*mixer_args, *m_args, state=m_state)
    gated, h1 = _ssd(*mixer_args, *s_args, state=s_state)
    x = _merge(x, hm.reshape(n, M_V_W), gated.reshape(n, S_INNER), proj, mod3, rows_per_group,
               5, wts["w_pa"], wts["w_pb"], wts["w_out"])
    y = _ffn(x, mod3, rows_per_group, 6, wts["norm_ff2"], wts["ff2_w1"], wts["ff2_w3"],
             wts["ff2_w2"], final_norm=final_norm)
    last = proj3[:, T - (CONV_W - 1):]
    mconv1 = last[:, :, P_QK:P_QK + 2 * M_QK_W]
    sconv1 = last[:, :, P_XBC:P_XBC + S_CONV_CH]
    return y.reshape(bsz, T, D_MODEL), (c1, n1, m1, mconv1, h1, sconv1)


def kernel(x_prompt, x_sample, c_prompt, c_sample, state_mlstm_C, state_mlstm_n, state_mlstm_m, state_mlstm_conv, state_ssm, state_ssm_conv, ada_w, ada_b, norm_ff1, ff1_w1, ff1_w3, ff1_w2, norm_mix, w_in, m_conv_w, m_conv_b, m_if_b, m_head_g, w_pa, s_conv_w, s_conv_b, s_dt_bias, s_A_log, s_D, s_norm_g, w_pb, w_out, norm_ff2, ff2_w1, ff2_w3, ff2_w2, final_norm):
    bp, tp, _ = x_prompt.shape
    bs, ts, _ = x_sample.shape
    depth = ada_w.shape[0]
    assert depth == 1

    hp, hs = x_prompt, x_sample
    new_p, new_s = [], []
    for l in range(depth):
        c_all = jnp.concatenate([c_prompt, c_sample], axis=0)
        pad_rows = (-c_all.shape[0]) % (2 * SUBLANES)
        c_all = jnp.pad(c_all, ((0, pad_rows), (0, 0)))
        mod = _ada(c_all, ada_w[l], ada_b[l])
        mod_p = mod[:bp].reshape(bp, 1, N_MOD * D_MODEL)
        mod_s = mod[bp:bp + bs].reshape(bs, 1, N_MOD * D_MODEL)

        lane_pad = LANES - (2 * M_HEADS + S_HEADS)
        small_bias = jnp.concatenate(
            [m_if_b[l], s_dt_bias[l], jnp.zeros((lane_pad,), F32)]).reshape(1, LANES)
        alog_row = jnp.concatenate(
            [jnp.zeros((2 * M_HEADS,), F32), s_A_log[l], jnp.zeros((lane_pad,), F32)]
        ).reshape(1, LANES)
        wts = {
            "norm_ff1": norm_ff1[l], "ff1_w1": ff1_w1[l], "ff1_w3": ff1_w3[l],
            "ff1_w2": ff1_w2[l],
            "norm_mix": norm_mix[l], "w_packed": _pack_w_in(jnp.swapaxes(w_in, 1, 2)[l]),
            "m_conv_w": m_conv_w[l], "m_conv_b": m_conv_b[l], "small_bias": small_bias,
            "m_head_g": m_head_g[l], "w_pa": w_pa[l],
            "s_conv_w": s_conv_w[l], "s_conv_b": s_conv_b[l], "alog_row": alog_row,
            "d_row": jnp.repeat(s_D[l], S_HEADDIM).reshape(1, S_INNER),
            "s_norm_g": s_norm_g[l], "w_pb": w_pb[l], "w_out": w_out[l],
            "norm_ff2": norm_ff2[l], "ff2_w1": ff2_w1[l], "ff2_w3": ff2_w3[l],
            "ff2_w2": ff2_w2[l],
        }
        st_s = (state_mlstm_C[l], state_mlstm_n[l], state_mlstm_m[l], state_mlstm_conv[l],
                state_ssm[l], state_ssm_conv[l])
        hp, sp = _layer(hp, mod_p, tp, bp, tp, wts, final_norm, None)
        hs, ss = _layer(hs, mod_s, ts, bs, ts, wts, final_norm, st_s)
        new_p.append(sp)
        new_s.append(ss)
    outs_p = [jnp.stack([s[i] for s in new_p]) for i in range(6)]
    outs_s = [jnp.stack([s[i] for s in new_s]) for i in range(6)]
    return (hp, hs, *outs_p, *outs_s)
```
